```python
import math
import jax, jax.numpy as jnp
from jax import lax
import numpy as np

D_MODEL = 2048
BATCH = 2
SEQ = 4096
DEPTH = 4
DEC_BATCH = 8
DEC_SEQ = 1
PAST_LEN = 16384
PAGE_SIZE = 128

RET_HEADS = 4
RET_DK = 256
RET_DV = 512
RET_CHUNK = 128
ROPE_BASE = 10000.0
S5_WIDTH = D_MODEL // 2
S5_GROUP = 16
S5_GROUPS = S5_WIDTH // S5_GROUP
S5_STATE = 64
S5_DT_MIN = 0.001
S5_DT_MAX = 0.1
DSA_HEADS = 8
DSA_KV_HEADS = 2
DSA_HEAD_DIM = 128
IDX_HEADS = 8
IDX_DIM = 64
DSA_TOPK = 256
Q_BLOCK = 128
D_FF = ((8 * D_MODEL // 3 + 255) // 256) * 256
LN_EPS = 1e-5
GN_EPS = 1e-5
DEEPNORM_ALPHA = (2 * DEPTH) ** 0.25
DEEPNORM_BETA = (8 * DEPTH) ** -0.25
N_BRANCHES = 3
IN_SPLITS = (RET_HEADS * RET_DK, RET_HEADS * RET_DK, RET_HEADS * RET_DV, RET_HEADS * RET_DV,
             S5_WIDTH,
             DSA_HEADS * DSA_HEAD_DIM, DSA_KV_HEADS * DSA_HEAD_DIM, DSA_KV_HEADS * DSA_HEAD_DIM,
             IDX_HEADS * IDX_DIM, IDX_DIM, IDX_HEADS,
             N_BRANCHES * D_MODEL)

kernel_name = 'hybrid_retention_s5_dsa_macaron_step'


def _layernorm(x, g, b):
    xf = x.astype(jnp.float32)
    mu = jnp.mean(xf, -1, keepdims=True)
    var = jnp.mean(jnp.square(xf - mu), -1, keepdims=True)
    return ((xf - mu) * lax.rsqrt(var + LN_EPS) * g + b).astype(x.dtype)


def _residual_ln(x, f, g, b):
    return _layernorm(DEEPNORM_ALPHA * x + f, g, b)


def _swiglu(x, wg, wu, wd):
    return (jax.nn.silu(x @ wg) * (x @ wu)) @ wd


def _rotary(x, pos):
    half = x.shape[-1] // 2
    inv = 1.0 / (ROPE_BASE ** jnp.linspace(0.0, 1.0, half, dtype=jnp.float32))
    ang = pos.astype(jnp.float32)[:, None] * inv[None, :]
    cos = jnp.cos(ang)[None, :, None, :]
    sin = jnp.sin(ang)[None, :, None, :]
    xf = x.astype(jnp.float32)
    x1, x2 = xf[..., :half], xf[..., half:]
    return jnp.concatenate([x1 * cos - x2 * sin, x1 * sin + x2 * cos], -1)


def _retention(q, k, v, s0, chunk):
    bn, length, nh, _ = q.shape
    dv = v.shape[-1]
    n_chunks = length // chunk
    lg = jnp.log(1.0 - 2.0 ** (-5.0 - jnp.arange(nh, dtype=jnp.float32)))
    i = jnp.arange(chunk, dtype=jnp.float32)
    diff = i[:, None] - i[None, :]
    dmask = jnp.where(diff[None] >= 0, jnp.exp(jnp.maximum(diff, 0.0)[None] * lg[:, None, None]), 0.0)
    q_dec = jnp.exp((i[:, None] + 1.0) * lg[None, :])
    k_dec = jnp.exp((chunk - 1.0 - i)[:, None] * lg[None, :])
    c_dec = jnp.exp(chunk * lg)

    def blocks(a):
        return jnp.moveaxis(a.astype(jnp.float32).reshape((bn, n_chunks, chunk) + a.shape[2:]), 1, 0)

    def step(s, inp):
        qc, kc, vc = inp
        sc = jnp.einsum('bihd,bjhd->bhij', qc, kc) * dmask[None]
        o = (jnp.einsum('bhij,bjhv->bihv', sc, vc)
             + jnp.einsum('bihd,bhdv->bihv', qc, s) * q_dec[None, :, :, None])
        s = c_dec[None, :, None, None] * s + jnp.einsum('bjhd,bjhv->bhdv', kc * k_dec[None, :, :, None], vc)
        return s, o

    s_fin, o = lax.scan(step, s0.astype(jnp.float32), (blocks(q), blocks(k), blocks(v)))
    return jnp.moveaxis(o, 0, 1).reshape(bn, length, nh, dv), s_fin


def _group_norm(o):
    mu = jnp.mean(o, -1, keepdims=True)
    var = jnp.mean(jnp.square(o - mu), -1, keepdims=True)
    return (o - mu) * lax.rsqrt(var + GN_EPS)


def _s5(u, h0_re, h0_im, a_re, a_im, log_dt, b_re, b_im, c_re, c_im, d):
    u = u.astype(jnp.float32)
    dt = jnp.exp(log_dt)[:, None]
    mag = jnp.exp(a_re * dt)
    ab_re = mag * jnp.cos(a_im * dt)
    ab_im = mag * jnp.sin(a_im * dt)
    den = a_re * a_re + a_im * a_im
    x_re = ab_re - 1.0
    f_re = (x_re * a_re + ab_im * a_im) / den
    f_im = (ab_im * a_re - x_re * a_im) / den
    bb_re = f_re[..., None] * b_re - f_im[..., None] * b_im
    bb_im = f_re[..., None] * b_im + f_im[..., None] * b_re
    bu_re = jnp.einsum('blgc,gpc->blgp', u, bb_re)
    bu_im = jnp.einsum('blgc,gpc->blgp', u, bb_im)
    bu_re = bu_re.at[:, 0].add(ab_re * h0_re - ab_im * h0_im)
    bu_im = bu_im.at[:, 0].add(ab_re * h0_im + ab_im * h0_re)
    el_re = jnp.broadcast_to(ab_re, bu_re.shape)
    el_im = jnp.broadcast_to(ab_im, bu_im.shape)

    def combine(e1, e2):
        a1r, a1i, b1r, b1i = e1
        a2r, a2i, b2r, b2i = e2
        return (a2r * a1r - a2i * a1i, a2r * a1i + a2i * a1r,
                a2r * b1r - a2i * b1i + b2r, a2r * b1i + a2i * b1r + b2i)

    _, _, h_re, h_im = lax.associative_scan(combine, (el_re, el_im, bu_re, bu_im), axis=1)
    y = (jnp.einsum('blgp,gcp->blgc', h_re, c_re) - jnp.einsum('blgp,gcp->blgc', h_im, c_im)
         + d[None, None] * u)
    return y, h_re[:, -1], h_im[:, -1]


def _indexer_scores(qi, wi, ki):
    s = jax.nn.relu(jnp.einsum('bqhd,bsd->bqhs', qi.astype(jnp.float32), ki.astype(jnp.float32)))
    return jnp.einsum('bqhs,bqh->bqs', s, wi.astype(jnp.float32))


def _sparse_attend(q, kg, vg, valid):
    bn, nq = q.shape[:2]
    qg = q.astype(jnp.float32).reshape(bn, nq, DSA_KV_HEADS, DSA_HEADS // DSA_KV_HEADS, DSA_HEAD_DIM)
    logits = jnp.einsum('bqkgd,bqskd->bqkgs', qg, kg.astype(jnp.float32)) * (DSA_HEAD_DIM ** -0.5)
    logits = jnp.where(valid[:, :, None, None, :], logits, -jnp.inf)
    p = jax.nn.softmax(logits, axis=-1)
    o = jnp.einsum('bqkgs,bqskd->bqkgd', p, vg.astype(jnp.float32))
    return o.reshape(bn, nq, DSA_HEADS * DSA_HEAD_DIM).astype(q.dtype)


def _take_rows(a, idx):
    return jax.vmap(lambda aa, ii: aa[ii])(a, idx)


def _dsa_prompt(q, k, v, qi, ki, wi):
    bn, length = q.shape[:2]
    n_sel = min(DSA_TOPK, length // 4)
    qb = Q_BLOCK if length % Q_BLOCK == 0 else length
    nb = length // qb
    key_pos = jnp.arange(length)

    def to_blocks(a):
        return jnp.moveaxis(a.reshape((bn, nb, qb) + a.shape[2:]), 1, 0)

    def block(args):
        q_b, qi_b, wi_b, start = args
        qpos = start + jnp.arange(qb)
        sc = _indexer_scores(qi_b, wi_b, ki)
        sc = jnp.where(key_pos[None, None, :] <= qpos[None, :, None], sc, -jnp.inf)
        _, idx = lax.top_k(sc, n_sel)
        return _sparse_attend(q_b, _take_rows(k, idx), _take_rows(v, idx), idx <= qpos[None, :, None])

    out = lax.map(block, (to_blocks(q), to_blocks(qi), to_blocks(wi), jnp.arange(nb) * qb))
    return jnp.moveaxis(out, 0, 1).reshape(bn, length, DSA_HEADS * DSA_HEAD_DIM)


def _dsa_sample(q, k, v, qi, ki, wi, ck, cv, cki, page_table):
    bn, ns = q.shape[:2]
    past = page_table.shape[1] * PAGE_SIZE
    n_sel = min(DSA_TOPK, (past + ns) // 4)
    ki_past = cki[page_table].reshape(bn, past, IDX_DIM)
    ki_all = jnp.concatenate([ki_past, ki.astype(ki_past.dtype)], axis=1)
    qpos = past + jnp.arange(ns)
    key_pos = jnp.arange(past + ns)
    sc = _indexer_scores(qi, wi, ki_all)
    sc = jnp.where(key_pos[None, None, :] <= qpos[None, :, None], sc, -jnp.inf)
    _, idx = lax.top_k(sc, n_sel)
    from_past = (idx < past)[..., None, None]
    ip = jnp.minimum(idx, past - 1)
    phys = jax.vmap(lambda pt, pg: pt[pg])(page_table, ip // PAGE_SIZE)
    off = ip % PAGE_SIZE
    inew = jnp.clip(idx - past, 0, ns - 1)
    kg = jnp.where(from_past, ck[phys, off], _take_rows(k, inew))
    vg = jnp.where(from_past, cv[phys, off], _take_rows(v, inew))
    return _sparse_attend(q, kg, vg, idx <= qpos[None, :, None])


def _mixer(x, pos, ret_s0, s5_h0_re, s5_h0_im, dsa_cache, page_table, mw):
    (w_in, a_re, a_im, log_dt, b_re, b_im, c_re, c_im, s5_d, w_glu, w_ret_o, w_s5_o, w_dsa_o, w_out) = mw
    bn, length, _ = x.shape
    split_at = [int(c) for c in np.cumsum(IN_SPLITS)[:-1]]
    (rq, rk, rv, rg, su, dq, dk, dv, iq, ik, iw, gates) = jnp.split(x @ w_in, split_at, axis=-1)

    q = _rotary(rq.reshape(bn, length, RET_HEADS, RET_DK), pos)
    k = _rotary(rk.reshape(bn, length, RET_HEADS, RET_DK), pos) * (RET_DK ** -0.5)
    v = rv.reshape(bn, length, RET_HEADS, RET_DV)
    chunk = RET_CHUNK if length % RET_CHUNK == 0 else length
    o, ret_s = _retention(q, k, v, ret_s0, chunk)
    o = _group_norm(o).reshape(bn, length, RET_HEADS * RET_DV)
    ret_out = (jax.nn.silu(rg.astype(jnp.float32)) * o).astype(x.dtype) @ w_ret_o

    y, s5_re, s5_im = _s5(su.reshape(bn, length, S5_GROUPS, S5_GROUP), s5_h0_re, s5_h0_im,
                          a_re, a_im, log_dt, b_re, b_im, c_re, c_im, s5_d)
    y = jax.nn.gelu(y.reshape(bn, length, S5_WIDTH)).astype(x.dtype)
    s5_out = (y * jax.nn.sigmoid(y @ w_glu)) @ w_s5_o

    dq = dq.reshape(bn, length, DSA_HEADS, DSA_HEAD_DIM)
    dk = dk.reshape(bn, length, DSA_KV_HEADS, DSA_HEAD_DIM)
    dv = dv.reshape(bn, length, DSA_KV_HEADS, DSA_HEAD_DIM)
    iq = iq.reshape(bn, length, IDX_HEADS, IDX_DIM)
    if dsa_cache is None:
        att = _dsa_prompt(dq, dk, dv, iq, ik, iw)
    else:
        att = _dsa_sample(dq, dk, dv, iq, ik, iw, dsa_cache[0], dsa_cache[1], dsa_cache[2], page_table)
    dsa_out = att @ w_dsa_o

    g = jax.nn.sigmoid(gates.reshape(bn, length, N_BRANCHES, D_MODEL).astype(jnp.float32))
    merged = g[:, :, 0] * ret_out + g[:, :, 1] * s5_out + g[:, :, 2] * dsa_out
    out = merged.astype(x.dtype) @ w_out
    return out, (dk, dv, ik, ret_s, s5_re, s5_im)


def setup_inputs(seed: int = 0) -> dict:
    key = jax.random.key(seed)
    ks = jax.random.split(key, 40)
    f32 = jnp.float32
    n_pages = PAST_LEN // PAGE_SIZE
    n_used = DEC_BATCH * n_pages
    n_pool = n_used + n_used // 4
    d_in = sum(IN_SPLITS)
    L = DEPTH

    def nrm(k, shape, scale):
        return jax.random.normal(k, shape, f32) * scale

    page_table = jax.random.permutation(ks[8], n_pool)[:n_used].reshape(DEC_BATCH, n_pages).astype(jnp.int32)
    s5_n = jnp.arange(S5_STATE, dtype=f32)
    return {
        'x_prompt': nrm(ks[0], (BATCH, SEQ, D_MODEL), 1.0),
        'x_sample': nrm(ks[1], (DEC_BATCH, DEC_SEQ, D_MODEL), 1.0),
        'cache_k': nrm(ks[2], (L, n_pool, PAGE_SIZE, DSA_KV_HEADS, DSA_HEAD_DIM), 1.0),
        'cache_v': nrm(ks[3], (L, n_pool, PAGE_SIZE, DSA_KV_HEADS, DSA_HEAD_DIM), 1.0),
        'cache_idx_k': nrm(ks[4], (L, n_pool, PAGE_SIZE, IDX_DIM), 1.0),
        'state_ret': nrm(ks[5], (L, DEC_BATCH, RET_HEADS, RET_DK, RET_DV), 0.5),
        'state_s5_re': nrm(ks[6], (L, DEC_BATCH, S5_GROUPS, S5_STATE), 0.5),
        'state_s5_im': nrm(ks[7], (L, DEC_BATCH, S5_GROUPS, S5_STATE), 0.5),
        'page_table': page_table,
        'ln1_g': 1.0 + nrm(ks[9], (L, D_MODEL), 0.02),
        'ln1_b': nrm(ks[10], (L, D_MODEL), 0.02),
        'ffn1_wg': nrm(ks[11], (L, D_MODEL, D_FF), D_MODEL ** -0.5),
        'ffn1_wu': nrm(ks[12], (L, D_MODEL, D_FF), D_MODEL ** -0.5),
        'ffn1_wd': nrm(ks[13], (L, D_FF, D_MODEL), DEEPNORM_BETA * D_FF ** -0.5),
        'w_in': nrm(ks[14], (L, D_MODEL, d_in), D_MODEL ** -0.5),
        's5_a_re': -0.5 + nrm(ks[15], (L, S5_GROUPS, S5_STATE), 0.01),
        's5_a_im': math.pi * s5_n + nrm(ks[16], (L, S5_GROUPS, S5_STATE), 0.01),
        's5_log_dt': jax.random.uniform(ks[17], (L, S5_GROUPS), f32, math.log(S5_DT_MIN), math.log(S5_DT_MAX)),
        's5_b_re': nrm(ks[18], (L, S5_GROUPS, S5_STATE, S5_GROUP), (2 * S5_GROUP) ** -0.5),
        's5_b_im': nrm(ks[19], (L, S5_GROUPS, S5_STATE, S5_GROUP), (2 * S5_GROUP) ** -0.5),
        's5_c_re': nrm(ks[20], (L, S5_GROUPS, S5_GROUP, S5_STATE), (2 * S5_STATE) ** -0.5),
        's5_c_im': nrm(ks[21], (L, S5_GROUPS, S5_GROUP, S5_STATE), (2 * S5_STATE) ** -0.5),
        's5_d': nrm(ks[22], (L, S5_GROUPS, S5_GROUP), 1.0),
        'w_glu': nrm(ks[23], (L, S5_WIDTH, S5_WIDTH), S5_WIDTH ** -0.5),
        'w_ret_o': nrm(ks[24], (L, RET_HEADS * RET_DV, D_MODEL), (RET_HEADS * RET_DV) ** -0.5),
        'w_s5_o': nrm(ks[25], (L, S5_WIDTH, D_MODEL), S5_WIDTH ** -0.5),
        'w_dsa_o': nrm(ks[26], (L, DSA_HEADS * DSA_HEAD_DIM, D_MODEL), (DSA_HEADS * DSA_HEAD_DIM) ** -0.5),
        'w_out': nrm(ks[27], (L, D_MODEL, D_MODEL), DEEPNORM_BETA * D_MODEL ** -0.5),
        'ln2_g': 1.0 + nrm(ks[28], (L, D_MODEL), 0.02),
        'ln2_b': nrm(ks[29], (L, D_MODEL), 0.02),
        'ffn2_wg': nrm(ks[30], (L, D_MODEL, D_FF), D_MODEL ** -0.5),
        'ffn2_wu': nrm(ks[31], (L, D_MODEL, D_FF), D_MODEL ** -0.5),
        'ffn2_wd': nrm(ks[32], (L, D_FF, D_MODEL), DEEPNORM_BETA * D_FF ** -0.5),
        'ln3_g': 1.0 + nrm(ks[33], (L, D_MODEL), 0.02),
        'ln3_b': nrm(ks[34], (L, D_MODEL), 0.02),
    }


def reference(x_prompt, x_sample, cache_k, cache_v, cache_idx_k, state_ret, state_s5_re, state_s5_im, page_table,
              ln1_g, ln1_b, ffn1_wg, ffn1_wu, ffn1_wd, w_in, s5_a_re, s5_a_im, s5_log_dt, s5_b_re, s5_b_im,
              s5_c_re, s5_c_im, s5_d, w_glu, w_ret_o, w_s5_o, w_dsa_o, w_out, ln2_g, ln2_b,
              ffn2_wg, ffn2_wu, ffn2_wd, ln3_g, ln3_b):
    bp, lp = x_prompt.shape[:2]
    ls = x_sample.shape[1]
    past = page_table.shape[1] * PAGE_SIZE
    pos_p = jnp.arange(lp)
    pos_s = past + jnp.arange(ls)
    ret0 = jnp.zeros((bp, RET_HEADS, RET_DK, RET_DV), jnp.float32)
    s50 = jnp.zeros((bp, S5_GROUPS, S5_STATE), jnp.float32)
    xp, xs = x_prompt, x_sample
    new_p, new_s = [], []
    for l in range(DEPTH):
        mw = (w_in[l], s5_a_re[l], s5_a_im[l], s5_log_dt[l], s5_b_re[l], s5_b_im[l], s5_c_re[l], s5_c_im[l],
              s5_d[l], w_glu[l], w_ret_o[l], w_s5_o[l], w_dsa_o[l], w_out[l])
        xp = _residual_ln(xp, 0.5 * _swiglu(xp, ffn1_wg[l], ffn1_wu[l], ffn1_wd[l]), ln1_g[l], ln1_b[l])
        xs = _residual_ln(xs, 0.5 * _swiglu(xs, ffn1_wg[l], ffn1_wu[l], ffn1_wd[l]), ln1_g[l], ln1_b[l])
        mp, sp = _mixer(xp, pos_p, ret0, s50, s50, None, None, mw)
        ms, ss = _mixer(xs, pos_s, state_ret[l], state_s5_re[l], state_s5_im[l],
                        (cache_k[l], cache_v[l], cache_idx_k[l]), page_table, mw)
        xp = _residual_ln(xp, mp, ln2_g[l], ln2_b[l])
        xs = _residual_ln(xs, ms, ln2_g[l], ln2_b[l])
        xp = _residual_ln(xp, 0.5 * _swiglu(xp, ffn2_wg[l], ffn2_wu[l], ffn2_wd[l]), ln3_g[l], ln3_b[l])
        xs = _residual_ln(xs, 0.5 * _swiglu(xs, ffn2_wg[l], ffn2_wu[l], ffn2_wd[l]), ln3_g[l], ln3_b[l])
        new_p.append(sp)
        new_s.append(ss)
    k_p, v_p, ik_p, ret_p, s5r_p, s5i_p = [jnp.stack(a) for a in zip(*new_p)]
    k_s, v_s, ik_s, ret_s, s5r_s, s5i_s = [jnp.stack(a) for a in zip(*new_s)]
    return (xp, xs, k_p, v_p, ik_p, k_s, v_s, ik_s, ret_p, ret_s, s5r_p, s5i_p, s5r_s, s5i_s)
```

```python
import functools
import math

import numpy as np
import jax
import jax.numpy as jnp
from jax import lax
from jax.experimental import pallas as pl
from jax.experimental.pallas import tpu as pltpu

F32 = jnp.float32
BF16 = jnp.bfloat16
I32 = jnp.int32

PAGE_SIZE = 128
RET_HEADS = 4
RET_DK = 256
RET_DV = 512
RET_CHUNK = 128
ROPE_BASE = 10000.0
S5_GROUP = 16
S5_STATE = 64
DSA_HEADS = 8
DSA_KV_HEADS = 2
DSA_HEAD_DIM = 128
IDX_HEADS = 8
IDX_DIM = 64
DSA_TOPK = 256
LN_EPS = 1e-5
GN_EPS = 1e-5
N_BRANCHES = 3

LANES = 128
SUBLANES = 8
VMEM_LIMIT_CAP = 56 * 1024 * 1024
S5_CHUNK = 16
S5_OCT = LANES // S5_GROUP
SAMPLE_ROWS = 16
MASK_NEG = -1e30
INT_MIN = -2 ** 31


def _cparams(semantics, *block_bytes):
    est = 2 * sum(block_bytes) + (8 << 20)
    return pltpu.CompilerParams(dimension_semantics=semantics,
                                vmem_limit_bytes=int(min(max(est, 32 << 20), VMEM_LIMIT_CAP)))


def _nbytes(shape, dtype):
    return int(np.prod(shape)) * jnp.dtype(dtype).itemsize


def _dot(a, b):
    return jnp.dot(a, b, preferred_element_type=F32)


def _dot_nt(a, b):
    return lax.dot_general(a, b, (((1,), (1,)), ((), ())), preferred_element_type=F32)


def _layernorm(y, g, b):
    mu = jnp.mean(y, axis=-1, keepdims=True)
    yc = y - mu
    var = jnp.mean(yc * yc, axis=-1, keepdims=True)
    return yc * lax.rsqrt(var + LN_EPS) * g + b


def _col_of_row(r):
    n = r.shape[1]
    eye = lax.broadcasted_iota(I32, (n, n), 0) == lax.broadcasted_iota(I32, (n, n), 1)
    return jnp.sum(jnp.where(eye, jnp.broadcast_to(r, (n, n)), 0.0), axis=1, keepdims=True)


def _mm_kernel(x_ref, w_ref, *o_refs):
    acc = _dot(x_ref[...], w_ref[...])
    for o_ref in o_refs:
        o_ref[...] = acc.astype(o_ref.dtype)


def _mm_oct_kernel(x_ref, w_ref, o_ref):
    acc = _dot(x_ref[...], w_ref[...])
    for i in range(o_ref.shape[0]):
        o_ref[i] = acc[:, i * LANES:(i + 1) * LANES]


def _mm(x, w, out_dtypes, oct_layout=False):
    m, k = x.shape
    n = w.shape[1]
    tm = min(m, 1024)
    tn = n if n <= 1024 else 512
    assert m % tm == 0 and n % tn == 0
    w_resident = (n // tn) * m * k + k * n <= (m // tm) * k * n + m * k
    if w_resident:
        grid = (n // tn, m // tm)
        xi, wi, oi = (lambda j, i: (i, 0)), (lambda j, i: (0, j)), (lambda j, i: (i, j))
        ooct = lambda j, i: (j, i, 0)
    else:
        grid = (m // tm, n // tn)
        xi, wi, oi = (lambda i, j: (i, 0)), (lambda i, j: (0, j)), (lambda i, j: (i, j))
        ooct = lambda i, j: (j, i, 0)
    in_specs = [pl.BlockSpec((tm, k), xi), pl.BlockSpec((k, tn), wi)]
    blk = [_nbytes((tm, k), BF16), _nbytes((k, tn), BF16)]
    if oct_layout:
        out_shape = jax.ShapeDtypeStruct((n // LANES, m, LANES), F32)
        out_specs = pl.BlockSpec((tn // LANES, tm, LANES), ooct)
        body = _mm_oct_kernel
        blk.append(_nbytes((tm, tn), F32))
    else:
        out_shape = tuple(jax.ShapeDtypeStruct((m, n), d) for d in out_dtypes)
        out_specs = tuple(pl.BlockSpec((tm, tn), oi) for _ in out_dtypes)
        body = _mm_kernel
        blk += [_nbytes((tm, tn), d) for d in out_dtypes]
    out = pl.pallas_call(body, grid=grid, in_specs=in_specs, out_specs=out_specs, out_shape=out_shape,
                         compiler_params=_cparams(("arbitrary", "arbitrary"), *blk))(x, w)
    return out if oct_layout else (out[0] if len(out_dtypes) == 1 else out)


def _ffn_kernel(x_ref, wg_ref, wu_ref, wd_ref, g_ref, b_ref, o_ref, ob_ref, xb_scr, acc_scr, *, alpha):
    f = pl.program_id(1)

    @pl.when(f == 0)
    def _():
        xb_scr[...] = x_ref[...].astype(BF16)
        acc_scr[...] = jnp.zeros_like(acc_scr)

    xb = xb_scr[...]
    hg = _dot(xb, wg_ref[...])
    hu = _dot(xb, wu_ref[...])
    h = hg * jax.nn.sigmoid(hg) * hu
    acc_scr[...] += _dot(h.astype(BF16), wd_ref[...])

    @pl.when(f == pl.num_programs(1) - 1)
    def _():
        y = _layernorm(alpha * x_ref[...] + 0.5 * acc_scr[...], g_ref[...], b_ref[...])
        o_ref[...] = y
        ob_ref[...] = y.astype(BF16)


def _ffn_ln(x, wg, wu, wd, g, b, alpha):
    m, d = x.shape
    dff = wg.shape[1]
    tm = min(m, 512)
    tf = 512
    assert m % tm == 0 and dff % tf == 0
    row = lambda i, f: (i, 0)
    blk = [_nbytes((tm, d), F32), 3 * _nbytes((d, tf), BF16), _nbytes((tm, d), F32), _nbytes((tm, d), BF16),
           _nbytes((tm, d), F32)]
    return pl.pallas_call(
        functools.partial(_ffn_kernel, alpha=alpha),
        grid=(m // tm, dff // tf),
        in_specs=[pl.BlockSpec((tm, d), row),
                  pl.BlockSpec((d, tf), lambda i, f: (0, f)),
                  pl.BlockSpec((d, tf), lambda i, f: (0, f)),
                  pl.BlockSpec((tf, d), lambda i, f: (f, 0)),
                  pl.BlockSpec((1, d), lambda i, f: (0, 0)),
                  pl.BlockSpec((1, d), lambda i, f: (0, 0))],
        out_specs=(pl.BlockSpec((tm, d), row), pl.BlockSpec((tm, d), row)),
        out_shape=(jax.ShapeDtypeStruct((m, d), F32), jax.ShapeDtypeStruct((m, d), BF16)),
        scratch_shapes=[pltpu.VMEM((tm, d), BF16), pltpu.VMEM((tm, d), F32)],
        compiler_params=_cparams(("arbitrary", "arbitrary"), *blk),
    )(x, wg, wu, wd, g.reshape(1, d), b.reshape(1, d))


def _glu_kernel(y_ref, w_ref, o_ref):
    y = jnp.concatenate([y_ref[i] for i in range(y_ref.shape[0])], axis=-1)
    o_ref[...] = (y * jax.nn.sigmoid(_dot(y.astype(BF16), w_ref[...]))).astype(o_ref.dtype)


def _glu(y_oct, w):
    no, m, _ = y_oct.shape
    n = w.shape[1]
    tm = min(m, 512)
    blk = [_nbytes((no, tm, LANES), F32), _nbytes(w.shape, BF16), _nbytes((tm, n), BF16)]
    return pl.pallas_call(
        _glu_kernel, grid=(m // tm,),
        in_specs=[pl.BlockSpec((no, tm, LANES), lambda i: (0, i, 0)), pl.BlockSpec(w.shape, lambda i: (0, 0))],
        out_specs=pl.BlockSpec((tm, n), lambda i: (i, 0)),
        out_shape=jax.ShapeDtypeStruct((m, n), BF16),
        compiler_params=_cparams(("arbitrary",), *blk),
    )(y_oct, w)


def _merge_kernel(o_ref, z_ref, a_ref, g0_ref, g1_ref, g2_ref, wr_ref, ws_ref, wd_ref, m_ref):
    ret = _dot(o_ref[...], wr_ref[...])
    s5 = _dot(z_ref[...], ws_ref[...])
    dsa = _dot(a_ref[...], wd_ref[...])
    merged = (jax.nn.sigmoid(g0_ref[...]) * ret + jax.nn.sigmoid(g1_ref[...]) * s5
              + jax.nn.sigmoid(g2_ref[...]) * dsa)
    m_ref[...] = merged.astype(m_ref.dtype)


def _merge(o, z, a, gates, wr, ws, wd):
    m = o.shape[0]
    d = wr.shape[1]
    tm = min(m, 512)
    tn = min(d, 512)
    nb = d // tn
    blk = [_nbytes((tm, o.shape[1]), BF16), 2 * _nbytes((tm, z.shape[1]), BF16), 3 * _nbytes((tm, tn), F32),
           _nbytes((o.shape[1], tn), BF16), 2 * _nbytes((z.shape[1], tn), BF16), _nbytes((tm, tn), BF16)]
    gate_spec = lambda br: pl.BlockSpec((tm, tn), lambda i, j: (i, br * nb + j))
    return pl.pallas_call(
        _merge_kernel, grid=(m // tm, nb),
        in_specs=[pl.BlockSpec((tm, o.shape[1]), lambda i, j: (i, 0)),
                  pl.BlockSpec((tm, z.shape[1]), lambda i, j: (i, 0)),
                  pl.BlockSpec((tm, a.shape[1]), lambda i, j: (i, 0)),
                  gate_spec(0), gate_spec(1), gate_spec(2),
                  pl.BlockSpec((wr.shape[0], tn), lambda i, j: (0, j)),
                  pl.BlockSpec((ws.shape[0], tn), lambda i, j: (0, j)),
                  pl.BlockSpec((wd.shape[0], tn), lambda i, j: (0, j))],
        out_specs=pl.BlockSpec((tm, tn), lambda i, j: (i, j)),
        out_shape=jax.ShapeDtypeStruct((m, d), BF16),
        compiler_params=_cparams(("arbitrary", "arbitrary"), *blk),
    )(o, z, a, gates, gates, gates, wr, ws, wd)


def _out_ln_kernel(x_ref, m_ref, w_ref, g_ref, b_ref, o_ref, ob_ref, *, alpha):
    y = _layernorm(alpha * x_ref[...] + _dot(m_ref[...], w_ref[...]), g_ref[...], b_ref[...])
    o_ref[...] = y
    ob_ref[...] = y.astype(BF16)


def _out_ln(x, merged, w, g, b, alpha):
    m, d = x.shape
    tm = min(m, 512)
    row = lambda i: (i, 0)
    blk = [2 * _nbytes((tm, d), F32), 2 * _nbytes((tm, d), BF16), _nbytes((d, d), BF16)]
    return pl.pallas_call(
        functools.partial(_out_ln_kernel, alpha=alpha), grid=(m // tm,),
        in_specs=[pl.BlockSpec((tm, d), row), pl.BlockSpec((tm, d), row), pl.BlockSpec((d, d), lambda i: (0, 0)),
                  pl.BlockSpec((1, d), lambda i: (0, 0)), pl.BlockSpec((1, d), lambda i: (0, 0))],
        out_specs=(pl.BlockSpec((tm, d), row), pl.BlockSpec((tm, d), row)),
        out_shape=(jax.ShapeDtypeStruct((m, d), F32), jax.ShapeDtypeStruct((m, d), BF16)),
        compiler_params=_cparams(("arbitrary",), *blk),
    )(x, merged, w, g.reshape(1, d), b.reshape(1, d))


def _rope(x, cos, sin):
    half = x.shape[-1] // 2
    x1, x2 = x[:, :half], x[:, half:]
    return jnp.concatenate([x1 * cos - x2 * sin, x1 * sin + x2 * cos], axis=-1)


def _group_norm_gate(o, gate):
    mu = jnp.mean(o, axis=-1, keepdims=True)
    oc = o - mu
    var = jnp.mean(oc * oc, axis=-1, keepdims=True)
    return gate * jax.nn.sigmoid(gate) * (oc * lax.rsqrt(var + GN_EPS))


def _ret_kernel(qk_ref, v_ref, g_ref, cos_ref, sin_ref, o_ref, st_ref):
    @pl.when(pl.program_id(1) == 0)
    def _():
        st_ref[...] = jnp.zeros_like(st_ref)

    c = qk_ref.shape[0]
    cos, sin = cos_ref[...], sin_ref[...]
    ri = lax.broadcasted_iota(I32, (c, c), 0).astype(F32)
    ci = lax.broadcasted_iota(I32, (c, c), 1).astype(F32)
    diff = ri - ci
    ti = lax.broadcasted_iota(I32, (c, 1), 0).astype(F32)
    for h in range(RET_HEADS):
        lg = math.log(1.0 - 2.0 ** (-5.0 - h))
        dmask = jnp.where(diff >= 0, jnp.exp(jnp.maximum(diff, 0.0) * lg), 0.0)
        q_dec = jnp.exp((ti + 1.0) * lg)
        k_dec = jnp.exp((c - 1.0 - ti) * lg)
        c_dec = math.exp(c * lg)
        q = _rope(qk_ref[:, h * RET_DK:(h + 1) * RET_DK], cos, sin)
        k = _rope(qk_ref[:, (RET_HEADS + h) * RET_DK:(RET_HEADS + h + 1) * RET_DK], cos, sin) * (RET_DK ** -0.5)
        v = v_ref[:, h * RET_DV:(h + 1) * RET_DV]
        s = st_ref[0, h]
        qb = q.astype(BF16)
        sc = _dot_nt(qb, k.astype(BF16)) * dmask
        o = _dot(sc.astype(BF16), v) + _dot(qb, s.astype(BF16)) * q_dec
        kd_t = jnp.transpose(k * k_dec).astype(BF16)
        st_ref[0, h] = c_dec * s + _dot(kd_t, v)
        o_ref[:, h * RET_DV:(h + 1) * RET_DV] = _group_norm_gate(
            o, g_ref[:, h * RET_DV:(h + 1) * RET_DV]).astype(o_ref.dtype)


def _retention(qk, v, gate, cos, sin, bn):
    m = qk.shape[0]
    length = m // bn
    c = RET_CHUNK
    nc = length // c
    hv = RET_HEADS * RET_DV
    row = lambda b, j: (b * nc + j, 0)
    blk = [_nbytes((c, qk.shape[1]), F32), _nbytes((c, hv), BF16), _nbytes((c, hv), F32), _nbytes((c, hv), BF16),
           _nbytes((RET_HEADS, RET_DK, RET_DV), F32)]
    return pl.pallas_call(
        _ret_kernel, grid=(bn, nc),
        in_specs=[pl.BlockSpec((c, qk.shape[1]), row), pl.BlockSpec((c, hv), row), pl.BlockSpec((c, hv), row),
                  pl.BlockSpec((c, RET_DK // 2), lambda b, j: (j, 0)),
                  pl.BlockSpec((c, RET_DK // 2), lambda b, j: (j, 0))],
        out_specs=(pl.BlockSpec((c, hv), row),
                   pl.BlockSpec((1, RET_HEADS, RET_DK, RET_DV), lambda b, j: (b, 0, 0, 0))),
        out_shape=(jax.ShapeDtypeStruct((m, hv), BF16),
                   jax.ShapeDtypeStruct((bn, RET_HEADS, RET_DK, RET_DV), F32)),
        compiler_params=_cparams(("arbitrary", "arbitrary"), *blk),
    )(qk, v, gate, cos, sin)


def _ret_step_kernel(qk_ref, v_ref, g_ref, cos_ref, sin_ref, s0_ref, o_ref, st_ref):
    row = pl.ds(pl.program_id(0), 1)
    cos, sin = cos_ref[...], sin_ref[...]
    for h in range(RET_HEADS):
        decay = 1.0 - 2.0 ** (-5.0 - h)
        q = _rope(qk_ref[row, h * RET_DK:(h + 1) * RET_DK], cos, sin)
        k = _rope(qk_ref[row, (RET_HEADS + h) * RET_DK:(RET_HEADS + h + 1) * RET_DK], cos, sin) * (RET_DK ** -0.5)
        v = v_ref[row, h * RET_DV:(h + 1) * RET_DV]
        s0 = s0_ref[0, h]
        st_ref[0, h] = decay * s0 + _col_of_row(k) * v
        o = (jnp.sum(q * k, axis=1, keepdims=True) * v
             + jnp.sum(_col_of_row(q) * s0, axis=0, keepdims=True) * decay)
        o_ref[0, :, h * RET_DV:(h + 1) * RET_DV] = _group_norm_gate(o, g_ref[row, h * RET_DV:(h + 1) * RET_DV])


def _retention_step(qk, v, gate, cos, sin, s0):
    bn = s0.shape[0]
    const = lambda b: (0, 0)
    state = pl.BlockSpec((1,) + s0.shape[1:], lambda b: (b, 0, 0, 0))
    return pl.pallas_call(
        _ret_step_kernel, grid=(bn,),
        in_specs=[pl.BlockSpec(qk.shape, const), pl.BlockSpec(v.shape, const), pl.BlockSpec(gate.shape, const),
                  pl.BlockSpec(cos.shape, const), pl.BlockSpec(sin.shape, const), state],
        out_specs=(pl.BlockSpec((1, 1, v.shape[1]), lambda b: (b, 0, 0)), state),
        out_shape=(jax.ShapeDtypeStruct((bn, 1, v.shape[1]), F32), jax.ShapeDtypeStruct(s0.shape, F32)),
        compiler_params=_cparams(("arbitrary",), 2 * _nbytes(s0.shape[1:], F32)),
    )(qk, v, gate, cos, sin, s0)


def _s5_tables(a_re, a_im, log_dt, b_re, b_im, c_re, c_im, d):
    hp = lax.Precision.HIGHEST
    g, p = a_re.shape
    nc = b_re.shape[-1]
    no = g // S5_OCT
    t = S5_CHUNK
    dt = jnp.exp(log_dt)[:, None]
    mag = jnp.exp(a_re * dt)
    ab_re = mag * jnp.cos(a_im * dt)
    ab_im = mag * jnp.sin(a_im * dt)
    den = a_re * a_re + a_im * a_im
    x_re = ab_re - 1.0
    f_re = (x_re * a_re + ab_im * a_im) / den
    f_im = (ab_im * a_re - x_re * a_im) / den
    bb_re = f_re[..., None] * b_re - f_im[..., None] * b_im
    bb_im = f_re[..., None] * b_im + f_im[..., None] * b_re

    def powers(n):
        n = n.astype(F32)[:, None, None]
        pmag = jnp.exp(n * (a_re * dt)[None])
        return pmag * jnp.cos(n * (a_im * dt)[None]), pmag * jnp.sin(n * (a_im * dt)[None])

    pw_re, pw_im = powers(jnp.arange(t + 1))
    rev_re, rev_im = powers(t - 1 - jnp.arange(t))
    eye = jnp.eye(S5_OCT, dtype=F32)

    def c_times(power_re, power_im):
        return (c_re[None] * power_re[:, :, None, :] - c_im[None] * power_im[:, :, None, :],
                c_re[None] * power_im[:, :, None, :] + c_im[None] * power_re[:, :, None, :])

    cp_re, cp_im = c_times(pw_re[:t], pw_im[:t])
    kern = (jnp.einsum('ngdp,gpc->gndc', cp_re, bb_re, precision=hp)
            - jnp.einsum('ngdp,gpc->gndc', cp_im, bb_im, precision=hp))
    lag = jnp.arange(t)[None, :] - jnp.arange(t)[:, None]
    kt = kern[:, jnp.clip(lag, 0, t - 1)] * (lag >= 0)[None, :, :, None, None]
    kt = kt.reshape(no, S5_OCT, t, t, nc, nc)
    toep = jnp.einsum('oistdc,ij->osictjd', kt, eye).reshape(no, t * LANES, t * LANES)

    def b_rows(power_re, power_im):
        e_re = power_re[..., None] * bb_re[None] - power_im[..., None] * bb_im[None]
        e_im = power_re[..., None] * bb_im[None] + power_im[..., None] * bb_re[None]
        out = []
        for e in (e_re, e_im):
            e = e.reshape(e.shape[0], no, S5_OCT, p, nc)
            out.append(jnp.einsum('soipc,ij->osicjp', e, eye).reshape(no, e.shape[0] * LANES, S5_OCT * p))
        return jnp.concatenate(out, axis=-1)

    def c_cols(power_re, power_im):
        a_r, a_i = c_times(power_re, power_im)
        out = []
        for a in (a_r, -a_i):
            a = a.reshape(a.shape[0], no, S5_OCT, nc, p)
            out.append(jnp.einsum('toidp,ij->oiptjd', a, eye).reshape(no, S5_OCT * p, a.shape[0] * LANES))
        return jnp.concatenate(out, axis=1)

    def state_row(v_re, v_im):
        return jnp.concatenate([v_re.reshape(no, 1, S5_OCT * p), v_im.reshape(no, 1, S5_OCT * p)], axis=-1)

    d_row = d.reshape(no, 1, LANES)
    return dict(
        toep=toep.astype(BF16),
        b_end=b_rows(rev_re, rev_im).astype(BF16),
        c_in=c_cols(pw_re[1:], pw_im[1:]).astype(BF16),
        a_row=state_row(pw_re[t], pw_im[t]),
        d_row=jnp.tile(d_row, (1, 1, t)),
        b_one=b_rows(pw_re[:1], pw_im[:1]).astype(BF16),
        c_one=c_cols(pw_re[:1], pw_im[:1]).astype(BF16),
        a_one=state_row(ab_re, ab_im),
        d_one=d_row,
    )


def _s5_kernel(u_ref, toep_ref, bend_ref, cin_ref, a_ref, d_ref, y_ref, h_ref, e_scr, s_scr):
    u = u_ref[0]
    ub = u.astype(BF16)
    e_scr[...] = _dot(ub, bend_ref[0])
    half = a_ref.shape[-1] // 2
    a_re, a_im = a_ref[0, :, :half], a_ref[0, :, half:]

    def step(k, carry):
        s_re, s_im = carry
        s_scr[pl.ds(k, 1), :] = jnp.concatenate([s_re, s_im], axis=-1)
        e = e_scr[pl.ds(k, 1), :]
        return (a_re * s_re - a_im * s_im + e[:, :half], a_re * s_im + a_im * s_re + e[:, half:])

    zero = jnp.zeros((1, half), F32)
    s_re, s_im = lax.fori_loop(0, u.shape[0], step, (zero, zero))
    h_ref[0, 0] = jnp.concatenate([s_re, s_im], axis=-1)
    y = _dot(ub, toep_ref[0]) + _dot(s_scr[...].astype(BF16), cin_ref[0]) + d_ref[0] * u
    y_ref[0] = jax.nn.gelu(y)


def _s5(u_oct, tb, bn):
    no, m, _ = u_oct.shape
    t = S5_CHUNK
    rows = m // bn // t
    w = t * LANES
    ns = tb['a_row'].shape[-1]
    u = u_oct.reshape(no, m // t, w)
    oct_blk = lambda shape: pl.BlockSpec((1,) + shape, lambda o, b: (o, 0, 0))
    blk = [2 * _nbytes((rows, w), F32), _nbytes((w, w), BF16), 2 * _nbytes((w, ns), BF16),
           2 * _nbytes((rows, ns), F32)]
    y, h = pl.pallas_call(
        _s5_kernel, grid=(no, bn),
        in_specs=[pl.BlockSpec((1, rows, w), lambda o, b: (o, b, 0)),
                  oct_blk((w, w)), oct_blk((w, ns)), oct_blk((ns, w)), oct_blk((1, ns)), oct_blk((1, w))],
        out_specs=(pl.BlockSpec((1, rows, w), lambda o, b: (o, b, 0)),
                   pl.BlockSpec((1, 1, 1, ns), lambda o, b: (b, o, 0, 0))),
        out_shape=(jax.ShapeDtypeStruct(u.shape, F32), jax.ShapeDtypeStruct((bn, no, 1, ns), F32)),
        scratch_shapes=[pltpu.VMEM((rows, ns), F32), pltpu.VMEM((rows, ns), F32)],
        compiler_params=_cparams(("arbitrary", "arbitrary"), *blk),
    )(u, tb['toep'], tb['b_end'], tb['c_in'], tb['a_row'], tb['d_row'])
    return y.reshape(no, m, LANES), h


def _s5_step_kernel(u_ref, h0_ref, b_ref, c_ref, a_ref, d_ref, y_ref, h_ref):
    u = u_ref[0]
    half = a_ref.shape[-1] // 2
    a_re, a_im = a_ref[0, :, :half], a_ref[0, :, half:]
    h0 = h0_ref[0]
    h0_re, h0_im = h0[:, :half], h0[:, half:]
    bu = _dot(u.astype(BF16), b_ref[0])
    h_re = a_re * h0_re - a_im * h0_im + bu[:, :half]
    h_im = a_re * h0_im + a_im * h0_re + bu[:, half:]
    h = jnp.concatenate([h_re, h_im], axis=-1)
    h_ref[0] = h
    y_ref[0] = jax.nn.gelu(_dot(h.astype(BF16), c_ref[0]) + d_ref[0] * u)


def _s5_step(u_oct, h0, tb):
    no, rows, _ = u_oct.shape
    ns = h0.shape[-1]
    o3 = lambda shape: pl.BlockSpec((1,) + shape, lambda o: (o, 0, 0))
    return pl.pallas_call(
        _s5_step_kernel, grid=(no,),
        in_specs=[o3((rows, LANES)), o3((rows, ns)), o3((LANES, ns)), o3((ns, LANES)), o3((1, ns)), o3((1, LANES))],
        out_specs=(o3((rows, LANES)), o3((rows, ns))),
        out_shape=(jax.ShapeDtypeStruct(u_oct.shape, F32), jax.ShapeDtypeStruct(h0.shape, F32)),
        compiler_params=_cparams(("arbitrary",), _nbytes((LANES, ns), BF16) * 2),
    )(u_oct, h0, tb['b_one'], tb['c_one'], tb['a_one'], tb['d_one'])


def _sort_key(x):
    bits = pltpu.bitcast(x, I32)
    return jnp.where(bits >= 0, bits, bits ^ jnp.int32(0x7FFFFFFF))


def _kth_largest_key(count_ge, nsel, shape):
    res = jnp.where(count_ge(jnp.zeros(shape, I32)) >= nsel, jnp.int32(0), jnp.int32(INT_MIN))

    def bit_step(i, res):
        cand = res | jnp.left_shift(jnp.int32(1), 30 - i)
        return jnp.where(count_ge(cand) >= nsel, cand, res)

    return lax.fori_loop(0, 31, bit_step, res)


def _dsa_kernel(dq_ref, iq_ref, ikw_ref, ikb_ref, kv_ref, o_ref, key_scr, bias_scr, m_scr, l_scr, acc_scr,
                *, nsel):
    tq = dq_ref.shape[0]
    jq = pl.program_id(1)
    nkb = jq + 1
    row = lax.broadcasted_iota(I32, (tq, tq), 0)
    col = lax.broadcasted_iota(I32, (tq, tq), 1)
    qpos = jq * tq + row
    hd = DSA_HEAD_DIM

    iq_st = jnp.concatenate([iq_ref[:, h * LANES:(h + 1) * LANES] for h in range(IDX_HEADS)], axis=0)
    w = ikw_ref[:, IDX_DIM:IDX_DIM + IDX_HEADS]

    def score_block(kb, carry):
        ik = ikb_ref[pl.ds(pl.multiple_of(kb * tq, tq), tq), :]
        sh = jnp.maximum(_dot_nt(iq_st, ik), 0.0)
        acc = jnp.zeros((tq, tq), F32)
        for h in range(IDX_HEADS):
            acc = acc + sh[h * tq:(h + 1) * tq] * w[:, h:h + 1]
        acc = jnp.where(kb * tq + col <= qpos, acc, -jnp.inf)
        key_scr[kb] = _sort_key(acc)
        return carry

    lax.fori_loop(0, nkb, score_block, 0)

    def count(pred):
        def body(kb, c):
            return c + jnp.where(pred(key_scr[kb]), 1.0, 0.0)
        return jnp.sum(lax.fori_loop(0, nkb, body, jnp.zeros((tq, tq), F32)), axis=1, keepdims=True)

    kth = _kth_largest_key(lambda cand: count(lambda key: key >= cand), float(nsel), (tq, 1))
    need = float(nsel) - count(lambda key: key > kth)
    tri = jnp.where(row <= col, 1.0, 0.0).astype(BF16)

    def select_block(kb, taken):
        key = key_scr[kb]
        tie = jnp.where(key == kth, 1.0, 0.0)
        rank = taken + _dot(tie.astype(BF16), tri)
        sel = ((key > kth) | ((tie > 0.0) & (rank <= need))) & (kb * tq + col <= qpos)
        bias_scr[kb] = jnp.where(sel, 0.0, MASK_NEG)
        return taken + jnp.sum(tie, axis=1, keepdims=True)

    lax.fori_loop(0, nkb, select_block, jnp.zeros((tq, 1), F32))

    gsz = DSA_HEADS // DSA_KV_HEADS
    for g in range(DSA_KV_HEADS):
        qg = jnp.concatenate([dq_ref[:, (g * gsz + i) * hd:(g * gsz + i + 1) * hd] for i in range(gsz)], axis=0)
        m_scr[...] = jnp.full_like(m_scr, -jnp.inf)
        l_scr[...] = jnp.zeros_like(l_scr)
        acc_scr[...] = jnp.zeros_like(acc_scr)

        def attend_block(kb, carry):
            ks = pl.ds(pl.multiple_of(kb * tq, tq), tq)
            kblk = kv_ref[ks, g * hd:(g + 1) * hd]
            vblk = kv_ref[ks, (DSA_KV_HEADS + g) * hd:(DSA_KV_HEADS + g + 1) * hd]
            lg = _dot_nt(qg, kblk) * (hd ** -0.5)
            lg = (lg.reshape(gsz, tq, tq) + bias_scr[kb][None]).reshape(gsz * tq, tq)
            m_old = m_scr[...]
            m_new = jnp.maximum(m_old, jnp.max(lg, axis=1, keepdims=True))
            alpha = jnp.exp(m_old - m_new)
            p = jnp.exp(lg - m_new)
            l_scr[...] = alpha * l_scr[...] + jnp.sum(p, axis=1, keepdims=True)
            acc_scr[...] = alpha * acc_scr[...] + _dot(p.astype(BF16), vblk)
            m_scr[...] = m_new
            return carry

        lax.fori_loop(0, nkb, attend_block, 0)
        out = acc_scr[...] / l_scr[...]
        for i in range(gsz):
            o_ref[:, (g * gsz + i) * hd:(g * gsz + i + 1) * hd] = out[i * tq:(i + 1) * tq].astype(o_ref.dtype)


def _dsa_prompt(dq, iq, ikw, ikb, kvb, bn):
    m = dq.shape[0]
    length = m // bn
    tq = 256
    nq = length // tq
    nsel = min(DSA_TOPK, length // 4)
    qrow = lambda b, j: (b * nq + j, 0)
    full = lambda b, j: (b, 0)
    gsz = DSA_HEADS // DSA_KV_HEADS
    blk = [_nbytes((tq, dq.shape[1]), BF16) * 3, _nbytes((length, LANES), BF16), _nbytes((length, kvb.shape[1]), BF16),
           _nbytes((nq, tq, tq), F32)]
    return pl.pallas_call(
        functools.partial(_dsa_kernel, nsel=nsel), grid=(bn, nq),
        in_specs=[pl.BlockSpec((tq, dq.shape[1]), qrow), pl.BlockSpec((tq, iq.shape[1]), qrow),
                  pl.BlockSpec((tq, LANES), qrow), pl.BlockSpec((length, LANES), full),
                  pl.BlockSpec((length, kvb.shape[1]), full)],
        out_specs=pl.BlockSpec((tq, dq.shape[1]), qrow),
        out_shape=jax.ShapeDtypeStruct(dq.shape, BF16),
        scratch_shapes=[pltpu.VMEM((nq, tq, tq), I32), pltpu.VMEM((nq, tq, tq), F32),
                        pltpu.VMEM((gsz * tq, 1), F32), pltpu.VMEM((gsz * tq, 1), F32),
                        pltpu.VMEM((gsz * tq, DSA_HEAD_DIM), F32)],
        compiler_params=_cparams(("arbitrary", "arbitrary"), *blk),
    )(dq, iq, ikw, ikb, kvb)


PAGES_PER_STEP = 8


def _idx_heads(iq_row):
    return jnp.concatenate([iq_row[:, h * LANES:h * LANES + IDX_DIM] for h in range(IDX_HEADS)], axis=0)


def _page_score_kernel(pt_ref, iq_ref, ikw_ref, *rest):
    page_refs, o_ref = rest[:-1], rest[-1]
    b = pl.program_id(0)
    iq_h = _idx_heads(iq_ref[pl.ds(b, 1), :])
    w_col = _col_of_row(ikw_ref[pl.ds(b, 1), IDX_DIM:IDX_DIM + IDX_HEADS])
    for i, page in enumerate(page_refs):
        sh = jnp.maximum(_dot_nt(iq_h, page[...]), 0.0)
        o_ref[0, i:i + 1, :] = jnp.sum(sh * w_col, axis=0, keepdims=True)


def _page_scores(page_table, iq, ikw, cache_idx, layer):
    bn, npages = page_table.shape
    pg = PAGES_PER_STEP
    const = lambda b, s, pt: (0, 0)

    def page_spec(i):
        return pl.BlockSpec((None, None, PAGE_SIZE, IDX_DIM), lambda b, s, pt: (layer, pt[b, s * pg + i], 0, 0))

    return pl.pallas_call(
        _page_score_kernel,
        grid_spec=pltpu.PrefetchScalarGridSpec(
            num_scalar_prefetch=1, grid=(bn, npages // pg),
            in_specs=[pl.BlockSpec(iq.shape, const), pl.BlockSpec(ikw.shape, const)]
            + [page_spec(i) for i in range(pg)],
            out_specs=pl.BlockSpec((1, pg, PAGE_SIZE), lambda b, s, pt: (b, s, 0))),
        out_shape=jax.ShapeDtypeStruct((bn, npages, PAGE_SIZE), F32),
        compiler_params=_cparams(("arbitrary", "arbitrary")),
    )(page_table, iq, ikw, *([cache_idx] * pg))


def _page_select_kernel(sc_ref, iq_ref, ikw_ref, bias_ref, bias_self_ref, *, nsel):
    b = pl.program_id(0)
    npages, psz = sc_ref.shape[1:]
    iq_h = _idx_heads(iq_ref[pl.ds(b, 1), :])
    ikw = ikw_ref[pl.ds(b, 1), :]
    w_col = _col_of_row(ikw[:, IDX_DIM:IDX_DIM + IDX_HEADS])
    s_self = jnp.sum(jnp.maximum(jnp.sum(iq_h * ikw[:, :IDX_DIM], axis=1, keepdims=True), 0.0) * w_col,
                     axis=0, keepdims=True)
    key = _sort_key(sc_ref[0])
    key_self = _sort_key(s_self)

    def total(x):
        return jnp.sum(jnp.sum(x, axis=1, keepdims=True), axis=0, keepdims=True)

    def count_ge(cand):
        return total(jnp.where(key >= cand, 1.0, 0.0)) + jnp.where(key_self >= cand, 1.0, 0.0)

    kth = _kth_largest_key(count_ge, float(nsel), (1, 1))
    need = float(nsel) - (total(jnp.where(key > kth, 1.0, 0.0)) + jnp.where(key_self > kth, 1.0, 0.0))
    tie = jnp.where(key == kth, 1.0, 0.0)
    r_in = lax.broadcasted_iota(I32, (psz, psz), 0)
    c_in = lax.broadcasted_iota(I32, (psz, psz), 1)
    in_page = _dot(tie.astype(BF16), jnp.where(r_in <= c_in, 1.0, 0.0).astype(BF16))
    per_page = jnp.broadcast_to(jnp.sum(tie, axis=1, keepdims=True), (npages, psz)).astype(BF16)
    r_pg = lax.broadcasted_iota(I32, (npages, npages), 0)
    c_pg = lax.broadcasted_iota(I32, (npages, npages), 1)
    before = _dot(jnp.where(c_pg < r_pg, 1.0, 0.0).astype(BF16), per_page)
    sel = (key > kth) | ((tie > 0.0) & (before + in_page <= need))
    bias_ref[0] = jnp.where(sel, 0.0, MASK_NEG)
    sel_self = (key_self > kth) | ((key_self == kth) & (total(tie) + 1.0 <= need))
    bias_self_ref[0] = jnp.broadcast_to(jnp.where(sel_self, 0.0, MASK_NEG), (1, LANES))


def _page_select(scores, iq, ikw, nsel):
    bn, npages, psz = scores.shape
    const = lambda b: (0, 0)
    return pl.pallas_call(
        functools.partial(_page_select_kernel, nsel=nsel), grid=(bn,),
        in_specs=[pl.BlockSpec((1, npages, psz), lambda b: (b, 0, 0)),
                  pl.BlockSpec(iq.shape, const), pl.BlockSpec(ikw.shape, const)],
        out_specs=(pl.BlockSpec((1, npages, psz), lambda b: (b, 0, 0)),
                   pl.BlockSpec((1, 1, LANES), lambda b: (b, 0, 0))),
        out_shape=(jax.ShapeDtypeStruct(scores.shape, F32), jax.ShapeDtypeStruct((bn, 1, LANES), F32)),
        compiler_params=_cparams(("arbitrary",)),
    )(scores, iq, ikw)


def _page_attend_kernel(pt_ref, dq_ref, kvs_ref, bias_ref, bself_ref, *rest, npg):
    k_refs, v_refs = rest[:npg], rest[npg:2 * npg]
    o_ref, m_scr, l_scr, acc_scr = rest[2 * npg:]
    b = pl.program_id(0)
    s = pl.program_id(1)
    hd = DSA_HEAD_DIM
    gsz = DSA_HEADS // DSA_KV_HEADS
    scale = hd ** -0.5
    dq_row = dq_ref[pl.ds(b, 1), :]
    q = jnp.concatenate([dq_row[:, h * hd:(h + 1) * hd] for h in range(DSA_HEADS)], axis=0)
    first_group = lax.broadcasted_iota(I32, (DSA_HEADS, 1), 0) < gsz

    @pl.when(s == 0)
    def _():
        m_scr[...] = jnp.full_like(m_scr, -jnp.inf)
        l_scr[...] = jnp.zeros_like(l_scr)
        acc_scr[...] = jnp.zeros_like(acc_scr)

    def update(lg, pv_of):
        m_old = m_scr[...]
        m_new = jnp.maximum(m_old, jnp.max(lg, axis=1, keepdims=True))
        alpha = jnp.exp(m_old - m_new)
        p = jnp.exp(lg - m_new)
        l_scr[...] = alpha * l_scr[...] + jnp.sum(p, axis=1, keepdims=True)
        acc_scr[...] = alpha * acc_scr[...] + pv_of(p)
        m_scr[...] = m_new

    for i in range(npg):
        kp, vp = k_refs[i][...], v_refs[i][...]
        lg = jnp.where(first_group, _dot_nt(q, kp[:, :hd]), _dot_nt(q, kp[:, hd:])) * scale + bias_ref[0, i:i + 1, :]

        def pv_of(p, vp=vp):
            pv = _dot(p, vp)
            return jnp.where(first_group, pv[:, :hd], pv[:, hd:])

        update(lg, pv_of)

    @pl.when(s == pl.num_programs(1) - 1)
    def _():
        kvs = kvs_ref[pl.ds(b, 1), :]
        k_self = jnp.where(first_group, kvs[:, :hd], kvs[:, hd:2 * hd])
        v_self = jnp.where(first_group, kvs[:, 2 * hd:3 * hd], kvs[:, 3 * hd:])
        lg = jnp.sum(q * k_self, axis=1, keepdims=True) * scale + bself_ref[0, :, :1]
        update(lg, lambda p: p * v_self)
        out = acc_scr[...] / l_scr[...]
        for h in range(DSA_HEADS):
            o_ref[0, :, h * hd:(h + 1) * hd] = out[h:h + 1]


def _page_attend(page_table, dq, kvs, bias, bias_self, cache_k, cache_v, layer):
    bn, npages = page_table.shape
    pg = PAGES_PER_STEP
    kvw = cache_k.shape[-1]
    const = lambda b, s, pt: (0, 0)

    def page_spec(i):
        return pl.BlockSpec((None, None, PAGE_SIZE, kvw), lambda b, s, pt: (layer, pt[b, s * pg + i], 0, 0))

    return pl.pallas_call(
        functools.partial(_page_attend_kernel, npg=pg),
        grid_spec=pltpu.PrefetchScalarGridSpec(
            num_scalar_prefetch=1, grid=(bn, npages // pg),
            in_specs=[pl.BlockSpec(dq.shape, const), pl.BlockSpec(kvs.shape, const),
                      pl.BlockSpec((1, pg, PAGE_SIZE), lambda b, s, pt: (b, s, 0)),
                      pl.BlockSpec((1, 1, LANES), lambda b, s, pt: (b, 0, 0))]
            + [page_spec(i) for i in range(pg)] + [page_spec(i) for i in range(pg)],
            out_specs=pl.BlockSpec((1, 1, dq.shape[1]), lambda b, s, pt: (b, 0, 0)),
            scratch_shapes=[pltpu.VMEM((DSA_HEADS, 1), F32), pltpu.VMEM((DSA_HEADS, 1), F32),
                            pltpu.VMEM((DSA_HEADS, DSA_HEAD_DIM), F32)]),
        out_shape=jax.ShapeDtypeStruct((bn, 1, dq.shape[1]), F32),
        compiler_params=_cparams(("arbitrary", "arbitrary"), 2 * pg * _nbytes((PAGE_SIZE, kvw), F32)),
    )(page_table, dq, kvs, bias, bias_self, *([cache_k] * pg), *([cache_v] * pg))


def _rope_tables(pos):
    half = RET_DK // 2
    inv = 1.0 / (ROPE_BASE ** jnp.linspace(0.0, 1.0, half, dtype=F32))
    ang = pos.astype(F32)[:, None] * inv[None, :]
    return jnp.cos(ang), jnp.sin(ang)


def _in_proj_weights(w_in):
    hq = RET_HEADS * RET_DK
    hv = RET_HEADS * RET_DV
    d = w_in.shape[0]
    o = 0
    cuts = {}
    for name, width in (('qk', 2 * hq), ('v', hv), ('g', hv)):
        cuts[name] = (o, o + width)
        o += width
    s5w = (w_in.shape[1] - 2 * hq - 2 * hv - DSA_HEADS * DSA_HEAD_DIM - 2 * DSA_KV_HEADS * DSA_HEAD_DIM
           - IDX_HEADS * IDX_DIM - IDX_DIM - IDX_HEADS) // (1 + 2 * N_BRANCHES)
    for name, width in (('su', s5w), ('dq', DSA_HEADS * DSA_HEAD_DIM), ('kv', 2 * DSA_KV_HEADS * DSA_HEAD_DIM),
                        ('iq', IDX_HEADS * IDX_DIM), ('ikw', IDX_DIM + IDX_HEADS), ('gates', N_BRANCHES * d)):
        cuts[name] = (o, o + width)
        o += width
    assert o == w_in.shape[1]
    wb = {k: w_in[:, a:b].astype(BF16) for k, (a, b) in cuts.items()}
    iq = wb['iq'].reshape(d, IDX_HEADS, IDX_DIM)
    wb['iq'] = jnp.pad(iq, ((0, 0), (0, 0), (0, LANES - IDX_DIM))).reshape(d, IDX_HEADS * LANES)
    wb['ikw'] = jnp.pad(wb['ikw'], ((0, 0), (0, LANES - IDX_DIM - IDX_HEADS)))
    return wb


def _project(xb, wb, sample):
    act = F32 if sample else BF16
    p = {
        'qk': _mm(xb, wb['qk'], (F32,)),
        'v': _mm(xb, wb['v'], (act,)),
        'g': _mm(xb, wb['g'], (F32,)),
        'su': _mm(xb, wb['su'], None, oct_layout=True),
        'dq': _mm(xb, wb['dq'], (act,)),
        'iq': _mm(xb, wb['iq'], (act,)),
        'gates': _mm(xb, wb['gates'], (F32,)),
    }
    p['kv'], p['kvb'] = _mm(xb, wb['kv'], (F32, BF16))
    p['ikw'], p['ikb'] = _mm(xb, wb['ikw'], (F32, BF16))
    return p


def _s5_state_out(h, groups):
    bn = h.shape[0]
    h = h.reshape(bn, groups // S5_OCT, 2, S5_OCT, S5_STATE)
    return h[:, :, 0].reshape(bn, groups, S5_STATE), h[:, :, 1].reshape(bn, groups, S5_STATE)


def kernel(x_prompt, x_sample, cache_k, cache_v, cache_idx_k, state_ret, state_s5_re, state_s5_im, page_table, ln1_g, ln1_b, ffn1_wg, ffn1_wu, ffn1_wd, w_in, s5_a_re, s5_a_im, s5_log_dt, s5_b_re, s5_b_im, s5_c_re, s5_c_im, s5_d, w_glu, w_ret_o, w_s5_o, w_dsa_o, w_out, ln2_g, ln2_b, ffn2_wg, ffn2_wu, ffn2_wd, ln3_g, ln3_b):
    bp, lp, d = x_prompt.shape
    bs, ls, _ = x_sample.shape
    depth = w_in.shape[0]
    assert ls == 1 and bs <= SAMPLE_ROWS
    npages = page_table.shape[1]
    past = npages * PAGE_SIZE
    groups = s5_a_re.shape[1]
    alpha = (2 * depth) ** 0.25
    kvw = DSA_KV_HEADS * DSA_HEAD_DIM
    nsel_s = min(DSA_TOPK, (past + ls) // 4)

    cos_p, sin_p = _rope_tables(jnp.arange(lp))
    cos_s, sin_s = _rope_tables(past + jnp.arange(ls))
    cache_k = cache_k.reshape(depth, -1, PAGE_SIZE, kvw)
    cache_v = cache_v.reshape(depth, -1, PAGE_SIZE, kvw)

    xp = x_prompt.reshape(bp * lp, d)
    xs = jnp.pad(x_sample.reshape(bs, d), ((0, SAMPLE_ROWS - bs), (0, 0)))
    outs_p, outs_s = [], []
    for l in range(depth):
        bf = lambda a: a[l].astype(BF16)
        wb = _in_proj_weights(w_in[l])
        tb = _s5_tables(s5_a_re[l], s5_a_im[l], s5_log_dt[l], s5_b_re[l], s5_b_im[l], s5_c_re[l], s5_c_im[l],
                        s5_d[l])
        w1 = (bf(ffn1_wg), bf(ffn1_wu), bf(ffn1_wd))
        w2 = (bf(ffn2_wg), bf(ffn2_wu), bf(ffn2_wd))
        wglu, wro, wso, wdo, wo = bf(w_glu), bf(w_ret_o), bf(w_s5_o), bf(w_dsa_o), bf(w_out)

        xp, xpb = _ffn_ln(xp, *w1, ln1_g[l], ln1_b[l], alpha)
        p = _project(xpb, wb, sample=False)
        o_ret, ret_p = _retention(p['qk'], p['v'], p['g'], cos_p, sin_p, bp)
        y_s5, h_p = _s5(p['su'], tb, bp)
        z = _glu(y_s5, wglu)
        att = _dsa_prompt(p['dq'], p['iq'], p['ikw'], p['ikb'], p['kvb'], bp)
        merged = _merge(o_ret, z, att, p['gates'], wro, wso, wdo)
        xp, xpb = _out_ln(xp, merged, wo, ln2_g[l], ln2_b[l], alpha)
        xp, xpb = _ffn_ln(xp, *w2, ln3_g[l], ln3_b[l], alpha)
        s5r_p, s5i_p = _s5_state_out(h_p, groups)
        outs_p.append((p['kv'][:, :kvw].reshape(bp, lp, DSA_KV_HEADS, DSA_HEAD_DIM),
                       p['kv'][:, kvw:].reshape(bp, lp, DSA_KV_HEADS, DSA_HEAD_DIM),
                       p['ikw'][:, :IDX_DIM].reshape(bp, lp, IDX_DIM), ret_p, s5r_p, s5i_p))

        xs, xsb = _ffn_ln(xs, *w1, ln1_g[l], ln1_b[l], alpha)
        q = _project(xsb, wb, sample=True)
        o_ret_s, ret_s = _retention_step(q['qk'], q['v'], q['g'], cos_s, sin_s, state_ret[l])
        o_ret_s = jnp.pad(o_ret_s[:, 0], ((0, SAMPLE_ROWS - bs), (0, 0))).astype(BF16)
        h0 = jnp.concatenate([state_s5_re[l].reshape(bs, groups // S5_OCT, S5_OCT * S5_STATE),
                              state_s5_im[l].reshape(bs, groups // S5_OCT, S5_OCT * S5_STATE)], axis=-1)
        h0 = jnp.pad(jnp.swapaxes(h0, 0, 1), ((0, 0), (0, SAMPLE_ROWS - bs), (0, 0)))
        y_s, h_s = _s5_step(q['su'], h0, tb)
        z_s = _glu(y_s, wglu)
        scores = _page_scores(page_table, q['iq'], q['ikw'], cache_idx_k, l)
        bias, bias_self = _page_select(scores, q['iq'], q['ikw'], nsel_s)
        att_s = _page_attend(page_table, q['dq'], q['kv'], bias, bias_self, cache_k, cache_v, l)
        att_s = jnp.pad(att_s[:, 0], ((0, SAMPLE_ROWS - bs), (0, 0))).astype(BF16)
        merged_s = _merge(o_ret_s, z_s, att_s, q['gates'], wro, wso, wdo)
        xs, xsb = _out_ln(xs, merged_s, wo, ln2_g[l], ln2_b[l], alpha)
        xs, xsb = _ffn_ln(xs, *w2, ln3_g[l], ln3_b[l], alpha)
        s5r_s, s5i_s = _s5_state_out(jnp.swapaxes(h_s, 0, 1)[:bs, :, None, :], groups)
        outs_s.append((q['kv'][:bs, :kvw].reshape(bs, ls, DSA_KV_HEADS, DSA_HEAD_DIM),
                       q['kv'][:bs, kvw:].reshape(bs, ls, DSA_KV_HEADS, DSA_HEAD_DIM),
                       q['ikw'][:bs, :IDX_DIM].reshape(bs, ls, IDX_DIM), ret_s, s5r_s, s5i_s))

    k_p, v_p, ik_p, ret_p, s5r_p, s5i_p = [jnp.stack(a) for a in zip(*outs_p)]
    k_s, v_s, ik_s, ret_s, s5r_s, s5i_s = [jnp.stack(a) for a in zip(*outs_s)]
    return (xp.reshape(bp, lp, d), xs[:bs].reshape(bs, ls, d), k_p, v_p, ik_p, k_s, v_s, ik_s,
            ret_p, ret_s, s5r_p, s5i_p, s5r_s, s5i_s)
```

```python
import functools
import math

import numpy as np
import jax
import jax.numpy as jnp
from jax import lax
from jax.experimental import pallas as pl
from jax.experimental.pallas import tpu as pltpu

F32 = jnp.float32
BF16 = jnp.bfloat16
I32 = jnp.int32

PAGE_SIZE = 128
RET_HEADS = 4
RET_DK = 256
RET_DV = 512
RET_CHUNK = 128
ROPE_BASE = 10000.0
S5_GROUP = 16
S5_STATE = 64
DSA_HEADS = 8
DSA_KV_HEADS = 2
DSA_HEAD_DIM = 128
IDX_HEADS = 8
IDX_DIM = 64
DSA_TOPK = 256
LN_EPS = 1e-5
GN_EPS = 1e-5
N_BRANCHES = 3

LANES = 128
SUBLANES = 8
VMEM_LIMIT_CAP = 56 * 1024 * 1024
S5_CHUNK = 16
S5_OCT = LANES // S5_GROUP
SAMPLE_ROWS = 16
MASK_NEG = -1e30
INT_MIN = -2 ** 31


def _cparams(semantics, *block_bytes):
    est = 2 * sum(block_bytes) + (8 << 20)
    return pltpu.CompilerParams(dimension_semantics=semantics,
                                vmem_limit_bytes=int(min(max(est, 32 << 20), VMEM_LIMIT_CAP)))


def _nbytes(shape, dtype):
    return int(np.prod(shape)) * jnp.dtype(dtype).itemsize


def _dot(a, b):
    return jnp.dot(a, b, preferred_element_type=F32)


def _dot_nt(a, b):
    return lax.dot_general(a, b, (((1,), (1,)), ((), ())), preferred_element_type=F32)


def _layernorm(y, g, b):
    mu = jnp.mean(y, axis=-1, keepdims=True)
    yc = y - mu
    var = jnp.mean(yc * yc, axis=-1, keepdims=True)
    return yc * lax.rsqrt(var + LN_EPS) * g + b


def _col_of_row(r):
    n = r.shape[1]
    eye = lax.broadcasted_iota(I32, (n, n), 0) == lax.broadcasted_iota(I32, (n, n), 1)
    return jnp.sum(jnp.where(eye, jnp.broadcast_to(r, (n, n)), 0.0), axis=1, keepdims=True)


def _mm_kernel(x_ref, w_ref, *o_refs):
    acc = _dot(x_ref[...], w_ref[...])
    for o_ref in o_refs:
        o_ref[...] = acc.astype(o_ref.dtype)


def _mm_oct_kernel(x_ref, w_ref, o_ref):
    acc = _dot(x_ref[...], w_ref[...])
    for i in range(o_ref.shape[0]):
        o_ref[i] = acc[:, i * LANES:(i + 1) * LANES]


def _mm(x, w, out_dtypes, oct_layout=False, name="mm"):
    m, k = x.shape
    n = w.shape[1]
    tm = min(m, 1024)
    tn = n if n <= 1024 else 512
    assert m % tm == 0 and n % tn == 0
    w_resident = (n // tn) * m * k + k * n <= (m // tm) * k * n + m * k
    if w_resident:
        grid = (n // tn, m // tm)
        xi, wi, oi = (lambda j, i: (i, 0)), (lambda j, i: (0, j)), (lambda j, i: (i, j))
        ooct = lambda j, i: (j, i, 0)
    else:
        grid = (m // tm, n // tn)
        xi, wi, oi = (lambda i, j: (i, 0)), (lambda i, j: (0, j)), (lambda i, j: (i, j))
        ooct = lambda i, j: (j, i, 0)
    in_specs = [pl.BlockSpec((tm, k), xi), pl.BlockSpec((k, tn), wi)]
    blk = [_nbytes((tm, k), BF16), _nbytes((k, tn), BF16)]
    if oct_layout:
        out_shape = jax.ShapeDtypeStruct((n // LANES, m, LANES), F32)
        out_specs = pl.BlockSpec((tn // LANES, tm, LANES), ooct)
        body = _mm_oct_kernel
        blk.append(_nbytes((tm, tn), F32))
    else:
        out_shape = tuple(jax.ShapeDtypeStruct((m, n), d) for d in out_dtypes)
        out_specs = tuple(pl.BlockSpec((tm, tn), oi) for _ in out_dtypes)
        body = _mm_kernel
        blk += [_nbytes((tm, tn), d) for d in out_dtypes]
    out = pl.pallas_call(body, grid=grid, in_specs=in_specs, out_specs=out_specs, out_shape=out_shape,
                         compiler_params=_cparams(("arbitrary", "arbitrary"), *blk), name=name)(x, w)
    return out if oct_layout else (out[0] if len(out_dtypes) == 1 else out)


def _ffn_kernel(x_ref, wg_ref, wu_ref, wd_ref, g_ref, b_ref, o_ref, ob_ref, xb_scr, acc_scr, *, alpha):
    f = pl.program_id(1)

    @pl.when(f == 0)
    def _():
        xb_scr[...] = x_ref[...].astype(BF16)
        acc_scr[...] = jnp.zeros_like(acc_scr)

    xb = xb_scr[...]
    hg = _dot(xb, wg_ref[...])
    hu = _dot(xb, wu_ref[...])
    h = hg * jax.nn.sigmoid(hg) * hu
    acc_scr[...] += _dot(h.astype(BF16), wd_ref[...])

    @pl.when(f == pl.num_programs(1) - 1)
    def _():
        y = _layernorm(alpha * x_ref[...] + 0.5 * acc_scr[...], g_ref[...], b_ref[...])
        o_ref[...] = y
        ob_ref[...] = y.astype(BF16)


def _ffn_ln(x, wg, wu, wd, g, b, alpha):
    m, d = x.shape
    dff = wg.shape[1]
    tm = min(m, 512)
    tf = 512
    assert m % tm == 0 and dff % tf == 0
    row = lambda i, f: (i, 0)
    blk = [_nbytes((tm, d), F32), 3 * _nbytes((d, tf), BF16), _nbytes((tm, d), F32), _nbytes((tm, d), BF16),
           _nbytes((tm, d), F32)]
    return pl.pallas_call(
        functools.partial(_ffn_kernel, alpha=alpha),
        grid=(m // tm, dff // tf),
        in_specs=[pl.BlockSpec((tm, d), row),
                  pl.BlockSpec((d, tf), lambda i, f: (0, f)),
                  pl.BlockSpec((d, tf), lambda i, f: (0, f)),
                  pl.BlockSpec((tf, d), lambda i, f: (f, 0)),
                  pl.BlockSpec((1, d), lambda i, f: (0, 0)),
                  pl.BlockSpec((1, d), lambda i, f: (0, 0))],
        out_specs=(pl.BlockSpec((tm, d), row), pl.BlockSpec((tm, d), row)),
        out_shape=(jax.ShapeDtypeStruct((m, d), F32), jax.ShapeDtypeStruct((m, d), BF16)),
        scratch_shapes=[pltpu.VMEM((tm, d), BF16), pltpu.VMEM((tm, d), F32)],
        compiler_params=_cparams(("arbitrary", "arbitrary"), *blk),
        name="ffn_ln",
    )(x, wg, wu, wd, g.reshape(1, d), b.reshape(1, d))


def _glu_kernel(y_ref, w_ref, o_ref):
    y = jnp.concatenate([y_ref[i] for i in range(y_ref.shape[0])], axis=-1)
    o_ref[...] = (y * jax.nn.sigmoid(_dot(y.astype(BF16), w_ref[...]))).astype(o_ref.dtype)


def _glu(y_oct, w):
    no, m, _ = y_oct.shape
    n = w.shape[1]
    tm = min(m, 512)
    blk = [_nbytes((no, tm, LANES), F32), _nbytes(w.shape, BF16), _nbytes((tm, n), BF16)]
    return pl.pallas_call(
        _glu_kernel, grid=(m // tm,),
        in_specs=[pl.BlockSpec((no, tm, LANES), lambda i: (0, i, 0)), pl.BlockSpec(w.shape, lambda i: (0, 0))],
        out_specs=pl.BlockSpec((tm, n), lambda i: (i, 0)),
        out_shape=jax.ShapeDtypeStruct((m, n), BF16),
        compiler_params=_cparams(("arbitrary",), *blk),
        name="s5_glu",
    )(y_oct, w)


def _merge_kernel(o_ref, z_ref, a_ref, g0_ref, g1_ref, g2_ref, wr_ref, ws_ref, wd_ref, m_ref):
    ret = _dot(o_ref[...], wr_ref[...])
    s5 = _dot(z_ref[...], ws_ref[...])
    dsa = _dot(a_ref[...], wd_ref[...])
    merged = (jax.nn.sigmoid(g0_ref[...]) * ret + jax.nn.sigmoid(g1_ref[...]) * s5
              + jax.nn.sigmoid(g2_ref[...]) * dsa)
    m_ref[...] = merged.astype(m_ref.dtype)


def _merge(o, z, a, gates, wr, ws, wd):
    m = o.shape[0]
    d = wr.shape[1]
    tm = min(m, 512)
    tn = min(d, 512)
    nb = d // tn
    blk = [_nbytes((tm, o.shape[1]), BF16), 2 * _nbytes((tm, z.shape[1]), BF16), 3 * _nbytes((tm, tn), F32),
           _nbytes((o.shape[1], tn), BF16), 2 * _nbytes((z.shape[1], tn), BF16), _nbytes((tm, tn), BF16)]
    gate_spec = lambda br: pl.BlockSpec((tm, tn), lambda i, j: (i, br * nb + j))
    return pl.pallas_call(
        _merge_kernel, grid=(m // tm, nb),
        in_specs=[pl.BlockSpec((tm, o.shape[1]), lambda i, j: (i, 0)),
                  pl.BlockSpec((tm, z.shape[1]), lambda i, j: (i, 0)),
                  pl.BlockSpec((tm, a.shape[1]), lambda i, j: (i, 0)),
                  gate_spec(0), gate_spec(1), gate_spec(2),
                  pl.BlockSpec((wr.shape[0], tn), lambda i, j: (0, j)),
                  pl.BlockSpec((ws.shape[0], tn), lambda i, j: (0, j)),
                  pl.BlockSpec((wd.shape[0], tn), lambda i, j: (0, j))],
        out_specs=pl.BlockSpec((tm, tn), lambda i, j: (i, j)),
        out_shape=jax.ShapeDtypeStruct((m, d), BF16),
        compiler_params=_cparams(("arbitrary", "arbitrary"), *blk),
        name="branch_merge",
    )(o, z, a, gates, gates, gates, wr, ws, wd)


def _out_ln_kernel(x_ref, m_ref, w_ref, g_ref, b_ref, o_ref, ob_ref, *, alpha):
    y = _layernorm(alpha * x_ref[...] + _dot(m_ref[...], w_ref[...]), g_ref[...], b_ref[...])
    o_ref[...] = y
    ob_ref[...] = y.astype(BF16)


def _out_ln(x, merged, w, g, b, alpha):
    m, d = x.shape
    tm = min(m, 512)
    row = lambda i: (i, 0)
    blk = [2 * _nbytes((tm, d), F32), 2 * _nbytes((tm, d), BF16), _nbytes((d, d), BF16)]
    return pl.pallas_call(
        functools.partial(_out_ln_kernel, alpha=alpha), grid=(m // tm,),
        in_specs=[pl.BlockSpec((tm, d), row), pl.BlockSpec((tm, d), row), pl.BlockSpec((d, d), lambda i: (0, 0)),
                  pl.BlockSpec((1, d), lambda i: (0, 0)), pl.BlockSpec((1, d), lambda i: (0, 0))],
        out_specs=(pl.BlockSpec((tm, d), row), pl.BlockSpec((tm, d), row)),
        out_shape=(jax.ShapeDtypeStruct((m, d), F32), jax.ShapeDtypeStruct((m, d), BF16)),
        compiler_params=_cparams(("arbitrary",), *blk),
        name="out_ln",
    )(x, merged, w, g.reshape(1, d), b.reshape(1, d))


def _rope(x, cos, sin):
    half = x.shape[-1] // 2
    x1, x2 = x[:, :half], x[:, half:]
    return jnp.concatenate([x1 * cos - x2 * sin, x1 * sin + x2 * cos], axis=-1)


def _group_norm_gate(o, gate):
    mu = jnp.mean(o, axis=-1, keepdims=True)
    oc = o - mu
    var = jnp.mean(oc * oc, axis=-1, keepdims=True)
    return gate * jax.nn.sigmoid(gate) * (oc * lax.rsqrt(var + GN_EPS))


def _ret_kernel(qk_ref, v_ref, g_ref, cos_ref, sin_ref, o_ref, st_ref):
    @pl.when(pl.program_id(1) == 0)
    def _():
        st_ref[...] = jnp.zeros_like(st_ref)

    c = qk_ref.shape[0]
    cos, sin = cos_ref[...], sin_ref[...]
    ri = lax.broadcasted_iota(I32, (c, c), 0).astype(F32)
    ci = lax.broadcasted_iota(I32, (c, c), 1).astype(F32)
    diff = ri - ci
    ti = lax.broadcasted_iota(I32, (c, 1), 0).astype(F32)
    for h in range(RET_HEADS):
        lg = math.log(1.0 - 2.0 ** (-5.0 - h))
        dmask = jnp.where(diff >= 0, jnp.exp(jnp.maximum(diff, 0.0) * lg), 0.0)
        q_dec = jnp.exp((ti + 1.0) * lg)
        k_dec = jnp.exp((c - 1.0 - ti) * lg)
        c_dec = math.exp(c * lg)
        q = _rope(qk_ref[:, h * RET_DK:(h + 1) * RET_DK], cos, sin)
        k = _rope(qk_ref[:, (RET_HEADS + h) * RET_DK:(RET_HEADS + h + 1) * RET_DK], cos, sin) * (RET_DK ** -0.5)
        v = v_ref[:, h * RET_DV:(h + 1) * RET_DV]
        s = st_ref[0, h]
        qb = q.astype(BF16)
        sc = _dot_nt(qb, k.astype(BF16)) * dmask
        o = _dot(sc.astype(BF16), v) + _dot(qb, s.astype(BF16)) * q_dec
        kd_t = jnp.transpose(k * k_dec).astype(BF16)
        st_ref[0, h] = c_dec * s + _dot(kd_t, v)
        o_ref[:, h * RET_DV:(h + 1) * RET_DV] = _group_norm_gate(
            o, g_ref[:, h * RET_DV:(h + 1) * RET_DV]).astype(o_ref.dtype)


def _retention(qk, v, gate, cos, sin, bn):
    m = qk.shape[0]
    length = m // bn
    c = RET_CHUNK
    nc = length // c
    hv = RET_HEADS * RET_DV
    row = lambda b, j: (b * nc + j, 0)
    blk = [_nbytes((c, qk.shape[1]), F32), _nbytes((c, hv), BF16), _nbytes((c, hv), F32), _nbytes((c, hv), BF16),
           _nbytes((RET_HEADS, RET_DK, RET_DV), F32)]
    return pl.pallas_call(
        _ret_kernel, grid=(bn, nc),
        in_specs=[pl.BlockSpec((c, qk.shape[1]), row), pl.BlockSpec((c, hv), row), pl.BlockSpec((c, hv), row),
                  pl.BlockSpec((c, RET_DK // 2), lambda b, j: (j, 0)),
                  pl.BlockSpec((c, RET_DK // 2), lambda b, j: (j, 0))],
        out_specs=(pl.BlockSpec((c, hv), row),
                   pl.BlockSpec((1, RET_HEADS, RET_DK, RET_DV), lambda b, j: (b, 0, 0, 0))),
        out_shape=(jax.ShapeDtypeStruct((m, hv), BF16),
                   jax.ShapeDtypeStruct((bn, RET_HEADS, RET_DK, RET_DV), F32)),
        compiler_params=_cparams(("arbitrary", "arbitrary"), *blk),
        name="retention",
    )(qk, v, gate, cos, sin)


def _ret_step_kernel(qk_ref, v_ref, g_ref, cos_ref, sin_ref, s0_ref, o_ref, st_ref):
    row = pl.ds(pl.program_id(0), 1)
    cos, sin = cos_ref[...], sin_ref[...]
    for h in range(RET_HEADS):
        decay = 1.0 - 2.0 ** (-5.0 - h)
        q = _rope(qk_ref[row, h * RET_DK:(h + 1) * RET_DK], cos, sin)
        k = _rope(qk_ref[row, (RET_HEADS + h) * RET_DK:(RET_HEADS + h + 1) * RET_DK], cos, sin) * (RET_DK ** -0.5)
        v = v_ref[row, h * RET_DV:(h + 1) * RET_DV]
        s0 = s0_ref[0, h]
        st_ref[0, h] = decay * s0 + _col_of_row(k) * v
        o = (jnp.sum(q * k, axis=1, keepdims=True) * v
             + jnp.sum(_col_of_row(q) * s0, axis=0, keepdims=True) * decay)
        o_ref[0, :, h * RET_DV:(h + 1) * RET_DV] = _group_norm_gate(o, g_ref[row, h * RET_DV:(h + 1) * RET_DV])


def _retention_step(qk, v, gate, cos, sin, s0):
    bn = s0.shape[0]
    const = lambda b: (0, 0)
    state = pl.BlockSpec((1,) + s0.shape[1:], lambda b: (b, 0, 0, 0))
    return pl.pallas_call(
        _ret_step_kernel, grid=(bn,),
        in_specs=[pl.BlockSpec(qk.shape, const), pl.BlockSpec(v.shape, const), pl.BlockSpec(gate.shape, const),
                  pl.BlockSpec(cos.shape, const), pl.BlockSpec(sin.shape, const), state],
        out_specs=(pl.BlockSpec((1, 1, v.shape[1]), lambda b: (b, 0, 0)), state),
        out_shape=(jax.ShapeDtypeStruct((bn, 1, v.shape[1]), F32), jax.ShapeDtypeStruct(s0.shape, F32)),
        compiler_params=_cparams(("arbitrary",), 2 * _nbytes(s0.shape[1:], F32)),
        name="retention_step",
    )(qk, v, gate, cos, sin, s0)


def _s5_tables(a_re, a_im, log_dt, b_re, b_im, c_re, c_im, d):
    hp = lax.Precision.HIGHEST
    g, p = a_re.shape
    nc = b_re.shape[-1]
    no = g // S5_OCT
    t = S5_CHUNK
    dt = jnp.exp(log_dt)[:, None]
    mag = jnp.exp(a_re * dt)
    ab_re = mag * jnp.cos(a_im * dt)
    ab_im = mag * jnp.sin(a_im * dt)
    den = a_re * a_re + a_im * a_im
    x_re = ab_re - 1.0
    f_re = (x_re * a_re + ab_im * a_im) / den
    f_im = (ab_im * a_re - x_re * a_im) / den
    bb_re = f_re[..., None] * b_re - f_im[..., None] * b_im
    bb_im = f_re[..., None] * b_im + f_im[..., None] * b_re

    def powers(n):
        n = n.astype(F32)[:, None, None]
        pmag = jnp.exp(n * (a_re * dt)[None])
        return pmag * jnp.cos(n * (a_im * dt)[None]), pmag * jnp.sin(n * (a_im * dt)[None])

    pw_re, pw_im = powers(jnp.arange(t + 1))
    rev_re, rev_im = powers(t - 1 - jnp.arange(t))
    eye = jnp.eye(S5_OCT, dtype=F32)

    def c_times(power_re, power_im):
        return (c_re[None] * power_re[:, :, None, :] - c_im[None] * power_im[:, :, None, :],
                c_re[None] * power_im[:, :, None, :] + c_im[None] * power_re[:, :, None, :])

    cp_re, cp_im = c_times(pw_re[:t], pw_im[:t])
    kern = (jnp.einsum('ngdp,gpc->gndc', cp_re, bb_re, precision=hp)
            - jnp.einsum('ngdp,gpc->gndc', cp_im, bb_im, precision=hp))
    lag = jnp.arange(t)[None, :] - jnp.arange(t)[:, None]
    kt = kern[:, jnp.clip(lag, 0, t - 1)] * (lag >= 0)[None, :, :, None, None]
    kt = kt.reshape(no, S5_OCT, t, t, nc, nc)
    toep = jnp.einsum('oistdc,ij->osictjd', kt, eye).reshape(no, t * LANES, t * LANES)

    def b_rows(power_re, power_im):
        e_re = power_re[..., None] * bb_re[None] - power_im[..., None] * bb_im[None]
        e_im = power_re[..., None] * bb_im[None] + power_im[..., None] * bb_re[None]
        out = []
        for e in (e_re, e_im):
            e = e.reshape(e.shape[0], no, S5_OCT, p, nc)
            out.append(jnp.einsum('soipc,ij->osicjp', e, eye).reshape(no, e.shape[0] * LANES, S5_OCT * p))
        return jnp.concatenate(out, axis=-1)

    def c_cols(power_re, power_im):
        a_r, a_i = c_times(power_re, power_im)
        out = []
        for a in (a_r, -a_i):
            a = a.reshape(a.shape[0], no, S5_OCT, nc, p)
            out.append(jnp.einsum('toidp,ij->oiptjd', a, eye).reshape(no, S5_OCT * p, a.shape[0] * LANES))
        return jnp.concatenate(out, axis=1)

    def state_row(v_re, v_im):
        return jnp.concatenate([v_re.reshape(no, 1, S5_OCT * p), v_im.reshape(no, 1, S5_OCT * p)], axis=-1)

    d_row = d.reshape(no, 1, LANES)
    return dict(
        toep=toep.astype(BF16),
        b_end=b_rows(rev_re, rev_im).astype(BF16),
        c_in=c_cols(pw_re[1:], pw_im[1:]).astype(BF16),
        a_row=state_row(pw_re[t], pw_im[t]),
        d_row=jnp.tile(d_row, (1, 1, t)),
        b_one=b_rows(pw_re[:1], pw_im[:1]).astype(BF16),
        c_one=c_cols(pw_re[:1], pw_im[:1]).astype(BF16),
        a_one=state_row(ab_re, ab_im),
        d_one=d_row,
    )


def _s5_kernel(u_ref, toep_ref, bend_ref, cin_ref, a_ref, d_ref, y_ref, h_ref, e_scr, s_scr):
    rows, t = e_scr.shape[0], S5_CHUNK
    u = jnp.concatenate([u_ref[0, pl.ds(i, rows, stride=t), :] for i in range(t)], axis=-1)
    ub = u.astype(BF16)
    e_scr[...] = _dot(ub, bend_ref[0])
    half = a_ref.shape[-1] // 2
    a_re, a_im = a_ref[0, :, :half], a_ref[0, :, half:]

    def step(k, carry):
        s_re, s_im = carry
        s_scr[pl.ds(k, 1), :] = jnp.concatenate([s_re, s_im], axis=-1)
        e = e_scr[pl.ds(k, 1), :]
        return (a_re * s_re - a_im * s_im + e[:, :half], a_re * s_im + a_im * s_re + e[:, half:])

    zero = jnp.zeros((1, half), F32)
    s_re, s_im = lax.fori_loop(0, rows, step, (zero, zero))
    h_ref[0, 0] = jnp.concatenate([s_re, s_im], axis=-1)
    y = jax.nn.gelu(_dot(ub, toep_ref[0]) + _dot(s_scr[...].astype(BF16), cin_ref[0]) + d_ref[0] * u)
    for i in range(t):
        y_ref[0, pl.ds(i, rows, stride=t), :] = y[:, i * LANES:(i + 1) * LANES]


def _s5(u_oct, tb, bn):
    no, m, _ = u_oct.shape
    t = S5_CHUNK
    steps = m // bn
    rows = steps // t
    w = t * LANES
    ns = tb['a_row'].shape[-1]
    oct_blk = lambda shape: pl.BlockSpec((1,) + shape, lambda o, b: (o, 0, 0))
    blk = [2 * _nbytes((rows, w), F32), _nbytes((w, w), BF16), 2 * _nbytes((w, ns), BF16),
           2 * _nbytes((rows, ns), F32)]
    return pl.pallas_call(
        _s5_kernel, grid=(no, bn),
        in_specs=[pl.BlockSpec((1, steps, LANES), lambda o, b: (o, b, 0)),
                  oct_blk((w, w)), oct_blk((w, ns)), oct_blk((ns, w)), oct_blk((1, ns)), oct_blk((1, w))],
        out_specs=(pl.BlockSpec((1, steps, LANES), lambda o, b: (o, b, 0)),
                   pl.BlockSpec((1, 1, 1, ns), lambda o, b: (b, o, 0, 0))),
        out_shape=(jax.ShapeDtypeStruct(u_oct.shape, F32), jax.ShapeDtypeStruct((bn, no, 1, ns), F32)),
        scratch_shapes=[pltpu.VMEM((rows, ns), F32), pltpu.VMEM((rows, ns), F32)],
        compiler_params=_cparams(("arbitrary", "arbitrary"), *blk),
        name="s5_scan",
    )(u_oct, tb['toep'], tb['b_end'], tb['c_in'], tb['a_row'], tb['d_row'])


def _s5_step_kernel(u_ref, h0_ref, b_ref, c_ref, a_ref, d_ref, y_ref, h_ref):
    u = u_ref[0]
    half = a_ref.shape[-1] // 2
    a_re, a_im = a_ref[0, :, :half], a_ref[0, :, half:]
    h0 = h0_ref[0]
    h0_re, h0_im = h0[:, :half], h0[:, half:]
    bu = _dot(u.astype(BF16), b_ref[0])
    h_re = a_re * h0_re - a_im * h0_im + bu[:, :half]
    h_im = a_re * h0_im + a_im * h0_re + bu[:, half:]
    h = jnp.concatenate([h_re, h_im], axis=-1)
    h_ref[0] = h
    y_ref[0] = jax.nn.gelu(_dot(h.astype(BF16), c_ref[0]) + d_ref[0] * u)


def _s5_step(u_oct, h0, tb):
    no, rows, _ = u_oct.shape
    ns = h0.shape[-1]
    o3 = lambda shape: pl.BlockSpec((1,) + shape, lambda o: (o, 0, 0))
    return pl.pallas_call(
        _s5_step_kernel, grid=(no,),
        in_specs=[o3((rows, LANES)), o3((rows, ns)), o3((LANES, ns)), o3((ns, LANES)), o3((1, ns)), o3((1, LANES))],
        out_specs=(o3((rows, LANES)), o3((rows, ns))),
        out_shape=(jax.ShapeDtypeStruct(u_oct.shape, F32), jax.ShapeDtypeStruct(h0.shape, F32)),
        compiler_params=_cparams(("arbitrary",), _nbytes((LANES, ns), BF16) * 2),
        name="s5_step",
    )(u_oct, h0, tb['b_one'], tb['c_one'], tb['a_one'], tb['d_one'])


def _sort_key(x):
    bits = pltpu.bitcast(x, I32)
    return jnp.where(bits >= 0, bits, bits ^ jnp.int32(0x7FFFFFFF))


def _kth_largest_key(count_ge, nsel, shape):
    res = jnp.where(count_ge(jnp.zeros(shape, I32)) >= nsel, jnp.int32(0), jnp.int32(INT_MIN))

    def bit_step(i, res):
        cand = res | jnp.left_shift(jnp.int32(1), 30 - i)
        return jnp.where(count_ge(cand) >= nsel, cand, res)

    return lax.fori_loop(0, 31, bit_step, res)


def _dsa_kernel(dq_ref, iq_ref, ikw_ref, ikb_ref, kv_ref, o_ref, key_scr, bias_scr, m_scr, l_scr, acc_scr,
                *, nsel):
    tq = dq_ref.shape[0]
    jq = pl.program_id(1)
    nkb = jq + 1
    krow = lax.broadcasted_iota(I32, (tq, tq), 0)
    qcol = lax.broadcasted_iota(I32, (tq, tq), 1)
    qpos = jq * tq + qcol
    hd = DSA_HEAD_DIM
    gsz = DSA_HEADS // DSA_KV_HEADS

    def over_keys(x, op):
        part = op(x.reshape(x.shape[0] // SUBLANES, SUBLANES, x.shape[1]), axis=0)
        return op(part, axis=0, keepdims=True)

    iq_st = jnp.concatenate([iq_ref[:, h * LANES:(h + 1) * LANES] for h in range(IDX_HEADS)], axis=0)
    w_t = jnp.transpose(ikw_ref[...])

    def score_block(kb, carry):
        ik = ikb_ref[pl.ds(pl.multiple_of(kb * tq, tq), tq), :]
        sh = jnp.maximum(_dot_nt(ik, iq_st), 0.0)
        acc = jnp.zeros((tq, tq), F32)
        for h in range(IDX_HEADS):
            acc = acc + sh[:, h * tq:(h + 1) * tq] * w_t[IDX_DIM + h:IDX_DIM + h + 1, :]
        acc = jnp.where(kb * tq + krow <= qpos, acc, -jnp.inf)
        key_scr[kb] = _sort_key(acc)
        return carry

    lax.fori_loop(0, nkb, score_block, 0)

    def count(pred):
        def body(kb, c):
            hit = jnp.where(pred(key_scr[kb]), 1.0, 0.0)
            return c + jnp.sum(hit.reshape(tq // SUBLANES, SUBLANES, tq), axis=0)
        part = lax.fori_loop(0, nkb, body, jnp.zeros((SUBLANES, tq), F32))
        return jnp.sum(part, axis=0, keepdims=True)

    kth = _kth_largest_key(lambda cand: count(lambda key: key >= cand), float(nsel), (1, tq))
    need = float(nsel) - count(lambda key: key > kth)
    tri = jnp.where(krow >= qcol, 1.0, 0.0).astype(BF16)

    def select_block(kb, taken):
        key = key_scr[kb]
        tie = jnp.where(key == kth, 1.0, 0.0)
        rank = taken + _dot(tri, tie.astype(BF16))
        sel = ((key > kth) | ((tie > 0.0) & (rank <= need))) & (kb * tq + krow <= qpos)
        bias_scr[kb] = jnp.where(sel, 0.0, MASK_NEG)
        return taken + over_keys(tie, jnp.sum)

    lax.fori_loop(0, nkb, select_block, jnp.zeros((1, tq), F32))

    for g in range(DSA_KV_HEADS):
        qg = jnp.concatenate([dq_ref[:, (g * gsz + i) * hd:(g * gsz + i + 1) * hd] for i in range(gsz)], axis=0)
        m_scr[...] = jnp.full_like(m_scr, -jnp.inf)
        l_scr[...] = jnp.zeros_like(l_scr)
        acc_scr[...] = jnp.zeros_like(acc_scr)

        def attend_block(kb, carry):
            ks = pl.ds(pl.multiple_of(kb * tq, tq), tq)
            kblk = kv_ref[ks, g * hd:(g + 1) * hd]
            vblk = kv_ref[ks, (DSA_KV_HEADS + g) * hd:(DSA_KV_HEADS + g + 1) * hd]
            bias = bias_scr[kb]
            lg = _dot_nt(kblk, qg) * (hd ** -0.5) + jnp.concatenate([bias] * gsz, axis=1)
            m_old = m_scr[...]
            m_new = jnp.maximum(m_old, over_keys(lg, jnp.max))
            alpha = jnp.exp(m_old - m_new)
            p = jnp.exp(lg - m_new)
            l_scr[...] = alpha * l_scr[...] + over_keys(p, jnp.sum)
            v_t = jnp.transpose(vblk.astype(F32)).astype(BF16)
            acc_scr[...] = alpha * acc_scr[...] + _dot(v_t, p.astype(BF16))
            m_scr[...] = m_new
            return carry

        lax.fori_loop(0, nkb, attend_block, 0)
        out_t = acc_scr[...] / l_scr[...]
        for i in range(gsz):
            o_ref[:, (g * gsz + i) * hd:(g * gsz + i + 1) * hd] = jnp.transpose(
                out_t[:, i * tq:(i + 1) * tq]).astype(o_ref.dtype)


def _dsa_prompt(dq, iq, ikw, ikb, kvb, bn):
    m = dq.shape[0]
    length = m // bn
    tq = 256
    nq = length // tq
    nsel = min(DSA_TOPK, length // 4)
    qrow = lambda b, j: (b * nq + j, 0)
    full = lambda b, j: (b, 0)
    gsz = DSA_HEADS // DSA_KV_HEADS
    blk = [_nbytes((tq, dq.shape[1]), BF16) * 3, _nbytes((length, LANES), BF16), _nbytes((length, kvb.shape[1]), BF16),
           _nbytes((nq, tq, tq), F32)]
    return pl.pallas_call(
        functools.partial(_dsa_kernel, nsel=nsel), grid=(bn, nq),
        in_specs=[pl.BlockSpec((tq, dq.shape[1]), qrow), pl.BlockSpec((tq, iq.shape[1]), qrow),
                  pl.BlockSpec((tq, LANES), qrow), pl.BlockSpec((length, LANES), full),
                  pl.BlockSpec((length, kvb.shape[1]), full)],
        out_specs=pl.BlockSpec((tq, dq.shape[1]), qrow),
        out_shape=jax.ShapeDtypeStruct(dq.shape, BF16),
        scratch_shapes=[pltpu.VMEM((nq, tq, tq), I32), pltpu.VMEM((nq, tq, tq), F32),
                        pltpu.VMEM((1, gsz * tq), F32), pltpu.VMEM((1, gsz * tq), F32),
                        pltpu.VMEM((DSA_HEAD_DIM, gsz * tq), F32)],
        compiler_params=_cparams(("arbitrary", "arbitrary"), *blk),
        name="dsa_prompt",
    )(dq, iq, ikw, ikb, kvb)


MAX_PAGES_PER_STEP = 16


def _idx_heads(iq_row):
    return jnp.concatenate([iq_row[:, h * LANES:h * LANES + IDX_DIM] for h in range(IDX_HEADS)], axis=0)


def _page_score_kernel(pt_ref, iq_ref, ikw_ref, *rest):
    page_refs, o_ref = rest[:-1], rest[-1]
    b = pl.program_id(0)
    iq_h = jnp.concatenate([_idx_heads(iq_ref[pl.ds(b, 1), :]),
                            jnp.zeros((SAMPLE_ROWS - IDX_HEADS, IDX_DIM), F32)], axis=0).astype(BF16)
    w_col = _col_of_row(ikw_ref[pl.ds(b, 1), IDX_DIM:IDX_DIM + SAMPLE_ROWS])
    for i, page in enumerate(page_refs):
        sh = jnp.maximum(_dot_nt(iq_h, page[...].astype(BF16)), 0.0)
        o_ref[0, i:i + 1, :] = jnp.sum(sh * w_col, axis=0, keepdims=True)


def _page_scores(page_table, iq, ikw, cache_idx, layer):
    bn, npages = page_table.shape
    pg = min(MAX_PAGES_PER_STEP, npages)
    const = lambda b, s, pt: (0, 0)

    def page_spec(i):
        return pl.BlockSpec((None, None, PAGE_SIZE, IDX_DIM), lambda b, s, pt: (layer, pt[b, s * pg + i], 0, 0))

    return pl.pallas_call(
        _page_score_kernel,
        grid_spec=pltpu.PrefetchScalarGridSpec(
            num_scalar_prefetch=1, grid=(bn, npages // pg),
            in_specs=[pl.BlockSpec(iq.shape, const), pl.BlockSpec(ikw.shape, const)]
            + [page_spec(i) for i in range(pg)],
            out_specs=pl.BlockSpec((1, pg, PAGE_SIZE), lambda b, s, pt: (b, s, 0))),
        out_shape=jax.ShapeDtypeStruct((bn, npages, PAGE_SIZE), F32),
        compiler_params=_cparams(("arbitrary", "arbitrary")),
        name="page_scores",
    )(page_table, iq, ikw, *([cache_idx] * pg))


def _page_select_kernel(sc_ref, iq_ref, ikw_ref, bias_ref, bias_self_ref, *, nsel):
    b = pl.program_id(0)
    npages, psz = sc_ref.shape[1:]
    iq_h = _idx_heads(iq_ref[pl.ds(b, 1), :])
    ikw = ikw_ref[pl.ds(b, 1), :]
    w_col = _col_of_row(ikw[:, IDX_DIM:IDX_DIM + IDX_HEADS])
    s_self = jnp.sum(jnp.maximum(jnp.sum(iq_h * ikw[:, :IDX_DIM], axis=1, keepdims=True), 0.0) * w_col,
                     axis=0, keepdims=True)
    key = _sort_key(sc_ref[0])
    key_self = _sort_key(s_self)

    def total(x):
        return jnp.sum(jnp.sum(x, axis=1, keepdims=True), axis=0, keepdims=True)

    def count_ge(cand):
        return total(jnp.where(key >= cand, 1.0, 0.0)) + jnp.where(key_self >= cand, 1.0, 0.0)

    kth = _kth_largest_key(count_ge, float(nsel), (1, 1))
    need = float(nsel) - (total(jnp.where(key > kth, 1.0, 0.0)) + jnp.where(key_self > kth, 1.0, 0.0))
    tie = jnp.where(key == kth, 1.0, 0.0)
    r_in = lax.broadcasted_iota(I32, (psz, psz), 0)
    c_in = lax.broadcasted_iota(I32, (psz, psz), 1)
    in_page = _dot(tie.astype(BF16), jnp.where(r_in <= c_in, 1.0, 0.0).astype(BF16))
    per_page = jnp.broadcast_to(jnp.sum(tie, axis=1, keepdims=True), (npages, psz)).astype(BF16)
    r_pg = lax.broadcasted_iota(I32, (npages, npages), 0)
    c_pg = lax.broadcasted_iota(I32, (npages, npages), 1)
    before = _dot(jnp.where(c_pg < r_pg, 1.0, 0.0).astype(BF16), per_page)
    sel = (key > kth) | ((tie > 0.0) & (before + in_page <= need))
    r_dup = lax.broadcasted_iota(I32, (psz, bias_ref.shape[2]), 0)
    c_dup = lax.broadcasted_iota(I32, (psz, bias_ref.shape[2]), 1)
    dup = jnp.where(c_dup // DSA_KV_HEADS == r_dup, 1.0, 0.0).astype(BF16)
    spread = _dot(jnp.where(sel, 1.0, 0.0).astype(BF16), dup)
    bias_ref[0] = jnp.where(spread > 0.5, 0.0, MASK_NEG)
    sel_self = (key_self > kth) | ((key_self == kth) & (total(tie) + 1.0 <= need))
    bias_self_ref[0] = jnp.broadcast_to(jnp.where(sel_self, 0.0, MASK_NEG), (1, LANES))


def _page_select(scores, iq, ikw, nsel):
    bn, npages, psz = scores.shape
    const = lambda b: (0, 0)
    wide = DSA_KV_HEADS * psz
    return pl.pallas_call(
        functools.partial(_page_select_kernel, nsel=nsel), grid=(bn,),
        in_specs=[pl.BlockSpec((1, npages, psz), lambda b: (b, 0, 0)),
                  pl.BlockSpec(iq.shape, const), pl.BlockSpec(ikw.shape, const)],
        out_specs=(pl.BlockSpec((1, npages, wide), lambda b: (b, 0, 0)),
                   pl.BlockSpec((1, 1, LANES), lambda b: (b, 0, 0))),
        out_shape=(jax.ShapeDtypeStruct((bn, npages, wide), F32), jax.ShapeDtypeStruct((bn, 1, LANES), F32)),
        compiler_params=_cparams(("arbitrary",)),
        name="page_select",
    )(scores, iq, ikw)


def _page_attend_kernel(pt_ref, dq_ref, kvs_ref, bias_ref, bself_ref, *rest, npg):
    k_refs, v_refs = rest[:npg], rest[npg:2 * npg]
    o_ref, m_scr, l_scr, acc_scr = rest[2 * npg:]
    b = pl.program_id(0)
    s = pl.program_id(1)
    hd = DSA_HEAD_DIM
    gsz = DSA_HEADS // DSA_KV_HEADS
    scale = hd ** -0.5
    prow = k_refs[0].shape[0]
    dq_row = dq_ref[pl.ds(b, 1), :]
    q = jnp.concatenate([dq_row[:, h * hd:(h + 1) * hd] for h in range(DSA_HEADS)]
                        + [jnp.zeros((SAMPLE_ROWS - DSA_HEADS, hd), F32)], axis=0)
    kv_of_head = lax.broadcasted_iota(I32, (SAMPLE_ROWS, 1), 0) // gsz

    @pl.when(s == 0)
    def _():
        m_scr[...] = jnp.full_like(m_scr, -jnp.inf)
        l_scr[...] = jnp.zeros_like(l_scr)
        acc_scr[...] = jnp.zeros_like(acc_scr)

    def update(lg, pv_of):
        m_old = m_scr[...]
        m_new = jnp.maximum(m_old, jnp.max(lg, axis=1, keepdims=True))
        alpha = jnp.exp(m_old - m_new)
        p = jnp.exp(lg - m_new)
        l_scr[...] = alpha * l_scr[...] + jnp.sum(p, axis=1, keepdims=True)
        acc_scr[...] = alpha * acc_scr[...] + pv_of(p)
        m_scr[...] = m_new

    qb = q.astype(BF16)
    lg = jnp.concatenate([_dot_nt(qb, k_refs[i][...].astype(BF16)) for i in range(npg)], axis=1) * scale
    bias = jnp.concatenate([bias_ref[0, i:i + 1, :] for i in range(npg)], axis=1)
    col = lax.broadcasted_iota(I32, lg.shape, 1)
    lg = jnp.where(jnp.bitwise_and(col, DSA_KV_HEADS - 1) == kv_of_head, lg + bias, MASK_NEG)

    def pv_pages(p):
        pb = p.astype(BF16)
        acc = jnp.zeros((SAMPLE_ROWS, hd), F32)
        for i in range(npg):
            acc = acc + _dot(pb[:, i * prow:(i + 1) * prow], v_refs[i][...].astype(BF16))
        return acc

    update(lg, pv_pages)

    @pl.when(s == pl.num_programs(1) - 1)
    def _():
        kvs = kvs_ref[pl.ds(b, 1), :]
        k_self = jnp.where(kv_of_head == 0, kvs[:, :hd], kvs[:, hd:2 * hd])
        v_self = jnp.where(kv_of_head == 0, kvs[:, 2 * hd:3 * hd], kvs[:, 3 * hd:])
        lg_self = jnp.sum(q * k_self, axis=1, keepdims=True) * scale + bself_ref[0, :, :1]
        update(lg_self, lambda p: p * v_self)
        out = acc_scr[...] / l_scr[...]
        for h in range(DSA_HEADS):
            o_ref[0, :, h * hd:(h + 1) * hd] = out[h:h + 1]


def _page_attend(page_table, dq, kvs, bias, bias_self, cache_k, cache_v, layer):
    assert DSA_KV_HEADS == 2
    bn, npages = page_table.shape
    pg = min(MAX_PAGES_PER_STEP, npages)
    prow, hd = cache_k.shape[-2:]
    const = lambda b, s, pt: (0, 0)

    def page_spec(i):
        return pl.BlockSpec((None, None, prow, hd), lambda b, s, pt: (layer, pt[b, s * pg + i], 0, 0))

    return pl.pallas_call(
        functools.partial(_page_attend_kernel, npg=pg),
        grid_spec=pltpu.PrefetchScalarGridSpec(
            num_scalar_prefetch=1, grid=(bn, npages // pg),
            in_specs=[pl.BlockSpec(dq.shape, const), pl.BlockSpec(kvs.shape, const),
                      pl.BlockSpec((1, pg, prow), lambda b, s, pt: (b, s, 0)),
                      pl.BlockSpec((1, 1, LANES), lambda b, s, pt: (b, 0, 0))]
            + [page_spec(i) for i in range(pg)] + [page_spec(i) for i in range(pg)],
            out_specs=pl.BlockSpec((1, 1, dq.shape[1]), lambda b, s, pt: (b, 0, 0)),
            scratch_shapes=[pltpu.VMEM((SAMPLE_ROWS, 1), F32), pltpu.VMEM((SAMPLE_ROWS, 1), F32),
                            pltpu.VMEM((SAMPLE_ROWS, hd), F32)]),
        out_shape=jax.ShapeDtypeStruct((bn, 1, dq.shape[1]), F32),
        compiler_params=_cparams(("arbitrary", "arbitrary"), 2 * pg * _nbytes((prow, hd), F32)),
        name="page_attend",
    )(page_table, dq, kvs, bias, bias_self, *([cache_k] * pg), *([cache_v] * pg))


def _rope_tables(pos):
    half = RET_DK // 2
    inv = 1.0 / (ROPE_BASE ** jnp.linspace(0.0, 1.0, half, dtype=F32))
    ang = pos.astype(F32)[:, None] * inv[None, :]
    return jnp.cos(ang), jnp.sin(ang)


def _in_proj_weights(w_in):
    hq = RET_HEADS * RET_DK
    hv = RET_HEADS * RET_DV
    d = w_in.shape[0]
    o = 0
    cuts = {}
    for name, width in (('qk', 2 * hq), ('v', hv), ('g', hv)):
        cuts[name] = (o, o + width)
        o += width
    s5w = (w_in.shape[1] - 2 * hq - 2 * hv - DSA_HEADS * DSA_HEAD_DIM - 2 * DSA_KV_HEADS * DSA_HEAD_DIM
           - IDX_HEADS * IDX_DIM - IDX_DIM - IDX_HEADS) // (1 + 2 * N_BRANCHES)
    for name, width in (('su', s5w), ('dq', DSA_HEADS * DSA_HEAD_DIM), ('kv', 2 * DSA_KV_HEADS * DSA_HEAD_DIM),
                        ('iq', IDX_HEADS * IDX_DIM), ('ikw', IDX_DIM + IDX_HEADS), ('gates', N_BRANCHES * d)):
        cuts[name] = (o, o + width)
        o += width
    assert o == w_in.shape[1]
    wb = {k: w_in[:, a:b].astype(BF16) for k, (a, b) in cuts.items()}
    iq = wb['iq'].reshape(d, IDX_HEADS, IDX_DIM)
    wb['iq'] = jnp.pad(iq, ((0, 0), (0, 0), (0, LANES - IDX_DIM))).reshape(d, IDX_HEADS * LANES)
    wb['ikw'] = jnp.pad(wb['ikw'], ((0, 0), (0, LANES - IDX_DIM - IDX_HEADS)))
    return wb


def _project(xb, wb, sample):
    act = F32 if sample else BF16
    p = {
        'qk': _mm(xb, wb['qk'], (F32,), name="proj_qk"),
        'v': _mm(xb, wb['v'], (act,), name="proj_v"),
        'g': _mm(xb, wb['g'], (F32,), name="proj_g"),
        'su': _mm(xb, wb['su'], None, oct_layout=True, name="proj_su"),
        'dq': _mm(xb, wb['dq'], (act,), name="proj_dq"),
        'iq': _mm(xb, wb['iq'], (act,), name="proj_iq"),
        'gates': _mm(xb, wb['gates'], (F32,), name="proj_gates"),
    }
    p['kv'], p['kvb'] = _mm(xb, wb['kv'], (F32, BF16), name="proj_kv")
    p['ikw'], p['ikb'] = _mm(xb, wb['ikw'], (F32, BF16), name="proj_ikw")
    return p


def _s5_state_out(h, groups):
    bn = h.shape[0]
    h = h.reshape(bn, groups // S5_OCT, 2, S5_OCT, S5_STATE)
    return h[:, :, 0].reshape(bn, groups, S5_STATE), h[:, :, 1].reshape(bn, groups, S5_STATE)


def kernel(x_prompt, x_sample, cache_k, cache_v, cache_idx_k, state_ret, state_s5_re, state_s5_im, page_table, ln1_g, ln1_b, ffn1_wg, ffn1_wu, ffn1_wd, w_in, s5_a_re, s5_a_im, s5_log_dt, s5_b_re, s5_b_im, s5_c_re, s5_c_im, s5_d, w_glu, w_ret_o, w_s5_o, w_dsa_o, w_out, ln2_g, ln2_b, ffn2_wg, ffn2_wu, ffn2_wd, ln3_g, ln3_b):
    bp, lp, d = x_prompt.shape
    bs, ls, _ = x_sample.shape
    depth = w_in.shape[0]
    assert ls == 1 and bs <= SAMPLE_ROWS
    npages = page_table.shape[1]
    past = npages * PAGE_SIZE
    groups = s5_a_re.shape[1]
    alpha = (2 * depth) ** 0.25
    kvw = DSA_KV_HEADS * DSA_HEAD_DIM
    nsel_s = min(DSA_TOPK, (past + ls) // 4)

    cos_p, sin_p = _rope_tables(jnp.arange(lp))
    cos_s, sin_s = _rope_tables(past + jnp.arange(ls))
    cache_k = cache_k.reshape(depth, -1, PAGE_SIZE * DSA_KV_HEADS, DSA_HEAD_DIM)
    cache_v = cache_v.reshape(depth, -1, PAGE_SIZE * DSA_KV_HEADS, DSA_HEAD_DIM)

    xp = x_prompt.reshape(bp * lp, d)
    xs = jnp.pad(x_sample.reshape(bs, d), ((0, SAMPLE_ROWS - bs), (0, 0)))
    outs_p, outs_s = [], []
    for l in range(depth):
        bf = lambda a: a[l].astype(BF16)
        wb = _in_proj_weights(w_in[l])
        tb = _s5_tables(s5_a_re[l], s5_a_im[l], s5_log_dt[l], s5_b_re[l], s5_b_im[l], s5_c_re[l], s5_c_im[l],
                        s5_d[l])
        w1 = (bf(ffn1_wg), bf(ffn1_wu), bf(ffn1_wd))
        w2 = (bf(ffn2_wg), bf(ffn2_wu), bf(ffn2_wd))
        wglu, wro, wso, wdo, wo = bf(w_glu), bf(w_ret_o), bf(w_s5_o), bf(w_dsa_o), bf(w_out)

        xp, xpb = _ffn_ln(xp, *w1, ln1_g[l], ln1_b[l], alpha)
        p = _project(xpb, wb, sample=False)
        o_ret, ret_p = _retention(p['qk'], p['v'], p['g'], cos_p, sin_p, bp)
        y_s5, h_p = _s5(p['su'], tb, bp)
        z = _glu(y_s5, wglu)
        att = _dsa_prompt(p['dq'], p['iq'], p['ikw'], p['ikb'], p['kvb'], bp)
        merged = _merge(o_ret, z, att, p['gates'], wro, wso, wdo)
        xp, xpb = _out_ln(xp, merged, wo, ln2_g[l], ln2_b[l], alpha)
        xp, xpb = _ffn_ln(xp, *w2, ln3_g[l], ln3_b[l], alpha)
        s5r_p, s5i_p = _s5_state_out(h_p, groups)
        outs_p.append((p['kv'][:, :kvw].reshape(bp, lp, DSA_KV_HEADS, DSA_HEAD_DIM),
                       p['kv'][:, kvw:].reshape(bp, lp, DSA_KV_HEADS, DSA_HEAD_DIM),
                       p['ikw'][:, :IDX_DIM].reshape(bp, lp, IDX_DIM), ret_p, s5r_p, s5i_p))

        xs, xsb = _ffn_ln(xs, *w1, ln1_g[l], ln1_b[l], alpha)
        q = _project(xsb, wb, sample=True)
        o_ret_s, ret_s = _retention_step(q['qk'], q['v'], q['g'], cos_s, sin_s, state_ret[l])
        o_ret_s = jnp.pad(o_ret_s[:, 0], ((0, SAMPLE_ROWS - bs), (0, 0))).astype(BF16)
        h0 = jnp.concatenate([state_s5_re[l].reshape(bs, groups // S5_OCT, S5_OCT * S5_STATE),
                              state_s5_im[l].reshape(bs, groups // S5_OCT, S5_OCT * S5_STATE)], axis=-1)
        h0 = jnp.pad(jnp.swapaxes(h0, 0, 1), ((0, 0), (0, SAMPLE_ROWS - bs), (0, 0)))
        y_s, h_s = _s5_step(q['su'], h0, tb)
        z_s = _glu(y_s, wglu)
        scores = _page_scores(page_table, q['iq'], q['ikw'], cache_idx_k, l)
        bias, bias_self = _page_select(scores, q['iq'], q['ikw'], nsel_s)
        att_s = _page_attend(page_table, q['dq'], q['kv'], bias, bias_self, cache_k, cache_v, l)
        att_s = jnp.pad(att_s[:, 0], ((0, SAMPLE_ROWS - bs), (0, 0))).astype(BF16)
        merged_s = _merge(o_ret_s, z_s, att_s, q['gates'], wro, wso, wdo)
        xs, xsb = _out_ln(xs, merged_s, wo, ln2_g[l], ln2_b[l], alpha)
        xs, xsb = _ffn_ln(xs, *w2, ln3_g[l], ln3_b[l], alpha)
        s5r_s, s5i_s = _s5_state_out(jnp.swapaxes(h_s, 0, 1)[:bs, :, None, :], groups)
        outs_s.append((q['kv'][:bs, :kvw].reshape(bs, ls, DSA_KV_HEADS, DSA_HEAD_DIM),
                       q['kv'][:bs, kvw:].reshape(bs, ls, DSA_KV_HEADS, DSA_HEAD_DIM),
                       q['ikw'][:bs, :IDX_DIM].reshape(bs, ls, IDX_DIM), ret_s, s5r_s, s5i_s))

    k_p, v_p, ik_p, ret_p, s5r_p, s5i_p = [jnp.stack(a) for a in zip(*outs_p)]
    k_s, v_s, ik_s, ret_s, s5r_s, s5i_s = [jnp.stack(a) for a in zip(*outs_s)]
    return (xp.reshape(bp, lp, d), xs[:bs].reshape(bs, ls, d), k_p, v_p, ik_p, k_s, v_s, ik_s,
            ret_p, ret_s, s5r_p, s5i_p, s5r_s, s5i_s)
```

```python
import functools
import math

import numpy as np
import jax
import jax.numpy as jnp
from jax import lax
from jax.experimental import pallas as pl
from jax.experimental.pallas import tpu as pltpu

F32 = jnp.float32
BF16 = jnp.bfloat16
I32 = jnp.int32

PAGE_SIZE = 128
RET_HEADS = 4
RET_DK = 256
RET_DV = 512
RET_CHUNK = 128
ROPE_BASE = 10000.0
S5_GROUP = 16
S5_STATE = 64
DSA_HEADS = 8
DSA_KV_HEADS = 2
DSA_HEAD_DIM = 128
IDX_HEADS = 8
IDX_DIM = 64
DSA_TOPK = 256
LN_EPS = 1e-5
GN_EPS = 1e-5
N_BRANCHES = 3

LANES = 128
SUBLANES = 8
VMEM_LIMIT_CAP = 56 * 1024 * 1024
S5_CHUNK = 16
S5_OCT = LANES // S5_GROUP
SAMPLE_ROWS = 16
MASK_NEG = -1e30
INT_MIN = -2 ** 31


def _cparams(semantics, *block_bytes):
    est = 2 * sum(block_bytes) + (8 << 20)
    return pltpu.CompilerParams(dimension_semantics=semantics,
                                vmem_limit_bytes=int(min(max(est, 32 << 20), VMEM_LIMIT_CAP)))


def _nbytes(shape, dtype):
    return int(np.prod(shape)) * jnp.dtype(dtype).itemsize


def _dot(a, b):
    return jnp.dot(a, b, preferred_element_type=F32)


def _dot_nt(a, b):
    return lax.dot_general(a, b, (((1,), (1,)), ((), ())), preferred_element_type=F32)


def _layernorm(y, g, b):
    mu = jnp.mean(y, axis=-1, keepdims=True)
    yc = y - mu
    var = jnp.mean(yc * yc, axis=-1, keepdims=True)
    return yc * lax.rsqrt(var + LN_EPS) * g + b


def _col_of_row(r):
    n = r.shape[1]
    eye = lax.broadcasted_iota(I32, (n, n), 0) == lax.broadcasted_iota(I32, (n, n), 1)
    return jnp.sum(jnp.where(eye, jnp.broadcast_to(r, (n, n)), 0.0), axis=1, keepdims=True)


def _mm_kernel(x_ref, w_ref, *o_refs):
    acc = _dot(x_ref[...], w_ref[...])
    for o_ref in o_refs:
        o_ref[...] = acc.astype(o_ref.dtype)


def _mm_oct_kernel(x_ref, w_ref, o_ref):
    acc = _dot(x_ref[...], w_ref[...])
    for i in range(o_ref.shape[0]):
        o_ref[i] = acc[:, i * LANES:(i + 1) * LANES]


def _mm(x, w, out_dtypes, oct_layout=False, name="mm"):
    m, k = x.shape
    n = w.shape[1]
    tm = min(m, 1024)
    tn = n if n <= 1024 else 512
    assert m % tm == 0 and n % tn == 0
    w_resident = (n // tn) * m * k + k * n <= (m // tm) * k * n + m * k
    if w_resident:
        grid = (n // tn, m // tm)
        xi, wi, oi = (lambda j, i: (i, 0)), (lambda j, i: (0, j)), (lambda j, i: (i, j))
        ooct = lambda j, i: (j, i, 0)
    else:
        grid = (m // tm, n // tn)
        xi, wi, oi = (lambda i, j: (i, 0)), (lambda i, j: (0, j)), (lambda i, j: (i, j))
        ooct = lambda i, j: (j, i, 0)
    in_specs = [pl.BlockSpec((tm, k), xi), pl.BlockSpec((k, tn), wi)]
    blk = [_nbytes((tm, k), BF16), _nbytes((k, tn), BF16)]
    if oct_layout:
        out_shape = jax.ShapeDtypeStruct((n // LANES, m, LANES), F32)
        out_specs = pl.BlockSpec((tn // LANES, tm, LANES), ooct)
        body = _mm_oct_kernel
        blk.append(_nbytes((tm, tn), F32))
    else:
        out_shape = tuple(jax.ShapeDtypeStruct((m, n), d) for d in out_dtypes)
        out_specs = tuple(pl.BlockSpec((tm, tn), oi) for _ in out_dtypes)
        body = _mm_kernel
        blk += [_nbytes((tm, tn), d) for d in out_dtypes]
    out = pl.pallas_call(body, grid=grid, in_specs=in_specs, out_specs=out_specs, out_shape=out_shape,
                         compiler_params=_cparams(("arbitrary", "arbitrary"), *blk), name=name)(x, w)
    return out if oct_layout else (out[0] if len(out_dtypes) == 1 else out)


def _ffn_kernel(x_ref, wg_ref, wu_ref, wd_ref, g_ref, b_ref, o_ref, ob_ref, xb_scr, acc_scr, *, alpha):
    f = pl.program_id(1)

    @pl.when(f == 0)
    def _():
        xb_scr[...] = x_ref[...].astype(BF16)
        acc_scr[...] = jnp.zeros_like(acc_scr)

    xb = xb_scr[...]
    hg = _dot(xb, wg_ref[...])
    hu = _dot(xb, wu_ref[...])
    h = hg * jax.nn.sigmoid(hg) * hu
    acc_scr[...] += _dot(h.astype(BF16), wd_ref[...])

    @pl.when(f == pl.num_programs(1) - 1)
    def _():
        y = _layernorm(alpha * x_ref[...] + 0.5 * acc_scr[...], g_ref[...], b_ref[...])
        o_ref[...] = y
        ob_ref[...] = y.astype(BF16)


def _ffn_ln(x, wg, wu, wd, g, b, alpha):
    m, d = x.shape
    dff = wg.shape[1]
    tm = min(m, 512)
    tf = 512
    assert m % tm == 0 and dff % tf == 0
    row = lambda i, f: (i, 0)
    blk = [_nbytes((tm, d), F32), 3 * _nbytes((d, tf), BF16), _nbytes((tm, d), F32), _nbytes((tm, d), BF16),
           _nbytes((tm, d), F32)]
    return pl.pallas_call(
        functools.partial(_ffn_kernel, alpha=alpha),
        grid=(m // tm, dff // tf),
        in_specs=[pl.BlockSpec((tm, d), row),
                  pl.BlockSpec((d, tf), lambda i, f: (0, f)),
                  pl.BlockSpec((d, tf), lambda i, f: (0, f)),
                  pl.BlockSpec((tf, d), lambda i, f: (f, 0)),
                  pl.BlockSpec((1, d), lambda i, f: (0, 0)),
                  pl.BlockSpec((1, d), lambda i, f: (0, 0))],
        out_specs=(pl.BlockSpec((tm, d), row), pl.BlockSpec((tm, d), row)),
        out_shape=(jax.ShapeDtypeStruct((m, d), F32), jax.ShapeDtypeStruct((m, d), BF16)),
        scratch_shapes=[pltpu.VMEM((tm, d), BF16), pltpu.VMEM((tm, d), F32)],
        compiler_params=_cparams(("arbitrary", "arbitrary"), *blk),
        name="ffn_ln",
    )(x, wg, wu, wd, g.reshape(1, d), b.reshape(1, d))


def _glu_kernel(y_ref, w_ref, o_ref):
    y = jnp.concatenate([y_ref[i] for i in range(y_ref.shape[0])], axis=-1)
    o_ref[...] = (y * jax.nn.sigmoid(_dot(y.astype(BF16), w_ref[...]))).astype(o_ref.dtype)


def _glu(y_oct, w):
    no, m, _ = y_oct.shape
    n = w.shape[1]
    tm = min(m, 512)
    blk = [_nbytes((no, tm, LANES), F32), _nbytes(w.shape, BF16), _nbytes((tm, n), BF16)]
    return pl.pallas_call(
        _glu_kernel, grid=(m // tm,),
        in_specs=[pl.BlockSpec((no, tm, LANES), lambda i: (0, i, 0)), pl.BlockSpec(w.shape, lambda i: (0, 0))],
        out_specs=pl.BlockSpec((tm, n), lambda i: (i, 0)),
        out_shape=jax.ShapeDtypeStruct((m, n), BF16),
        compiler_params=_cparams(("arbitrary",), *blk),
        name="s5_glu",
    )(y_oct, w)


def _merge_kernel(o_ref, z_ref, a_ref, g0_ref, g1_ref, g2_ref, wr_ref, ws_ref, wd_ref, m_ref):
    ret = _dot(o_ref[...], wr_ref[...])
    s5 = _dot(z_ref[...], ws_ref[...])
    dsa = _dot(a_ref[...], wd_ref[...])
    merged = (jax.nn.sigmoid(g0_ref[...]) * ret + jax.nn.sigmoid(g1_ref[...]) * s5
              + jax.nn.sigmoid(g2_ref[...]) * dsa)
    m_ref[...] = merged.astype(m_ref.dtype)


def _merge(o, z, a, gates, wr, ws, wd):
    m = o.shape[0]
    d = wr.shape[1]
    tm = min(m, 512)
    tn = min(d, 512)
    nb = d // tn
    blk = [_nbytes((tm, o.shape[1]), BF16), 2 * _nbytes((tm, z.shape[1]), BF16), 3 * _nbytes((tm, tn), F32),
           _nbytes((o.shape[1], tn), BF16), 2 * _nbytes((z.shape[1], tn), BF16), _nbytes((tm, tn), BF16)]
    gate_spec = lambda br: pl.BlockSpec((tm, tn), lambda i, j: (i, br * nb + j))
    return pl.pallas_call(
        _merge_kernel, grid=(m // tm, nb),
        in_specs=[pl.BlockSpec((tm, o.shape[1]), lambda i, j: (i, 0)),
                  pl.BlockSpec((tm, z.shape[1]), lambda i, j: (i, 0)),
                  pl.BlockSpec((tm, a.shape[1]), lambda i, j: (i, 0)),
                  gate_spec(0), gate_spec(1), gate_spec(2),
                  pl.BlockSpec((wr.shape[0], tn), lambda i, j: (0, j)),
                  pl.BlockSpec((ws.shape[0], tn), lambda i, j: (0, j)),
                  pl.BlockSpec((wd.shape[0], tn), lambda i, j: (0, j))],
        out_specs=pl.BlockSpec((tm, tn), lambda i, j: (i, j)),
        out_shape=jax.ShapeDtypeStruct((m, d), BF16),
        compiler_params=_cparams(("arbitrary", "arbitrary"), *blk),
        name="branch_merge",
    )(o, z, a, gates, gates, gates, wr, ws, wd)


def _out_ln_kernel(x_ref, m_ref, w_ref, g_ref, b_ref, o_ref, ob_ref, *, alpha):
    y = _layernorm(alpha * x_ref[...] + _dot(m_ref[...], w_ref[...]), g_ref[...], b_ref[...])
    o_ref[...] = y
    ob_ref[...] = y.astype(BF16)


def _out_ln(x, merged, w, g, b, alpha):
    m, d = x.shape
    tm = min(m, 512)
    row = lambda i: (i, 0)
    blk = [2 * _nbytes((tm, d), F32), 2 * _nbytes((tm, d), BF16), _nbytes((d, d), BF16)]
    return pl.pallas_call(
        functools.partial(_out_ln_kernel, alpha=alpha), grid=(m // tm,),
        in_specs=[pl.BlockSpec((tm, d), row), pl.BlockSpec((tm, d), row), pl.BlockSpec((d, d), lambda i: (0, 0)),
                  pl.BlockSpec((1, d), lambda i: (0, 0)), pl.BlockSpec((1, d), lambda i: (0, 0))],
        out_specs=(pl.BlockSpec((tm, d), row), pl.BlockSpec((tm, d), row)),
        out_shape=(jax.ShapeDtypeStruct((m, d), F32), jax.ShapeDtypeStruct((m, d), BF16)),
        compiler_params=_cparams(("arbitrary",), *blk),
        name="out_ln",
    )(x, merged, w, g.reshape(1, d), b.reshape(1, d))


def _rope(x, cos, sin):
    half = x.shape[-1] // 2
    x1, x2 = x[:, :half], x[:, half:]
    return jnp.concatenate([x1 * cos - x2 * sin, x1 * sin + x2 * cos], axis=-1)


def _group_norm_gate(o, gate):
    mu = jnp.mean(o, axis=-1, keepdims=True)
    oc = o - mu
    var = jnp.mean(oc * oc, axis=-1, keepdims=True)
    return gate * jax.nn.sigmoid(gate) * (oc * lax.rsqrt(var + GN_EPS))


def _ret_kernel(qk_ref, v_ref, g_ref, cos_ref, sin_ref, o_ref, st_ref):
    @pl.when(pl.program_id(1) == 0)
    def _():
        st_ref[...] = jnp.zeros_like(st_ref)

    c = qk_ref.shape[0]
    cos, sin = cos_ref[...], sin_ref[...]
    ri = lax.broadcasted_iota(I32, (c, c), 0).astype(F32)
    ci = lax.broadcasted_iota(I32, (c, c), 1).astype(F32)
    diff = ri - ci
    ti = lax.broadcasted_iota(I32, (c, 1), 0).astype(F32)
    for h in range(RET_HEADS):
        lg = math.log(1.0 - 2.0 ** (-5.0 - h))
        dmask = jnp.where(diff >= 0, jnp.exp(jnp.maximum(diff, 0.0) * lg), 0.0)
        q_dec = jnp.exp((ti + 1.0) * lg)
        k_dec = jnp.exp((c - 1.0 - ti) * lg)
        c_dec = math.exp(c * lg)
        q = _rope(qk_ref[:, h * RET_DK:(h + 1) * RET_DK], cos, sin)
        k = _rope(qk_ref[:, (RET_HEADS + h) * RET_DK:(RET_HEADS + h + 1) * RET_DK], cos, sin) * (RET_DK ** -0.5)
        v = v_ref[:, h * RET_DV:(h + 1) * RET_DV]
        s = st_ref[0, h]
        qb = q.astype(BF16)
        sc = _dot_nt(qb, k.astype(BF16)) * dmask
        o = _dot(sc.astype(BF16), v) + _dot(qb, s.astype(BF16)) * q_dec
        kd_t = jnp.transpose(k * k_dec).astype(BF16)
        st_ref[0, h] = c_dec * s + _dot(kd_t, v)
        o_ref[:, h * RET_DV:(h + 1) * RET_DV] = _group_norm_gate(
            o, g_ref[:, h * RET_DV:(h + 1) * RET_DV]).astype(o_ref.dtype)


def _retention(qk, v, gate, cos, sin, bn):
    m = qk.shape[0]
    length = m // bn
    c = RET_CHUNK
    nc = length // c
    hv = RET_HEADS * RET_DV
    row = lambda b, j: (b * nc + j, 0)
    blk = [_nbytes((c, qk.shape[1]), F32), _nbytes((c, hv), BF16), _nbytes((c, hv), F32), _nbytes((c, hv), BF16),
           _nbytes((RET_HEADS, RET_DK, RET_DV), F32)]
    return pl.pallas_call(
        _ret_kernel, grid=(bn, nc),
        in_specs=[pl.BlockSpec((c, qk.shape[1]), row), pl.BlockSpec((c, hv), row), pl.BlockSpec((c, hv), row),
                  pl.BlockSpec((c, RET_DK // 2), lambda b, j: (j, 0)),
                  pl.BlockSpec((c, RET_DK // 2), lambda b, j: (j, 0))],
        out_specs=(pl.BlockSpec((c, hv), row),
                   pl.BlockSpec((1, RET_HEADS, RET_DK, RET_DV), lambda b, j: (b, 0, 0, 0))),
        out_shape=(jax.ShapeDtypeStruct((m, hv), BF16),
                   jax.ShapeDtypeStruct((bn, RET_HEADS, RET_DK, RET_DV), F32)),
        compiler_params=_cparams(("arbitrary", "arbitrary"), *blk),
        name="retention",
    )(qk, v, gate, cos, sin)


def _ret_step_kernel(qk_ref, v_ref, g_ref, cos_ref, sin_ref, s0_ref, o_ref, st_ref):
    row = pl.ds(pl.program_id(0), 1)
    cos, sin = cos_ref[...], sin_ref[...]
    for h in range(RET_HEADS):
        decay = 1.0 - 2.0 ** (-5.0 - h)
        q = _rope(qk_ref[row, h * RET_DK:(h + 1) * RET_DK], cos, sin)
        k = _rope(qk_ref[row, (RET_HEADS + h) * RET_DK:(RET_HEADS + h + 1) * RET_DK], cos, sin) * (RET_DK ** -0.5)
        v = v_ref[row, h * RET_DV:(h + 1) * RET_DV]
        s0 = s0_ref[0, h]
        st_ref[0, h] = decay * s0 + _col_of_row(k) * v
        o = (jnp.sum(q * k, axis=1, keepdims=True) * v
             + jnp.sum(_col_of_row(q) * s0, axis=0, keepdims=True) * decay)
        o_ref[0, :, h * RET_DV:(h + 1) * RET_DV] = _group_norm_gate(o, g_ref[row, h * RET_DV:(h + 1) * RET_DV])


def _retention_step(qk, v, gate, cos, sin, s0):
    bn = s0.shape[0]
    const = lambda b: (0, 0)
    state = pl.BlockSpec((1,) + s0.shape[1:], lambda b: (b, 0, 0, 0))
    return pl.pallas_call(
        _ret_step_kernel, grid=(bn,),
        in_specs=[pl.BlockSpec(qk.shape, const), pl.BlockSpec(v.shape, const), pl.BlockSpec(gate.shape, const),
                  pl.BlockSpec(cos.shape, const), pl.BlockSpec(sin.shape, const), state],
        out_specs=(pl.BlockSpec((1, 1, v.shape[1]), lambda b: (b, 0, 0)), state),
        out_shape=(jax.ShapeDtypeStruct((bn, 1, v.shape[1]), F32), jax.ShapeDtypeStruct(s0.shape, F32)),
        compiler_params=_cparams(("arbitrary",), 2 * _nbytes(s0.shape[1:], F32)),
        name="retention_step",
    )(qk, v, gate, cos, sin, s0)


def _s5_tables(a_re, a_im, log_dt, b_re, b_im, c_re, c_im, d):
    hp = lax.Precision.HIGHEST
    g, p = a_re.shape
    nc = b_re.shape[-1]
    no = g // S5_OCT
    t = S5_CHUNK
    dt = jnp.exp(log_dt)[:, None]
    mag = jnp.exp(a_re * dt)
    ab_re = mag * jnp.cos(a_im * dt)
    ab_im = mag * jnp.sin(a_im * dt)
    den = a_re * a_re + a_im * a_im
    x_re = ab_re - 1.0
    f_re = (x_re * a_re + ab_im * a_im) / den
    f_im = (ab_im * a_re - x_re * a_im) / den
    bb_re = f_re[..., None] * b_re - f_im[..., None] * b_im
    bb_im = f_re[..., None] * b_im + f_im[..., None] * b_re

    def powers(n):
        n = n.astype(F32)[:, None, None]
        pmag = jnp.exp(n * (a_re * dt)[None])
        return pmag * jnp.cos(n * (a_im * dt)[None]), pmag * jnp.sin(n * (a_im * dt)[None])

    pw_re, pw_im = powers(jnp.arange(t + 1))
    rev_re, rev_im = powers(t - 1 - jnp.arange(t))
    bt_re, bt_im = jnp.swapaxes(bb_re, 1, 2), jnp.swapaxes(bb_im, 1, 2)

    def packed(v_re, v_im):
        out = []
        for v in (v_re, v_re, v_im, v_im):
            v = v.reshape(v.shape[0], no, S5_OCT, nc, p)
            out.append(jnp.swapaxes(v, 0, 1).reshape(no, v.shape[0] * LANES, p))
        return jnp.concatenate(out, axis=-1)

    def b_packed(power_re, power_im):
        pr, pi = power_re[:, :, None, :], power_im[:, :, None, :]
        return packed(pr * bt_re[None] - pi * bt_im[None], pr * bt_im[None] + pi * bt_re[None])

    def c_packed(power_re, power_im):
        pr, pi = power_re[:, :, None, :], power_im[:, :, None, :]
        return packed(c_re[None] * pr - c_im[None] * pi, -(c_re[None] * pi + c_im[None] * pr))

    cp_re = c_re[None] * pw_re[:t, :, None, :] - c_im[None] * pw_im[:t, :, None, :]
    cp_im = c_re[None] * pw_im[:t, :, None, :] + c_im[None] * pw_re[:t, :, None, :]
    kern = (jnp.einsum('ngdp,gpc->gncd', cp_re, bb_re, precision=hp)
            - jnp.einsum('ngdp,gpc->gncd', cp_im, bb_im, precision=hp))
    kern = jnp.swapaxes(kern.reshape(no, S5_OCT, t, nc, nc), 1, 2).reshape(no, t, LANES, nc)
    same_group = (jnp.arange(LANES)[:, None] // nc == jnp.arange(LANES)[None, :] // nc)
    lag_blk = jnp.tile(kern, (1, 1, 1, S5_OCT)) * same_group.astype(F32)

    def state_row(v_re, v_im):
        return jnp.concatenate([v_re.reshape(no, 1, S5_OCT * p), v_im.reshape(no, 1, S5_OCT * p)], axis=-1)

    d_row = d.reshape(no, 1, LANES)
    return dict(
        lag_blk=lag_blk,
        b_end=b_packed(rev_re, rev_im),
        c_in=c_packed(pw_re[1:], pw_im[1:]),
        a_row=state_row(pw_re[t], pw_im[t]),
        d_row=jnp.tile(d_row, (1, 1, t)),
        b_one=b_packed(pw_re[:1], pw_im[:1]),
        c_one=c_packed(pw_re[:1], pw_im[:1]),
        a_one=state_row(ab_re, ab_im),
        d_one=d_row,
    )


def _expand_groups(packed_ref, rows):
    n = rows.stop - rows.start
    g_row = (lax.broadcasted_iota(I32, (n, LANES), 0) // S5_GROUP) % S5_OCT
    g_half = lax.broadcasted_iota(I32, (n, LANES), 1) // S5_STATE
    tiles = []
    for part in range(2):
        src = packed_ref[0, rows, part * LANES:(part + 1) * LANES]
        for pair in range(S5_OCT // 2):
            tiles.append(jnp.where(g_row == 2 * pair + g_half, src, 0.0))
    return jnp.concatenate(tiles, axis=-1)


def _s5_kernel(u_ref, lag_ref, bend_ref, cin_ref, a_ref, d_ref, y_ref, h_ref,
               toep_scr, bfull_scr, cfull_scr, e_scr, s_scr):
    rows, t = e_scr.shape[0], S5_CHUNK

    @pl.when(pl.program_id(1) == 0)
    def _():
        toep_scr[...] = jnp.zeros_like(toep_scr)
        for lag in range(t):
            blk = lag_ref[0, lag].astype(BF16)
            for s in range(t - lag):
                toep_scr[s * LANES:(s + 1) * LANES, (s + lag) * LANES:(s + lag + 1) * LANES] = blk
        for s in range(t):
            sl = slice(s * LANES, (s + 1) * LANES)
            bfull_scr[sl, :] = _expand_groups(bend_ref, sl).astype(BF16)
            cfull_scr[sl, :] = _expand_groups(cin_ref, sl).astype(BF16)

    u = jnp.concatenate([u_ref[0, pl.ds(i, rows, stride=t), :] for i in range(t)], axis=-1)
    ub = u.astype(BF16)
    e_scr[...] = _dot(ub, bfull_scr[...])
    half = a_ref.shape[-1] // 2
    a_re, a_im = a_ref[0, :, :half], a_ref[0, :, half:]

    def step(k, carry):
        s_re, s_im = carry
        s_scr[pl.ds(k, 1), :] = jnp.concatenate([s_re, s_im], axis=-1)
        e = e_scr[pl.ds(k, 1), :]
        return (a_re * s_re - a_im * s_im + e[:, :half], a_re * s_im + a_im * s_re + e[:, half:])

    zero = jnp.zeros((1, half), F32)
    s_re, s_im = lax.fori_loop(0, rows, step, (zero, zero))
    h_ref[0, 0] = jnp.concatenate([s_re, s_im], axis=-1)
    y = jax.nn.gelu(_dot(ub, toep_scr[...]) + _dot_nt(s_scr[...].astype(BF16), cfull_scr[...]) + d_ref[0] * u)
    for i in range(t):
        y_ref[0, pl.ds(i, rows, stride=t), :] = y[:, i * LANES:(i + 1) * LANES]


def _s5(u_oct, tb, bn):
    no, m, _ = u_oct.shape
    t = S5_CHUNK
    steps = m // bn
    rows = steps // t
    w = t * LANES
    ns = tb['a_row'].shape[-1]
    pk = tb['b_end'].shape[-1]
    oct_blk = lambda shape: pl.BlockSpec((1,) + shape, lambda o, b: (o,) + (0,) * len(shape))
    blk = [2 * _nbytes((rows, w), F32), _nbytes((t, LANES, LANES), F32), 2 * _nbytes((w, pk), F32),
           (_nbytes((w, w), BF16) + 2 * _nbytes((w, ns), BF16) + 2 * _nbytes((rows, ns), F32)) // 2]
    return pl.pallas_call(
        _s5_kernel, grid=(no, bn),
        in_specs=[pl.BlockSpec((1, steps, LANES), lambda o, b: (o, b, 0)),
                  oct_blk((t, LANES, LANES)), oct_blk((w, pk)), oct_blk((w, pk)), oct_blk((1, ns)),
                  oct_blk((1, w))],
        out_specs=(pl.BlockSpec((1, steps, LANES), lambda o, b: (o, b, 0)),
                   pl.BlockSpec((1, 1, 1, ns), lambda o, b: (b, o, 0, 0))),
        out_shape=(jax.ShapeDtypeStruct(u_oct.shape, F32), jax.ShapeDtypeStruct((bn, no, 1, ns), F32)),
        scratch_shapes=[pltpu.VMEM((w, w), BF16), pltpu.VMEM((w, ns), BF16), pltpu.VMEM((w, ns), BF16),
                        pltpu.VMEM((rows, ns), F32), pltpu.VMEM((rows, ns), F32)],
        compiler_params=_cparams(("arbitrary", "arbitrary"), *blk),
        name="s5_scan",
    )(u_oct, tb['lag_blk'], tb['b_end'], tb['c_in'], tb['a_row'], tb['d_row'])


def _s5_step_kernel(u_ref, h0_ref, b_ref, c_ref, a_ref, d_ref, y_ref, h_ref):
    u = u_ref[0]
    half = a_ref.shape[-1] // 2
    a_re, a_im = a_ref[0, :, :half], a_ref[0, :, half:]
    h0 = h0_ref[0]
    h0_re, h0_im = h0[:, :half], h0[:, half:]
    whole = slice(0, LANES)
    bu = _dot(u.astype(BF16), _expand_groups(b_ref, whole).astype(BF16))
    h_re = a_re * h0_re - a_im * h0_im + bu[:, :half]
    h_im = a_re * h0_im + a_im * h0_re + bu[:, half:]
    h = jnp.concatenate([h_re, h_im], axis=-1)
    h_ref[0] = h
    y_ref[0] = jax.nn.gelu(_dot_nt(h.astype(BF16), _expand_groups(c_ref, whole).astype(BF16)) + d_ref[0] * u)


def _s5_step(u_oct, h0, tb):
    no, rows, _ = u_oct.shape
    ns = h0.shape[-1]
    pk = tb['b_one'].shape[-1]
    o3 = lambda shape: pl.BlockSpec((1,) + shape, lambda o: (o, 0, 0))
    return pl.pallas_call(
        _s5_step_kernel, grid=(no,),
        in_specs=[o3((rows, LANES)), o3((rows, ns)), o3((LANES, pk)), o3((LANES, pk)), o3((1, ns)), o3((1, LANES))],
        out_specs=(o3((rows, LANES)), o3((rows, ns))),
        out_shape=(jax.ShapeDtypeStruct(u_oct.shape, F32), jax.ShapeDtypeStruct(h0.shape, F32)),
        compiler_params=_cparams(("arbitrary",), _nbytes((LANES, ns), BF16) * 2),
        name="s5_step",
    )(u_oct, h0, tb['b_one'], tb['c_one'], tb['a_one'], tb['d_one'])


def _sort_key(x):
    bits = pltpu.bitcast(x, I32)
    return jnp.where(bits >= 0, bits, bits ^ jnp.int32(0x7FFFFFFF))


def _kth_largest_key(count_ge, nsel, shape):
    res = jnp.where(count_ge(jnp.zeros(shape, I32)) >= nsel, jnp.int32(0), jnp.int32(INT_MIN))

    def bit_step(i, res):
        cand = res | jnp.left_shift(jnp.int32(1), 30 - i)
        return jnp.where(count_ge(cand) >= nsel, cand, res)

    return lax.fori_loop(0, 31, bit_step, res)


def _dsa_kernel(dq_ref, iq_ref, ikw_ref, ikb_ref, kv_ref, o_ref, key_scr, bias_scr, m_scr, l_scr, acc_scr,
                *, nsel):
    tq = dq_ref.shape[0]
    jq = pl.program_id(1)
    nkb = jq + 1
    krow = lax.broadcasted_iota(I32, (tq, tq), 0)
    qcol = lax.broadcasted_iota(I32, (tq, tq), 1)
    qpos = jq * tq + qcol
    hd = DSA_HEAD_DIM
    gsz = DSA_HEADS // DSA_KV_HEADS

    def over_keys(x, op):
        part = op(x.reshape(x.shape[0] // SUBLANES, SUBLANES, x.shape[1]), axis=0)
        return op(part, axis=0, keepdims=True)

    iq_st = jnp.concatenate([iq_ref[:, h * LANES:(h + 1) * LANES] for h in range(IDX_HEADS)], axis=0)
    w_t = jnp.transpose(ikw_ref[...])

    def score_block(kb, carry):
        ik = ikb_ref[pl.ds(pl.multiple_of(kb * tq, tq), tq), :]
        sh = jnp.maximum(_dot_nt(ik, iq_st), 0.0)
        acc = jnp.zeros((tq, tq), F32)
        for h in range(IDX_HEADS):
            acc = acc + sh[:, h * tq:(h + 1) * tq] * w_t[IDX_DIM + h:IDX_DIM + h + 1, :]
        acc = jnp.where(kb * tq + krow <= qpos, acc, -jnp.inf)
        key_scr[kb] = _sort_key(acc)
        return carry

    lax.fori_loop(0, nkb, score_block, 0)

    def count(pred):
        def body(kb, c):
            hit = jnp.where(pred(key_scr[kb]), 1.0, 0.0)
            return c + jnp.sum(hit.reshape(tq // SUBLANES, SUBLANES, tq), axis=0)
        part = lax.fori_loop(0, nkb, body, jnp.zeros((SUBLANES, tq), F32))
        return jnp.sum(part, axis=0, keepdims=True)

    kth = _kth_largest_key(lambda cand: count(lambda key: key >= cand), float(nsel), (1, tq))
    need = float(nsel) - count(lambda key: key > kth)
    tri = jnp.where(krow >= qcol, 1.0, 0.0).astype(BF16)

    def select_block(kb, taken):
        key = key_scr[kb]
        tie = jnp.where(key == kth, 1.0, 0.0)
        rank = taken + _dot(tri, tie.astype(BF16))
        sel = ((key > kth) | ((tie > 0.0) & (rank <= need))) & (kb * tq + krow <= qpos)
        bias_scr[kb] = jnp.where(sel, 0.0, MASK_NEG)
        return taken + over_keys(tie, jnp.sum)

    lax.fori_loop(0, nkb, select_block, jnp.zeros((1, tq), F32))

    scale2 = (hd ** -0.5) * math.log2(math.e)
    for g in range(DSA_KV_HEADS):
        qg = jnp.concatenate([dq_ref[:, (g * gsz + i) * hd:(g * gsz + i + 1) * hd] for i in range(gsz)], axis=0)
        m_scr[...] = jnp.full_like(m_scr, -jnp.inf)
        l_scr[...] = jnp.zeros_like(l_scr)
        acc_scr[...] = jnp.zeros_like(acc_scr)

        def attend_block(kb, carry):
            ks = pl.ds(pl.multiple_of(kb * tq, tq), tq)
            kblk = kv_ref[ks, g * hd:(g + 1) * hd]
            vblk = kv_ref[ks, (DSA_KV_HEADS + g) * hd:(DSA_KV_HEADS + g + 1) * hd]
            bias = bias_scr[kb]
            lg = _dot_nt(kblk, qg) * scale2 + jnp.concatenate([bias] * gsz, axis=1)
            m_old = m_scr[...]
            m_new = jnp.maximum(m_old, over_keys(lg, jnp.max))
            alpha = jnp.exp2(m_old - m_new)
            p = jnp.exp2(lg - m_new)
            l_scr[...] = alpha * l_scr[...] + over_keys(p, jnp.sum)
            v_t = jnp.transpose(vblk.astype(F32)).astype(BF16)
            acc_scr[...] = alpha * acc_scr[...] + _dot(v_t, p.astype(BF16))
            m_scr[...] = m_new
            return carry

        lax.fori_loop(0, nkb, attend_block, 0)
        out_t = acc_scr[...] / l_scr[...]
        for i in range(gsz):
            o_ref[:, (g * gsz + i) * hd:(g * gsz + i + 1) * hd] = jnp.transpose(
                out_t[:, i * tq:(i + 1) * tq]).astype(o_ref.dtype)


def _dsa_prompt(dq, iq, ikw, ikb, kvb, bn):
    m = dq.shape[0]
    length = m // bn
    tq = 256
    nq = length // tq
    nsel = min(DSA_TOPK, length // 4)
    qrow = lambda b, j: (b * nq + j, 0)
    full = lambda b, j: (b, 0)
    gsz = DSA_HEADS // DSA_KV_HEADS
    blk = [_nbytes((tq, dq.shape[1]), BF16) * 3, _nbytes((length, LANES), BF16), _nbytes((length, kvb.shape[1]), BF16),
           _nbytes((nq, tq, tq), F32)]
    return pl.pallas_call(
        functools.partial(_dsa_kernel, nsel=nsel), grid=(bn, nq),
        in_specs=[pl.BlockSpec((tq, dq.shape[1]), qrow), pl.BlockSpec((tq, iq.shape[1]), qrow),
                  pl.BlockSpec((tq, LANES), qrow), pl.BlockSpec((length, LANES), full),
                  pl.BlockSpec((length, kvb.shape[1]), full)],
        out_specs=pl.BlockSpec((tq, dq.shape[1]), qrow),
        out_shape=jax.ShapeDtypeStruct(dq.shape, BF16),
        scratch_shapes=[pltpu.VMEM((nq, tq, tq), I32), pltpu.VMEM((nq, tq, tq), F32),
                        pltpu.VMEM((1, gsz * tq), F32), pltpu.VMEM((1, gsz * tq), F32),
                        pltpu.VMEM((DSA_HEAD_DIM, gsz * tq), F32)],
        compiler_params=_cparams(("arbitrary", "arbitrary"), *blk),
        name="dsa_prompt",
    )(dq, iq, ikw, ikb, kvb)


MAX_PAGES_PER_STEP = 16


def _idx_heads(iq_row):
    return jnp.concatenate([iq_row[:, h * LANES:h * LANES + IDX_DIM] for h in range(IDX_HEADS)], axis=0)


def _page_score_kernel(pt_ref, iq_ref, ikw_ref, *rest):
    page_refs, o_ref = rest[:-1], rest[-1]
    b = pl.program_id(0)
    iq_h = jnp.concatenate([_idx_heads(iq_ref[pl.ds(b, 1), :]),
                            jnp.zeros((SAMPLE_ROWS - IDX_HEADS, IDX_DIM), F32)], axis=0).astype(BF16)
    w_col = _col_of_row(ikw_ref[pl.ds(b, 1), IDX_DIM:IDX_DIM + SAMPLE_ROWS])
    for i, page in enumerate(page_refs):
        sh = jnp.maximum(_dot_nt(iq_h, page[...].astype(BF16)), 0.0)
        o_ref[0, i:i + 1, :] = jnp.sum(sh * w_col, axis=0, keepdims=True)


def _page_scores(page_table, iq, ikw, cache_idx, layer):
    bn, npages = page_table.shape
    pg = min(MAX_PAGES_PER_STEP, npages)
    const = lambda b, s, pt: (0, 0)

    def page_spec(i):
        return pl.BlockSpec((None, None, PAGE_SIZE, IDX_DIM), lambda b, s, pt: (layer, pt[b, s * pg + i], 0, 0))

    return pl.pallas_call(
        _page_score_kernel,
        grid_spec=pltpu.PrefetchScalarGridSpec(
            num_scalar_prefetch=1, grid=(bn, npages // pg),
            in_specs=[pl.BlockSpec(iq.shape, const), pl.BlockSpec(ikw.shape, const)]
            + [page_spec(i) for i in range(pg)],
            out_specs=pl.BlockSpec((1, pg, PAGE_SIZE), lambda b, s, pt: (b, s, 0))),
        out_shape=jax.ShapeDtypeStruct((bn, npages, PAGE_SIZE), F32),
        compiler_params=_cparams(("arbitrary", "arbitrary")),
        name="page_scores",
    )(page_table, iq, ikw, *([cache_idx] * pg))


def _page_select_kernel(sc_ref, iq_ref, ikw_ref, bias_ref, bias_self_ref, *, nsel):
    b = pl.program_id(0)
    npages, psz = sc_ref.shape[1:]
    iq_h = _idx_heads(iq_ref[pl.ds(b, 1), :])
    ikw = ikw_ref[pl.ds(b, 1), :]
    w_col = _col_of_row(ikw[:, IDX_DIM:IDX_DIM + IDX_HEADS])
    s_self = jnp.sum(jnp.maximum(jnp.sum(iq_h * ikw[:, :IDX_DIM], axis=1, keepdims=True), 0.0) * w_col,
                     axis=0, keepdims=True)
    key = _sort_key(sc_ref[0])
    key_self = _sort_key(s_self)

    def total(x):
        return jnp.sum(jnp.sum(x, axis=1, keepdims=True), axis=0, keepdims=True)

    def count_ge(cand):
        return total(jnp.where(key >= cand, 1.0, 0.0)) + jnp.where(key_self >= cand, 1.0, 0.0)

    kth = _kth_largest_key(count_ge, float(nsel), (1, 1))
    need = float(nsel) - (total(jnp.where(key > kth, 1.0, 0.0)) + jnp.where(key_self > kth, 1.0, 0.0))
    tie = jnp.where(key == kth, 1.0, 0.0)
    r_in = lax.broadcasted_iota(I32, (psz, psz), 0)
    c_in = lax.broadcasted_iota(I32, (psz, psz), 1)
    in_page = _dot(tie.astype(BF16), jnp.where(r_in <= c_in, 1.0, 0.0).astype(BF16))
    per_page = jnp.broadcast_to(jnp.sum(tie, axis=1, keepdims=True), (npages, psz)).astype(BF16)
    r_pg = lax.broadcasted_iota(I32, (npages, npages), 0)
    c_pg = lax.broadcasted_iota(I32, (npages, npages), 1)
    before = _dot(jnp.where(c_pg < r_pg, 1.0, 0.0).astype(BF16), per_page)
    sel = (key > kth) | ((tie > 0.0) & (before + in_page <= need))
    r_dup = lax.broadcasted_iota(I32, (psz, bias_ref.shape[2]), 0)
    c_dup = lax.broadcasted_iota(I32, (psz, bias_ref.shape[2]), 1)
    dup = jnp.where(c_dup // DSA_KV_HEADS == r_dup, 1.0, 0.0).astype(BF16)
    spread = _dot(jnp.where(sel, 1.0, 0.0).astype(BF16), dup)
    bias_ref[0] = jnp.where(spread > 0.5, 0.0, MASK_NEG)
    sel_self = (key_self > kth) | ((key_self == kth) & (total(tie) + 1.0 <= need))
    bias_self_ref[0] = jnp.broadcast_to(jnp.where(sel_self, 0.0, MASK_NEG), (1, LANES))


def _page_select(scores, iq, ikw, nsel):
    bn, npages, psz = scores.shape
    const = lambda b: (0, 0)
    wide = DSA_KV_HEADS * psz
    return pl.pallas_call(
        functools.partial(_page_select_kernel, nsel=nsel), grid=(bn,),
        in_specs=[pl.BlockSpec((1, npages, psz), lambda b: (b, 0, 0)),
                  pl.BlockSpec(iq.shape, const), pl.BlockSpec(ikw.shape, const)],
        out_specs=(pl.BlockSpec((1, npages, wide), lambda b: (b, 0, 0)),
                   pl.BlockSpec((1, 1, LANES), lambda b: (b, 0, 0))),
        out_shape=(jax.ShapeDtypeStruct((bn, npages, wide), F32), jax.ShapeDtypeStruct((bn, 1, LANES), F32)),
        compiler_params=_cparams(("arbitrary",)),
        name="page_select",
    )(scores, iq, ikw)


def _page_attend_kernel(pt_ref, dq_ref, kvs_ref, bias_ref, bself_ref, *rest, npg):
    k_refs, v_refs = rest[:npg], rest[npg:2 * npg]
    o_ref, m_scr, l_scr, acc_scr = rest[2 * npg:]
    b = pl.program_id(0)
    s = pl.program_id(1)
    hd = DSA_HEAD_DIM
    gsz = DSA_HEADS // DSA_KV_HEADS
    scale = hd ** -0.5
    prow = k_refs[0].shape[0]
    dq_row = dq_ref[pl.ds(b, 1), :]
    q = jnp.concatenate([dq_row[:, h * hd:(h + 1) * hd] for h in range(DSA_HEADS)]
                        + [jnp.zeros((SAMPLE_ROWS - DSA_HEADS, hd), F32)], axis=0)
    kv_of_head = lax.broadcasted_iota(I32, (SAMPLE_ROWS, 1), 0) // gsz

    @pl.when(s == 0)
    def _():
        m_scr[...] = jnp.full_like(m_scr, -jnp.inf)
        l_scr[...] = jnp.zeros_like(l_scr)
        acc_scr[...] = jnp.zeros_like(acc_scr)

    def update(lg, pv_of):
        m_old = m_scr[...]
        m_new = jnp.maximum(m_old, jnp.max(lg, axis=1, keepdims=True))
        alpha = jnp.exp(m_old - m_new)
        p = jnp.exp(lg - m_new)
        l_scr[...] = alpha * l_scr[...] + jnp.sum(p, axis=1, keepdims=True)
        acc_scr[...] = alpha * acc_scr[...] + pv_of(p)
        m_scr[...] = m_new

    qb = q.astype(BF16)
    lg = jnp.concatenate([_dot_nt(qb, k_refs[i][...].astype(BF16)) for i in range(npg)], axis=1) * scale
    bias = jnp.concatenate([bias_ref[0, i:i + 1, :] for i in range(npg)], axis=1)
    col = lax.broadcasted_iota(I32, lg.shape, 1)
    lg = jnp.where(jnp.bitwise_and(col, DSA_KV_HEADS - 1) == kv_of_head, lg + bias, MASK_NEG)

    def pv_pages(p):
        pb = p.astype(BF16)
        acc = jnp.zeros((SAMPLE_ROWS, hd), F32)
        for i in range(npg):
            acc = acc + _dot(pb[:, i * prow:(i + 1) * prow], v_refs[i][...].astype(BF16))
        return acc

    update(lg, pv_pages)

    @pl.when(s == pl.num_programs(1) - 1)
    def _():
        kvs = kvs_ref[pl.ds(b, 1), :]
        k_self = jnp.where(kv_of_head == 0, kvs[:, :hd], kvs[:, hd:2 * hd])
        v_self = jnp.where(kv_of_head == 0, kvs[:, 2 * hd:3 * hd], kvs[:, 3 * hd:])
        lg_self = jnp.sum(q * k_self, axis=1, keepdims=True) * scale + bself_ref[0, :, :1]
        update(lg_self, lambda p: p * v_self)
        out = acc_scr[...] / l_scr[...]
        for h in range(DSA_HEADS):
            o_ref[0, :, h * hd:(h + 1) * hd] = out[h:h + 1]


def _page_attend(page_table, dq, kvs, bias, bias_self, cache_k, cache_v, layer):
    assert DSA_KV_HEADS == 2
    bn, npages = page_table.shape
    pg = min(MAX_PAGES_PER_STEP, npages)
    prow, hd = cache_k.shape[-2:]
    const = lambda b, s, pt: (0, 0)

    def page_spec(i):
        return pl.BlockSpec((None, None, prow, hd), lambda b, s, pt: (layer, pt[b, s * pg + i], 0, 0))

    return pl.pallas_call(
        functools.partial(_page_attend_kernel, npg=pg),
        grid_spec=pltpu.PrefetchScalarGridSpec(
            num_scalar_prefetch=1, grid=(bn, npages // pg),
            in_specs=[pl.BlockSpec(dq.shape, const), pl.BlockSpec(kvs.shape, const),
                      pl.BlockSpec((1, pg, prow), lambda b, s, pt: (b, s, 0)),
                      pl.BlockSpec((1, 1, LANES), lambda b, s, pt: (b, 0, 0))]
            + [page_spec(i) for i in range(pg)] + [page_spec(i) for i in range(pg)],
            out_specs=pl.BlockSpec((1, 1, dq.shape[1]), lambda b, s, pt: (b, 0, 0)),
            scratch_shapes=[pltpu.VMEM((SAMPLE_ROWS, 1), F32), pltpu.VMEM((SAMPLE_ROWS, 1), F32),
                            pltpu.VMEM((SAMPLE_ROWS, hd), F32)]),
        out_shape=jax.ShapeDtypeStruct((bn, 1, dq.shape[1]), F32),
        compiler_params=_cparams(("arbitrary", "arbitrary"), 2 * pg * _nbytes((prow, hd), F32)),
        name="page_attend",
    )(page_table, dq, kvs, bias, bias_self, *([cache_k] * pg), *([cache_v] * pg))


def _rope_tables(pos):
    half = RET_DK // 2
    inv = 1.0 / (ROPE_BASE ** jnp.linspace(0.0, 1.0, half, dtype=F32))
    ang = pos.astype(F32)[:, None] * inv[None, :]
    return jnp.cos(ang), jnp.sin(ang)


def _in_proj_weights(w_in):
    hq = RET_HEADS * RET_DK
    hv = RET_HEADS * RET_DV
    d = w_in.shape[0]
    o = 0
    cuts = {}
    for name, width in (('qk', 2 * hq), ('v', hv), ('g', hv)):
        cuts[name] = (o, o + width)
        o += width
    s5w = (w_in.shape[1] - 2 * hq - 2 * hv - DSA_HEADS * DSA_HEAD_DIM - 2 * DSA_KV_HEADS * DSA_HEAD_DIM
           - IDX_HEADS * IDX_DIM - IDX_DIM - IDX_HEADS) // (1 + 2 * N_BRANCHES)
    for name, width in (('su', s5w), ('dq', DSA_HEADS * DSA_HEAD_DIM), ('kv', 2 * DSA_KV_HEADS * DSA_HEAD_DIM),
                        ('iq', IDX_HEADS * IDX_DIM), ('ikw', IDX_DIM + IDX_HEADS), ('gates', N_BRANCHES * d)):
        cuts[name] = (o, o + width)
        o += width
    assert o == w_in.shape[1]
    wb = {k: w_in[:, a:b].astype(BF16) for k, (a, b) in cuts.items()}
    iq = wb['iq'].reshape(d, IDX_HEADS, IDX_DIM)
    wb['iq'] = jnp.pad(iq, ((0, 0), (0, 0), (0, LANES - IDX_DIM))).reshape(d, IDX_HEADS * LANES)
    wb['ikw'] = jnp.pad(wb['ikw'], ((0, 0), (0, LANES - IDX_DIM - IDX_HEADS)))
    return wb


def _project(xb, wb, sample):
    act = F32 if sample else BF16
    p = {
        'qk': _mm(xb, wb['qk'], (F32,), name="proj_qk"),
        'v': _mm(xb, wb['v'], (act,), name="proj_v"),
        'g': _mm(xb, wb['g'], (F32,), name="proj_g"),
        'su': _mm(xb, wb['su'], None, oct_layout=True, name="proj_su"),
        'dq': _mm(xb, wb['dq'], (act,), name="proj_dq"),
        'iq': _mm(xb, wb['iq'], (act,), name="proj_iq"),
        'gates': _mm(xb, wb['gates'], (F32,), name="proj_gates"),
    }
    p['kv'], p['kvb'] = _mm(xb, wb['kv'], (F32, BF16), name="proj_kv")
    p['ikw'], p['ikb'] = _mm(xb, wb['ikw'], (F32, BF16), name="proj_ikw")
    return p


def _s5_state_out(h, groups):
    bn = h.shape[0]
    h = h.reshape(bn, groups // S5_OCT, 2, S5_OCT, S5_STATE)
    return h[:, :, 0].reshape(bn, groups, S5_STATE), h[:, :, 1].reshape(bn, groups, S5_STATE)


def kernel(x_prompt, x_sample, cache_k, cache_v, cache_idx_k, state_ret, state_s5_re, state_s5_im, page_table, ln1_g, ln1_b, ffn1_wg, ffn1_wu, ffn1_wd, w_in, s5_a_re, s5_a_im, s5_log_dt, s5_b_re, s5_b_im, s5_c_re, s5_c_im, s5_d, w_glu, w_ret_o, w_s5_o, w_dsa_o, w_out, ln2_g, ln2_b, ffn2_wg, ffn2_wu, ffn2_wd, ln3_g, ln3_b):
    bp, lp, d = x_prompt.shape
    bs, ls, _ = x_sample.shape
    depth = w_in.shape[0]
    assert ls == 1 and bs <= SAMPLE_ROWS
    npages = page_table.shape[1]
    past = npages * PAGE_SIZE
    groups = s5_a_re.shape[1]
    alpha = (2 * depth) ** 0.25
    kvw = DSA_KV_HEADS * DSA_HEAD_DIM
    nsel_s = min(DSA_TOPK, (past + ls) // 4)

    cos_p, sin_p = _rope_tables(jnp.arange(lp))
    cos_s, sin_s = _rope_tables(past + jnp.arange(ls))
    cache_k = cache_k.reshape(depth, -1, PAGE_SIZE * DSA_KV_HEADS, DSA_HEAD_DIM)
    cache_v = cache_v.reshape(depth, -1, PAGE_SIZE * DSA_KV_HEADS, DSA_HEAD_DIM)

    xp = x_prompt.reshape(bp * lp, d)
    xs = jnp.pad(x_sample.reshape(bs, d), ((0, SAMPLE_ROWS - bs), (0, 0)))
    outs_p, outs_s = [], []
    for l in range(depth):
        bf = lambda a: a[l].astype(BF16)
        wb = _in_proj_weights(w_in[l])
        tb = _s5_tables(s5_a_re[l], s5_a_im[l], s5_log_dt[l], s5_b_re[l], s5_b_im[l], s5_c_re[l], s5_c_im[l],
                        s5_d[l])
        w1 = (bf(ffn1_wg), bf(ffn1_wu), bf(ffn1_wd))
        w2 = (bf(ffn2_wg), bf(ffn2_wu), bf(ffn2_wd))
        wglu, wro, wso, wdo, wo = bf(w_glu), bf(w_ret_o), bf(w_s5_o), bf(w_dsa_o), bf(w_out)

        xp, xpb = _ffn_ln(xp, *w1, ln1_g[l], ln1_b[l], alpha)
        p = _project(xpb, wb, sample=False)
        o_ret, ret_p = _retention(p['qk'], p['v'], p['g'], cos_p, sin_p, bp)
        y_s5, h_p = _s5(p['su'], tb, bp)
        z = _glu(y_s5, wglu)
        att = _dsa_prompt(p['dq'], p['iq'], p['ikw'], p['ikb'], p['kvb'], bp)
        merged = _merge(o_ret, z, att, p['gates'], wro, wso, wdo)
        xp, xpb = _out_ln(xp, merged, wo, ln2_g[l], ln2_b[l], alpha)
        xp, xpb = _ffn_ln(xp, *w2, ln3_g[l], ln3_b[l], alpha)
        s5r_p, s5i_p = _s5_state_out(h_p, groups)
        outs_p.append((p['kv'][:, :kvw].reshape(bp, lp, DSA_KV_HEADS, DSA_HEAD_DIM),
                       p['kv'][:, kvw:].reshape(bp, lp, DSA_KV_HEADS, DSA_HEAD_DIM),
                       p['ikw'][:, :IDX_DIM].reshape(bp, lp, IDX_DIM), ret_p, s5r_p, s5i_p))

        xs, xsb = _ffn_ln(xs, *w1, ln1_g[l], ln1_b[l], alpha)
        q = _project(xsb, wb, sample=True)
        o_ret_s, ret_s = _retention_step(q['qk'], q['v'], q['g'], cos_s, sin_s, state_ret[l])
        o_ret_s = jnp.pad(o_ret_s[:, 0], ((0, SAMPLE_ROWS - bs), (0, 0))).astype(BF16)
        h0 = jnp.concatenate([state_s5_re[l].reshape(bs, groups // S5_OCT, S5_OCT * S5_STATE),
                              state_s5_im[l].reshape(bs, groups // S5_OCT, S5_OCT * S5_STATE)], axis=-1)
        h0 = jnp.pad(jnp.swapaxes(h0, 0, 1), ((0, 0), (0, SAMPLE_ROWS - bs), (0, 0)))
        y_s, h_s = _s5_step(q['su'], h0, tb)
        z_s = _glu(y_s, wglu)
        scores = _page_scores(page_table, q['iq'], q['ikw'], cache_idx_k, l)
        bias, bias_self = _page_select(scores, q['iq'], q['ikw'], nsel_s)
        att_s = _page_attend(page_table, q['dq'], q['kv'], bias, bias_self, cache_k, cache_v, l)
        att_s = jnp.pad(att_s[:, 0], ((0, SAMPLE_ROWS - bs), (0, 0))).astype(BF16)
        merged_s = _merge(o_ret_s, z_s, att_s, q['gates'], wro, wso, wdo)
        xs, xsb = _out_ln(xs, merged_s, wo, ln2_g[l], ln2_b[l], alpha)
        xs, xsb = _ffn_ln(xs, *w2, ln3_g[l], ln3_b[l], alpha)
        s5r_s, s5i_s = _s5_state_out(jnp.swapaxes(h_s, 0, 1)[:bs, :, None, :], groups)
        outs_s.append((q['kv'][:bs, :kvw].reshape(bs, ls, DSA_KV_HEADS, DSA_HEAD_DIM),
                       q['kv'][:bs, kvw:].reshape(bs, ls, DSA_KV_HEADS, DSA_HEAD_DIM),
                       q['ikw'][:bs, :IDX_DIM].reshape(bs, ls, IDX_DIM), ret_s, s5r_s, s5i_s))

    k_p, v_p, ik_p, ret_p, s5r_p, s5i_p = [jnp.stack(a) for a in zip(*outs_p)]
    k_s, v_s, ik_s, ret_s, s5r_s, s5i_s = [jnp.stack(a) for a in zip(*outs_s)]
    return (xp.reshape(bp, lp, d), xs[:bs].reshape(bs, ls, d), k_p, v_p, ik_p, k_s, v_s, ik_s,
            ret_p, ret_s, s5r_p, s5i_p, s5r_s, s5i_s)
```

```python
import functools
import math
from typing import NamedTuple

import numpy as np
import jax
import jax.numpy as jnp
from jax import lax
from jax.experimental import pallas as pl
from jax.experimental.pallas import tpu as pltpu

F32 = jnp.float32
BF16 = jnp.bfloat16
I32 = jnp.int32

PAGE_SIZE = 128
RET_HEADS = 4
RET_DK = 256
RET_DV = 512
RET_CHUNK = 128
ROPE_BASE = 10000.0
S5_GROUP = 16
S5_STATE = 64
DSA_HEADS = 8
DSA_KV_HEADS = 2
DSA_HEAD_DIM = 128
IDX_HEADS = 8
IDX_DIM = 64
DSA_TOPK = 256
LN_EPS = 1e-5
GN_EPS = 1e-5
N_BRANCHES = 3

LANES = 128
SUBLANES = 8
VMEM_LIMIT_CAP = 56 * 1024 * 1024
S5_CHUNK = 16
S5_OCT = LANES // S5_GROUP
SAMPLE_ROWS = 16
MASK_NEG = -1e30
INT_MIN = -2 ** 31


def _cparams(semantics, *block_bytes):
    est = 2 * sum(block_bytes) + (8 << 20)
    return pltpu.CompilerParams(dimension_semantics=semantics,
                                vmem_limit_bytes=int(min(max(est, 32 << 20), VMEM_LIMIT_CAP)))


def _nbytes(shape, dtype):
    return int(np.prod(shape)) * jnp.dtype(dtype).itemsize


class _LayerWeight(NamedTuple):
    arr: jax.Array
    layer: int
    col0: int
    ncols: int

    @property
    def shape(self):
        return (self.arr.shape[1], self.ncols)

    def spec(self, blk, idx):
        assert self.col0 % blk[1] == 0
        layer, off = self.layer, self.col0 // blk[1]

        def index_map(*g):
            r, c = idx(*g)
            return (layer, r, c + off)

        return pl.BlockSpec((None,) + tuple(blk), index_map)


def _dot(a, b):
    return jnp.dot(a, b, preferred_element_type=F32)


def _dot_nt(a, b):
    return lax.dot_general(a, b, (((1,), (1,)), ((), ())), preferred_element_type=F32)


def _layernorm(y, g, b):
    mu = jnp.mean(y, axis=-1, keepdims=True)
    yc = y - mu
    var = jnp.mean(yc * yc, axis=-1, keepdims=True)
    return yc * lax.rsqrt(var + LN_EPS) * g + b


def _col_of_row(r):
    n = r.shape[1]
    eye = lax.broadcasted_iota(I32, (n, n), 0) == lax.broadcasted_iota(I32, (n, n), 1)
    return jnp.sum(jnp.where(eye, jnp.broadcast_to(r, (n, n)), 0.0), axis=1, keepdims=True)


def _mm_kernel(x_ref, w_ref, *o_refs):
    acc = _dot(x_ref[...], w_ref[...])
    for o_ref in o_refs:
        o_ref[...] = acc.astype(o_ref.dtype)


def _mm_oct_kernel(x_ref, w_ref, o_ref):
    acc = _dot(x_ref[...], w_ref[...])
    for i in range(o_ref.shape[0]):
        o_ref[i] = acc[:, i * LANES:(i + 1) * LANES]


def _mm(x, w, out_dtypes, oct_layout=False, name="mm"):
    m, k = x.shape
    n = w.shape[1]
    tm = min(m, 1024)
    tn = n if n <= 1024 else 512
    while w.col0 % tn:
        tn //= 2
    assert m % tm == 0 and n % tn == 0 and tn % LANES == 0
    w_resident = (n // tn) * m * k + k * n <= (m // tm) * k * n + m * k
    if w_resident:
        grid = (n // tn, m // tm)
        xi, wi, oi = (lambda j, i: (i, 0)), (lambda j, i: (0, j)), (lambda j, i: (i, j))
        ooct = lambda j, i: (j, i, 0)
    else:
        grid = (m // tm, n // tn)
        xi, wi, oi = (lambda i, j: (i, 0)), (lambda i, j: (0, j)), (lambda i, j: (i, j))
        ooct = lambda i, j: (j, i, 0)
    in_specs = [pl.BlockSpec((tm, k), xi), w.spec((k, tn), wi)]
    blk = [_nbytes((tm, k), BF16), _nbytes((k, tn), BF16)]
    if oct_layout:
        out_shape = jax.ShapeDtypeStruct((n // LANES, m, LANES), F32)
        out_specs = pl.BlockSpec((tn // LANES, tm, LANES), ooct)
        body = _mm_oct_kernel
        blk.append(_nbytes((tm, tn), F32))
    else:
        out_shape = tuple(jax.ShapeDtypeStruct((m, n), d) for d in out_dtypes)
        out_specs = tuple(pl.BlockSpec((tm, tn), oi) for _ in out_dtypes)
        body = _mm_kernel
        blk += [_nbytes((tm, tn), d) for d in out_dtypes]
    out = pl.pallas_call(body, grid=grid, in_specs=in_specs, out_specs=out_specs, out_shape=out_shape,
                         compiler_params=_cparams(("arbitrary", "arbitrary"), *blk), name=name)(x, w.arr)
    return out if oct_layout else (out[0] if len(out_dtypes) == 1 else out)


def _ffn_kernel(x_ref, wg_ref, wu_ref, wd_ref, g_ref, b_ref, o_ref, ob_ref, xb_scr, acc_scr, *, alpha):
    f = pl.program_id(1)

    @pl.when(f == 0)
    def _():
        xb_scr[...] = x_ref[...].astype(BF16)
        acc_scr[...] = jnp.zeros_like(acc_scr)

    xb = xb_scr[...]
    hg = _dot(xb, wg_ref[...])
    hu = _dot(xb, wu_ref[...])
    h = hg * jax.nn.sigmoid(hg) * hu
    acc_scr[...] += _dot(h.astype(BF16), wd_ref[...])

    @pl.when(f == pl.num_programs(1) - 1)
    def _():
        y = _layernorm(alpha * x_ref[...] + 0.5 * acc_scr[...], g_ref[...], b_ref[...])
        o_ref[...] = y
        ob_ref[...] = y.astype(BF16)


def _ffn_ln(x, wg, wu, wd, g, b, alpha):
    m, d = x.shape
    dff = wg.shape[1]
    tm = min(m, 512)
    tf = 512
    assert m % tm == 0 and dff % tf == 0
    row = lambda i, f: (i, 0)
    blk = [_nbytes((tm, d), F32), 3 * _nbytes((d, tf), BF16), _nbytes((tm, d), F32), _nbytes((tm, d), BF16),
           _nbytes((tm, d), F32)]
    return pl.pallas_call(
        functools.partial(_ffn_kernel, alpha=alpha),
        grid=(m // tm, dff // tf),
        in_specs=[pl.BlockSpec((tm, d), row),
                  wg.spec((d, tf), lambda i, f: (0, f)),
                  wu.spec((d, tf), lambda i, f: (0, f)),
                  wd.spec((tf, d), lambda i, f: (f, 0)),
                  pl.BlockSpec((1, d), lambda i, f: (0, 0)),
                  pl.BlockSpec((1, d), lambda i, f: (0, 0))],
        out_specs=(pl.BlockSpec((tm, d), row), pl.BlockSpec((tm, d), row)),
        out_shape=(jax.ShapeDtypeStruct((m, d), F32), jax.ShapeDtypeStruct((m, d), BF16)),
        scratch_shapes=[pltpu.VMEM((tm, d), BF16), pltpu.VMEM((tm, d), F32)],
        compiler_params=_cparams(("arbitrary", "arbitrary"), *blk),
        name="ffn_ln",
    )(x, wg.arr, wu.arr, wd.arr, g.reshape(1, d), b.reshape(1, d))


def _glu_kernel(y_ref, w_ref, o_ref):
    y = jnp.concatenate([y_ref[i] for i in range(y_ref.shape[0])], axis=-1)
    o_ref[...] = (y * jax.nn.sigmoid(_dot(y.astype(BF16), w_ref[...]))).astype(o_ref.dtype)


def _glu(y_oct, w):
    no, m, _ = y_oct.shape
    n = w.shape[1]
    tm = min(m, 512)
    blk = [_nbytes((no, tm, LANES), F32), _nbytes(w.shape, BF16), _nbytes((tm, n), BF16)]
    return pl.pallas_call(
        _glu_kernel, grid=(m // tm,),
        in_specs=[pl.BlockSpec((no, tm, LANES), lambda i: (0, i, 0)), w.spec(w.shape, lambda i: (0, 0))],
        out_specs=pl.BlockSpec((tm, n), lambda i: (i, 0)),
        out_shape=jax.ShapeDtypeStruct((m, n), BF16),
        compiler_params=_cparams(("arbitrary",), *blk),
        name="s5_glu",
    )(y_oct, w.arr)


def _merge_kernel(o_ref, z_ref, a_ref, g0_ref, g1_ref, g2_ref, wr_ref, ws_ref, wd_ref, m_ref):
    ret = _dot(o_ref[...], wr_ref[...])
    s5 = _dot(z_ref[...], ws_ref[...])
    dsa = _dot(a_ref[...], wd_ref[...])
    merged = (jax.nn.sigmoid(g0_ref[...]) * ret + jax.nn.sigmoid(g1_ref[...]) * s5
              + jax.nn.sigmoid(g2_ref[...]) * dsa)
    m_ref[...] = merged.astype(m_ref.dtype)


def _merge(o, z, a, gates, wr, ws, wd):
    m = o.shape[0]
    d = wr.shape[1]
    tm = min(m, 512)
    tn = min(d, 512)
    nb = d // tn
    blk = [_nbytes((tm, o.shape[1]), BF16), 2 * _nbytes((tm, z.shape[1]), BF16), 3 * _nbytes((tm, tn), F32),
           _nbytes((o.shape[1], tn), BF16), 2 * _nbytes((z.shape[1], tn), BF16), _nbytes((tm, tn), BF16)]
    gate_spec = lambda br: pl.BlockSpec((tm, tn), lambda i, j: (i, br * nb + j))
    return pl.pallas_call(
        _merge_kernel, grid=(m // tm, nb),
        in_specs=[pl.BlockSpec((tm, o.shape[1]), lambda i, j: (i, 0)),
                  pl.BlockSpec((tm, z.shape[1]), lambda i, j: (i, 0)),
                  pl.BlockSpec((tm, a.shape[1]), lambda i, j: (i, 0)),
                  gate_spec(0), gate_spec(1), gate_spec(2),
                  wr.spec((wr.shape[0], tn), lambda i, j: (0, j)),
                  ws.spec((ws.shape[0], tn), lambda i, j: (0, j)),
                  wd.spec((wd.shape[0], tn), lambda i, j: (0, j))],
        out_specs=pl.BlockSpec((tm, tn), lambda i, j: (i, j)),
        out_shape=jax.ShapeDtypeStruct((m, d), BF16),
        compiler_params=_cparams(("arbitrary", "arbitrary"), *blk),
        name="branch_merge",
    )(o, z, a, gates, gates, gates, wr.arr, ws.arr, wd.arr)


def _out_ln_kernel(x_ref, m_ref, w_ref, g_ref, b_ref, o_ref, ob_ref, *, alpha):
    y = _layernorm(alpha * x_ref[...] + _dot(m_ref[...], w_ref[...]), g_ref[...], b_ref[...])
    o_ref[...] = y
    ob_ref[...] = y.astype(BF16)


def _out_ln(x, merged, w, g, b, alpha):
    m, d = x.shape
    tm = min(m, 512)
    row = lambda i: (i, 0)
    blk = [2 * _nbytes((tm, d), F32), 2 * _nbytes((tm, d), BF16), _nbytes((d, d), BF16)]
    return pl.pallas_call(
        functools.partial(_out_ln_kernel, alpha=alpha), grid=(m // tm,),
        in_specs=[pl.BlockSpec((tm, d), row), pl.BlockSpec((tm, d), row), w.spec((d, d), lambda i: (0, 0)),
                  pl.BlockSpec((1, d), lambda i: (0, 0)), pl.BlockSpec((1, d), lambda i: (0, 0))],
        out_specs=(pl.BlockSpec((tm, d), row), pl.BlockSpec((tm, d), row)),
        out_shape=(jax.ShapeDtypeStruct((m, d), F32), jax.ShapeDtypeStruct((m, d), BF16)),
        compiler_params=_cparams(("arbitrary",), *blk),
        name="out_ln",
    )(x, merged, w.arr, g.reshape(1, d), b.reshape(1, d))


def _rope(x, cos, sin):
    half = x.shape[-1] // 2
    x1, x2 = x[:, :half], x[:, half:]
    return jnp.concatenate([x1 * cos - x2 * sin, x1 * sin + x2 * cos], axis=-1)


def _group_norm_gate(o, gate):
    mu = jnp.mean(o, axis=-1, keepdims=True)
    oc = o - mu
    var = jnp.mean(oc * oc, axis=-1, keepdims=True)
    return gate * jax.nn.sigmoid(gate) * (oc * lax.rsqrt(var + GN_EPS))


def _ret_kernel(qk_ref, v_ref, g_ref, cos_ref, sin_ref, o_ref, st_ref):
    @pl.when(pl.program_id(1) == 0)
    def _():
        st_ref[...] = jnp.zeros_like(st_ref)

    c = qk_ref.shape[0]
    cos, sin = cos_ref[...], sin_ref[...]
    ri = lax.broadcasted_iota(I32, (c, c), 0).astype(F32)
    ci = lax.broadcasted_iota(I32, (c, c), 1).astype(F32)
    diff = ri - ci
    ti = lax.broadcasted_iota(I32, (c, 1), 0).astype(F32)
    for h in range(RET_HEADS):
        lg = math.log(1.0 - 2.0 ** (-5.0 - h))
        dmask = jnp.where(diff >= 0, jnp.exp(jnp.maximum(diff, 0.0) * lg), 0.0)
        q_dec = jnp.exp((ti + 1.0) * lg)
        k_dec = jnp.exp((c - 1.0 - ti) * lg)
        c_dec = math.exp(c * lg)
        q = _rope(qk_ref[:, h * RET_DK:(h + 1) * RET_DK], cos, sin)
        k = _rope(qk_ref[:, (RET_HEADS + h) * RET_DK:(RET_HEADS + h + 1) * RET_DK], cos, sin) * (RET_DK ** -0.5)
        v = v_ref[:, h * RET_DV:(h + 1) * RET_DV]
        s = st_ref[0, h]
        qb = q.astype(BF16)
        sc = _dot_nt(qb, k.astype(BF16)) * dmask
        o = _dot(sc.astype(BF16), v) + _dot(qb, s.astype(BF16)) * q_dec
        kd_t = jnp.transpose(k * k_dec).astype(BF16)
        st_ref[0, h] = c_dec * s + _dot(kd_t, v)
        o_ref[:, h * RET_DV:(h + 1) * RET_DV] = _group_norm_gate(
            o, g_ref[:, h * RET_DV:(h + 1) * RET_DV]).astype(o_ref.dtype)


def _retention(qk, v, gate, cos, sin, bn):
    m = qk.shape[0]
    length = m // bn
    c = RET_CHUNK
    nc = length // c
    hv = RET_HEADS * RET_DV
    row = lambda b, j: (b * nc + j, 0)
    blk = [_nbytes((c, qk.shape[1]), F32), _nbytes((c, hv), BF16), _nbytes((c, hv), F32), _nbytes((c, hv), BF16),
           _nbytes((RET_HEADS, RET_DK, RET_DV), F32)]
    return pl.pallas_call(
        _ret_kernel, grid=(bn, nc),
        in_specs=[pl.BlockSpec((c, qk.shape[1]), row), pl.BlockSpec((c, hv), row), pl.BlockSpec((c, hv), row),
                  pl.BlockSpec((c, RET_DK // 2), lambda b, j: (j, 0)),
                  pl.BlockSpec((c, RET_DK // 2), lambda b, j: (j, 0))],
        out_specs=(pl.BlockSpec((c, hv), row),
                   pl.BlockSpec((1, RET_HEADS, RET_DK, RET_DV), lambda b, j: (b, 0, 0, 0))),
        out_shape=(jax.ShapeDtypeStruct((m, hv), BF16),
                   jax.ShapeDtypeStruct((bn, RET_HEADS, RET_DK, RET_DV), F32)),
        compiler_params=_cparams(("arbitrary", "arbitrary"), *blk),
        name="retention",
    )(qk, v, gate, cos, sin)


def _ret_step_kernel(qk_ref, v_ref, g_ref, cos_ref, sin_ref, s0_ref, o_ref, st_ref):
    row = pl.ds(pl.program_id(0), 1)
    cos, sin = cos_ref[...], sin_ref[...]
    for h in range(RET_HEADS):
        decay = 1.0 - 2.0 ** (-5.0 - h)
        q = _rope(qk_ref[row, h * RET_DK:(h + 1) * RET_DK], cos, sin)
        k = _rope(qk_ref[row, (RET_HEADS + h) * RET_DK:(RET_HEADS + h + 1) * RET_DK], cos, sin) * (RET_DK ** -0.5)
        v = v_ref[row, h * RET_DV:(h + 1) * RET_DV]
        s0 = s0_ref[0, h]
        st_ref[0, h] = decay * s0 + _col_of_row(k) * v
        o = (jnp.sum(q * k, axis=1, keepdims=True) * v
             + jnp.sum(_col_of_row(q) * s0, axis=0, keepdims=True) * decay)
        o_ref[0, :, h * RET_DV:(h + 1) * RET_DV] = _group_norm_gate(o, g_ref[row, h * RET_DV:(h + 1) * RET_DV])


def _retention_step(qk, v, gate, cos, sin, s0):
    bn = s0.shape[0]
    const = lambda b: (0, 0)
    state = pl.BlockSpec((1,) + s0.shape[1:], lambda b: (b, 0, 0, 0))
    return pl.pallas_call(
        _ret_step_kernel, grid=(bn,),
        in_specs=[pl.BlockSpec(qk.shape, const), pl.BlockSpec(v.shape, const), pl.BlockSpec(gate.shape, const),
                  pl.BlockSpec(cos.shape, const), pl.BlockSpec(sin.shape, const), state],
        out_specs=(pl.BlockSpec((1, 1, v.shape[1]), lambda b: (b, 0, 0)), state),
        out_shape=(jax.ShapeDtypeStruct((bn, 1, v.shape[1]), F32), jax.ShapeDtypeStruct(s0.shape, F32)),
        compiler_params=_cparams(("arbitrary",), 2 * _nbytes(s0.shape[1:], F32)),
        name="retention_step",
    )(qk, v, gate, cos, sin, s0)


def _s5_tables(a_re, a_im, log_dt, b_re, b_im, c_re, c_im, d):
    g, p = a_re.shape
    nc = b_re.shape[-1]
    no = g // S5_OCT
    t = S5_CHUNK
    dt = jnp.exp(log_dt)[:, None]
    mag = jnp.exp(a_re * dt)
    ab_re = mag * jnp.cos(a_im * dt)
    ab_im = mag * jnp.sin(a_im * dt)
    den = a_re * a_re + a_im * a_im
    x_re = ab_re - 1.0
    f_re = (x_re * a_re + ab_im * a_im) / den
    f_im = (ab_im * a_re - x_re * a_im) / den
    bb_re = f_re[..., None] * b_re - f_im[..., None] * b_im
    bb_im = f_re[..., None] * b_im + f_im[..., None] * b_re

    def powers(n):
        n = n.astype(F32)[:, None, None]
        pmag = jnp.exp(n * (a_re * dt)[None])
        return pmag * jnp.cos(n * (a_im * dt)[None]), pmag * jnp.sin(n * (a_im * dt)[None])

    pw_re, pw_im = powers(jnp.arange(t + 1))
    rev_re, rev_im = powers(t - 1 - jnp.arange(t))
    bt_re, bt_im = jnp.swapaxes(bb_re, 1, 2), jnp.swapaxes(bb_im, 1, 2)

    def packed(v_re, v_im):
        out = []
        for v in (v_re, v_re, v_im, v_im):
            v = v.reshape(v.shape[0], no, S5_OCT, nc, p)
            out.append(jnp.swapaxes(v, 0, 1).reshape(no, v.shape[0] * LANES, p))
        return jnp.concatenate(out, axis=-1)

    def b_packed(power_re, power_im):
        pr, pi = power_re[:, :, None, :], power_im[:, :, None, :]
        return packed(pr * bt_re[None] - pi * bt_im[None], pr * bt_im[None] + pi * bt_re[None])

    def c_packed(power_re, power_im):
        pr, pi = power_re[:, :, None, :], power_im[:, :, None, :]
        return packed(c_re[None] * pr - c_im[None] * pi, -(c_re[None] * pi + c_im[None] * pr))

    def state_row(v_re, v_im):
        return jnp.concatenate([v_re.reshape(no, 1, S5_OCT * p), v_im.reshape(no, 1, S5_OCT * p)], axis=-1)

    d_row = d.reshape(no, 1, LANES)
    return dict(
        b_end=b_packed(rev_re, rev_im),
        c_in=c_packed(pw_re[1:], pw_im[1:]),
        a_row=state_row(pw_re[t], pw_im[t]),
        d_row=jnp.tile(d_row, (1, 1, t)),
        b_one=b_packed(pw_re[:1], pw_im[:1]),
        c_one=c_packed(pw_re[:1], pw_im[:1]),
        a_one=state_row(ab_re, ab_im),
        d_one=d_row,
    )


def _expand_groups(packed_ref, rows):
    n = rows.stop - rows.start
    g_row = (lax.broadcasted_iota(I32, (n, LANES), 0) // S5_GROUP) % S5_OCT
    g_half = lax.broadcasted_iota(I32, (n, LANES), 1) // S5_STATE
    tiles = []
    for part in range(2):
        src = packed_ref[0, rows, part * LANES:(part + 1) * LANES]
        for pair in range(S5_OCT // 2):
            tiles.append(jnp.where(g_row == 2 * pair + g_half, src, 0.0))
    return jnp.concatenate(tiles, axis=-1)


def _s5_kernel(u_ref, cone_ref, bend_ref, cin_ref, a_ref, d_ref, y_ref, h_ref,
               toep_scr, bfull_scr, cfull_scr, e_scr, s_scr):
    rows, t = e_scr.shape[0], S5_CHUNK

    @pl.when(pl.program_id(1) == 0)
    def _():
        toep_scr[...] = jnp.zeros_like(toep_scr)
        c_now = _expand_groups(cone_ref, slice(0, LANES))
        for s in range(t):
            sl = slice(s * LANES, (s + 1) * LANES)
            b_pow = _expand_groups(bend_ref, sl)
            bfull_scr[sl, :] = b_pow.astype(BF16)
            cfull_scr[sl, :] = _expand_groups(cin_ref, sl).astype(BF16)
            lag = t - 1 - s
            blk = lax.dot_general(b_pow, c_now, (((1,), (1,)), ((), ())), precision=lax.Precision.HIGHEST,
                                  preferred_element_type=F32).astype(BF16)
            for r in range(t - lag):
                toep_scr[r * LANES:(r + 1) * LANES, (r + lag) * LANES:(r + lag + 1) * LANES] = blk

    u = jnp.concatenate([u_ref[0, pl.ds(i, rows, stride=t), :] for i in range(t)], axis=-1)
    ub = u.astype(BF16)
    e_scr[...] = _dot(ub, bfull_scr[...])
    half = a_ref.shape[-1] // 2
    a_re, a_im = a_ref[0, :, :half], a_ref[0, :, half:]

    def step(k, carry):
        s_re, s_im = carry
        s_scr[pl.ds(k, 1), :] = jnp.concatenate([s_re, s_im], axis=-1)
        e = e_scr[pl.ds(k, 1), :]
        return (a_re * s_re - a_im * s_im + e[:, :half], a_re * s_im + a_im * s_re + e[:, half:])

    zero = jnp.zeros((1, half), F32)
    s_re, s_im = lax.fori_loop(0, rows, step, (zero, zero))
    h_ref[0, 0] = jnp.concatenate([s_re, s_im], axis=-1)
    y = jax.nn.gelu(_dot(ub, toep_scr[...]) + _dot_nt(s_scr[...].astype(BF16), cfull_scr[...]) + d_ref[0] * u)
    for i in range(t):
        y_ref[0, pl.ds(i, rows, stride=t), :] = y[:, i * LANES:(i + 1) * LANES]


def _s5(u_oct, tb, bn):
    no, m, _ = u_oct.shape
    t = S5_CHUNK
    steps = m // bn
    rows = steps // t
    w = t * LANES
    ns = tb['a_row'].shape[-1]
    pk = tb['b_end'].shape[-1]
    oct_blk = lambda shape: pl.BlockSpec((1,) + shape, lambda o, b: (o,) + (0,) * len(shape))
    blk = [2 * _nbytes((rows, w), F32), 2 * _nbytes((w, pk), F32),
           (_nbytes((w, w), BF16) + 2 * _nbytes((w, ns), BF16) + 2 * _nbytes((rows, ns), F32)) // 2]
    return pl.pallas_call(
        _s5_kernel, grid=(no, bn),
        in_specs=[pl.BlockSpec((1, steps, LANES), lambda o, b: (o, b, 0)),
                  oct_blk((LANES, pk)), oct_blk((w, pk)), oct_blk((w, pk)), oct_blk((1, ns)),
                  oct_blk((1, w))],
        out_specs=(pl.BlockSpec((1, steps, LANES), lambda o, b: (o, b, 0)),
                   pl.BlockSpec((1, 1, 1, ns), lambda o, b: (b, o, 0, 0))),
        out_shape=(jax.ShapeDtypeStruct(u_oct.shape, F32), jax.ShapeDtypeStruct((bn, no, 1, ns), F32)),
        scratch_shapes=[pltpu.VMEM((w, w), BF16), pltpu.VMEM((w, ns), BF16), pltpu.VMEM((w, ns), BF16),
                        pltpu.VMEM((rows, ns), F32), pltpu.VMEM((rows, ns), F32)],
        compiler_params=_cparams(("arbitrary", "arbitrary"), *blk),
        name="s5_scan",
    )(u_oct, tb['c_one'], tb['b_end'], tb['c_in'], tb['a_row'], tb['d_row'])


def _s5_step_kernel(u_ref, h0_ref, b_ref, c_ref, a_ref, d_ref, y_ref, h_ref):
    u = u_ref[0]
    half = a_ref.shape[-1] // 2
    a_re, a_im = a_ref[0, :, :half], a_ref[0, :, half:]
    h0 = h0_ref[0]
    h0_re, h0_im = h0[:, :half], h0[:, half:]
    whole = slice(0, LANES)
    bu = _dot(u.astype(BF16), _expand_groups(b_ref, whole).astype(BF16))
    h_re = a_re * h0_re - a_im * h0_im + bu[:, :half]
    h_im = a_re * h0_im + a_im * h0_re + bu[:, half:]
    h = jnp.concatenate([h_re, h_im], axis=-1)
    h_ref[0] = h
    y_ref[0] = jax.nn.gelu(_dot_nt(h.astype(BF16), _expand_groups(c_ref, whole).astype(BF16)) + d_ref[0] * u)


def _s5_step(u_oct, h0, tb):
    no, rows, _ = u_oct.shape
    ns = h0.shape[-1]
    pk = tb['b_one'].shape[-1]
    o3 = lambda shape: pl.BlockSpec((1,) + shape, lambda o: (o, 0, 0))
    return pl.pallas_call(
        _s5_step_kernel, grid=(no,),
        in_specs=[o3((rows, LANES)), o3((rows, ns)), o3((LANES, pk)), o3((LANES, pk)), o3((1, ns)), o3((1, LANES))],
        out_specs=(o3((rows, LANES)), o3((rows, ns))),
        out_shape=(jax.ShapeDtypeStruct(u_oct.shape, F32), jax.ShapeDtypeStruct(h0.shape, F32)),
        compiler_params=_cparams(("arbitrary",), _nbytes((LANES, ns), BF16) * 2),
        name="s5_step",
    )(u_oct, h0, tb['b_one'], tb['c_one'], tb['a_one'], tb['d_one'])


def _sort_key(x):
    bits = pltpu.bitcast(x, I32)
    return jnp.where(bits >= 0, bits, bits ^ jnp.int32(0x7FFFFFFF))


def _kth_largest_key(count_ge, nsel, shape):
    res = jnp.where(count_ge(jnp.zeros(shape, I32)) >= nsel, jnp.int32(0), jnp.int32(INT_MIN))

    def bit_step(i, res):
        cand = res | jnp.left_shift(jnp.int32(1), 30 - i)
        return jnp.where(count_ge(cand) >= nsel, cand, res)

    return lax.fori_loop(0, 31, bit_step, res)


def _dsa_kernel(dq_ref, iq_ref, ikw_ref, ikb_ref, kv_ref, o_ref, key_scr, bias_scr, m_scr, l_scr, acc_scr,
                *, nsel):
    tq = dq_ref.shape[0]
    jq = pl.program_id(1)
    nkb = jq + 1
    krow = lax.broadcasted_iota(I32, (tq, tq), 0)
    qcol = lax.broadcasted_iota(I32, (tq, tq), 1)
    qpos = jq * tq + qcol
    hd = DSA_HEAD_DIM
    gsz = DSA_HEADS // DSA_KV_HEADS

    def over_keys(x, op):
        part = op(x.reshape(x.shape[0] // SUBLANES, SUBLANES, x.shape[1]), axis=0)
        return op(part, axis=0, keepdims=True)

    iq_st = jnp.concatenate([iq_ref[:, h * LANES:(h + 1) * LANES] for h in range(IDX_HEADS)], axis=0)
    w_t = jnp.transpose(ikw_ref[...])

    def score_block(kb, carry):
        ik = ikb_ref[pl.ds(pl.multiple_of(kb * tq, tq), tq), :]
        sh = jnp.maximum(_dot_nt(ik, iq_st), 0.0)
        acc = jnp.zeros((tq, tq), F32)
        for h in range(IDX_HEADS):
            acc = acc + sh[:, h * tq:(h + 1) * tq] * w_t[IDX_DIM + h:IDX_DIM + h + 1, :]
        acc = jnp.where(kb * tq + krow <= qpos, acc, -jnp.inf)
        key_scr[kb] = _sort_key(acc)
        return carry

    lax.fori_loop(0, nkb, score_block, 0)

    acc_rows = 4 * SUBLANES

    def count(pred):
        def body(kb, c):
            hit = jnp.where(pred(key_scr[kb]), 1.0, 0.0)
            return c + jnp.sum(hit.reshape(tq // acc_rows, acc_rows, tq), axis=0)
        part = lax.fori_loop(0, nkb, body, jnp.zeros((acc_rows, tq), F32))
        return jnp.sum(part, axis=0, keepdims=True)

    kth = _kth_largest_key(lambda cand: count(lambda key: key >= cand), float(nsel), (1, tq))
    at_least = count(lambda key: key >= kth)

    def select_all_ge():
        def block(kb, carry):
            sel = (key_scr[kb] >= kth) & (kb * tq + krow <= qpos)
            bias_scr[kb] = jnp.where(sel, 0.0, MASK_NEG)
            return carry
        lax.fori_loop(0, nkb, block, 0)

    def select_with_ties():
        need = float(nsel) - count(lambda key: key > kth)
        tri = jnp.where(krow >= qcol, 1.0, 0.0).astype(BF16)

        def block(kb, taken):
            key = key_scr[kb]
            tie = jnp.where(key == kth, 1.0, 0.0)
            rank = taken + _dot(tri, tie.astype(BF16))
            sel = ((key > kth) | ((tie > 0.0) & (rank <= need))) & (kb * tq + krow <= qpos)
            bias_scr[kb] = jnp.where(sel, 0.0, MASK_NEG)
            return taken + over_keys(tie, jnp.sum)

        lax.fori_loop(0, nkb, block, jnp.zeros((1, tq), F32))

    lax.cond(jnp.max(at_least) > float(nsel), select_with_ties, select_all_ge)

    scale2 = (hd ** -0.5) * math.log2(math.e)
    for g in range(DSA_KV_HEADS):
        qg = jnp.concatenate([dq_ref[:, (g * gsz + i) * hd:(g * gsz + i + 1) * hd] for i in range(gsz)], axis=0)
        m_scr[...] = jnp.full_like(m_scr, -jnp.inf)
        l_scr[...] = jnp.zeros_like(l_scr)
        acc_scr[...] = jnp.zeros_like(acc_scr)

        def attend_block(kb, carry):
            ks = pl.ds(pl.multiple_of(kb * tq, tq), tq)
            kblk = kv_ref[ks, g * hd:(g + 1) * hd]
            vblk = kv_ref[ks, (DSA_KV_HEADS + g) * hd:(DSA_KV_HEADS + g + 1) * hd]
            bias = bias_scr[kb]
            lg = _dot_nt(kblk, qg) * scale2 + jnp.concatenate([bias] * gsz, axis=1)
            m_old = m_scr[...]
            m_new = jnp.maximum(m_old, over_keys(lg, jnp.max))
            alpha = jnp.exp2(m_old - m_new)
            p = jnp.exp2(lg - m_new)
            l_scr[...] = alpha * l_scr[...] + over_keys(p, jnp.sum)
            v_t = jnp.transpose(vblk.astype(F32)).astype(BF16)
            acc_scr[...] = alpha * acc_scr[...] + _dot(v_t, p.astype(BF16))
            m_scr[...] = m_new
            return carry

        lax.fori_loop(0, nkb, attend_block, 0)
        out_t = acc_scr[...] / l_scr[...]
        for i in range(gsz):
            o_ref[:, (g * gsz + i) * hd:(g * gsz + i + 1) * hd] = jnp.transpose(
                out_t[:, i * tq:(i + 1) * tq]).astype(o_ref.dtype)


def _dsa_prompt(dq, iq, ikw, ikb, kvb, bn):
    m = dq.shape[0]
    length = m // bn
    tq = 256
    nq = length // tq
    nsel = min(DSA_TOPK, length // 4)
    qrow = lambda b, j: (b * nq + j, 0)
    full = lambda b, j: (b, 0)
    gsz = DSA_HEADS // DSA_KV_HEADS
    blk = [_nbytes((tq, dq.shape[1]), BF16) * 3, _nbytes((length, LANES), BF16), _nbytes((length, kvb.shape[1]), BF16),
           _nbytes((nq, tq, tq), F32)]
    return pl.pallas_call(
        functools.partial(_dsa_kernel, nsel=nsel), grid=(bn, nq),
        in_specs=[pl.BlockSpec((tq, dq.shape[1]), qrow), pl.BlockSpec((tq, iq.shape[1]), qrow),
                  pl.BlockSpec((tq, LANES), qrow), pl.BlockSpec((length, LANES), full),
                  pl.BlockSpec((length, kvb.shape[1]), full)],
        out_specs=pl.BlockSpec((tq, dq.shape[1]), qrow),
        out_shape=jax.ShapeDtypeStruct(dq.shape, BF16),
        scratch_shapes=[pltpu.VMEM((nq, tq, tq), I32), pltpu.VMEM((nq, tq, tq), F32),
                        pltpu.VMEM((1, gsz * tq), F32), pltpu.VMEM((1, gsz * tq), F32),
                        pltpu.VMEM((DSA_HEAD_DIM, gsz * tq), F32)],
        compiler_params=_cparams(("arbitrary", "arbitrary"), *blk),
        name="dsa_prompt",
    )(dq, iq, ikw, ikb, kvb)


MAX_PAGES_PER_STEP = 16


def _idx_heads(iq_row):
    return jnp.concatenate([iq_row[:, h * LANES:h * LANES + IDX_DIM] for h in range(IDX_HEADS)], axis=0)


def _page_score_kernel(pt_ref, iq_ref, ikw_ref, *rest):
    page_refs, o_ref = rest[:-1], rest[-1]
    b = pl.program_id(0)
    iq_h = jnp.concatenate([_idx_heads(iq_ref[pl.ds(b, 1), :]),
                            jnp.zeros((SAMPLE_ROWS - IDX_HEADS, IDX_DIM), F32)], axis=0).astype(BF16)
    w_col = _col_of_row(ikw_ref[pl.ds(b, 1), IDX_DIM:IDX_DIM + SAMPLE_ROWS])
    for i, page in enumerate(page_refs):
        sh = jnp.maximum(_dot(iq_h, page[...].astype(BF16)), 0.0)
        o_ref[0, i:i + 1, :] = jnp.sum(sh * w_col, axis=0, keepdims=True)


def _page_scores(page_table, iq, ikw, cache_idx_t, layer):
    bn, npages = page_table.shape
    pg = min(MAX_PAGES_PER_STEP, npages)
    const = lambda b, s, pt: (0, 0)

    def page_spec(i):
        return pl.BlockSpec((None, None, IDX_DIM, PAGE_SIZE), lambda b, s, pt: (layer, pt[b, s * pg + i], 0, 0))

    return pl.pallas_call(
        _page_score_kernel,
        grid_spec=pltpu.PrefetchScalarGridSpec(
            num_scalar_prefetch=1, grid=(bn, npages // pg),
            in_specs=[pl.BlockSpec(iq.shape, const), pl.BlockSpec(ikw.shape, const)]
            + [page_spec(i) for i in range(pg)],
            out_specs=pl.BlockSpec((1, pg, PAGE_SIZE), lambda b, s, pt: (b, s, 0))),
        out_shape=jax.ShapeDtypeStruct((bn, npages, PAGE_SIZE), F32),
        compiler_params=_cparams(("arbitrary", "arbitrary")),
        name="page_scores",
    )(page_table, iq, ikw, *([cache_idx_t] * pg))


def _page_select_kernel(sc_ref, iq_ref, ikw_ref, bias_ref, bias_self_ref, *, nsel):
    b = pl.program_id(0)
    npages, psz = sc_ref.shape[1:]
    iq_h = _idx_heads(iq_ref[pl.ds(b, 1), :])
    ikw = ikw_ref[pl.ds(b, 1), :]
    w_col = _col_of_row(ikw[:, IDX_DIM:IDX_DIM + IDX_HEADS])
    s_self = jnp.sum(jnp.maximum(jnp.sum(iq_h * ikw[:, :IDX_DIM], axis=1, keepdims=True), 0.0) * w_col,
                     axis=0, keepdims=True)
    key = _sort_key(sc_ref[0])
    key_self = _sort_key(s_self)

    def total(x):
        return jnp.sum(jnp.sum(x, axis=1, keepdims=True), axis=0, keepdims=True)

    def count_ge(cand):
        return total(jnp.where(key >= cand, 1.0, 0.0)) + jnp.where(key_self >= cand, 1.0, 0.0)

    kth = _kth_largest_key(count_ge, float(nsel), (1, 1))
    need = float(nsel) - (total(jnp.where(key > kth, 1.0, 0.0)) + jnp.where(key_self > kth, 1.0, 0.0))
    tie = jnp.where(key == kth, 1.0, 0.0)
    r_in = lax.broadcasted_iota(I32, (psz, psz), 0)
    c_in = lax.broadcasted_iota(I32, (psz, psz), 1)
    in_page = _dot(tie.astype(BF16), jnp.where(r_in <= c_in, 1.0, 0.0).astype(BF16))
    per_page = jnp.broadcast_to(jnp.sum(tie, axis=1, keepdims=True), (npages, psz)).astype(BF16)
    r_pg = lax.broadcasted_iota(I32, (npages, npages), 0)
    c_pg = lax.broadcasted_iota(I32, (npages, npages), 1)
    before = _dot(jnp.where(c_pg < r_pg, 1.0, 0.0).astype(BF16), per_page)
    sel = (key > kth) | ((tie > 0.0) & (before + in_page <= need))
    r_dup = lax.broadcasted_iota(I32, (psz, bias_ref.shape[2]), 0)
    c_dup = lax.broadcasted_iota(I32, (psz, bias_ref.shape[2]), 1)
    dup = jnp.where(c_dup // DSA_KV_HEADS == r_dup, 1.0, 0.0).astype(BF16)
    spread = _dot(jnp.where(sel, 1.0, 0.0).astype(BF16), dup)
    bias_ref[0] = jnp.where(spread > 0.5, 0.0, MASK_NEG)
    sel_self = (key_self > kth) | ((key_self == kth) & (total(tie) + 1.0 <= need))
    bias_self_ref[0] = jnp.broadcast_to(jnp.where(sel_self, 0.0, MASK_NEG), (1, LANES))


def _page_select(scores, iq, ikw, nsel):
    bn, npages, psz = scores.shape
    const = lambda b: (0, 0)
    wide = DSA_KV_HEADS * psz
    return pl.pallas_call(
        functools.partial(_page_select_kernel, nsel=nsel), grid=(bn,),
        in_specs=[pl.BlockSpec((1, npages, psz), lambda b: (b, 0, 0)),
                  pl.BlockSpec(iq.shape, const), pl.BlockSpec(ikw.shape, const)],
        out_specs=(pl.BlockSpec((1, npages, wide), lambda b: (b, 0, 0)),
                   pl.BlockSpec((1, 1, LANES), lambda b: (b, 0, 0))),
        out_shape=(jax.ShapeDtypeStruct((bn, npages, wide), F32), jax.ShapeDtypeStruct((bn, 1, LANES), F32)),
        compiler_params=_cparams(("arbitrary",)),
        name="page_select",
    )(scores, iq, ikw)


def _page_attend_kernel(pt_ref, dq_ref, kvs_ref, bias_ref, bself_ref, *rest, npg):
    k_refs, v_refs = rest[:npg], rest[npg:2 * npg]
    o_ref, m_scr, l_scr, acc_scr = rest[2 * npg:]
    b = pl.program_id(0)
    s = pl.program_id(1)
    hd = DSA_HEAD_DIM
    gsz = DSA_HEADS // DSA_KV_HEADS
    scale = hd ** -0.5
    prow = k_refs[0].shape[0]
    dq_row = dq_ref[pl.ds(b, 1), :]
    q = jnp.concatenate([dq_row[:, h * hd:(h + 1) * hd] for h in range(DSA_HEADS)]
                        + [jnp.zeros((SAMPLE_ROWS - DSA_HEADS, hd), F32)], axis=0)
    kv_of_head = lax.broadcasted_iota(I32, (SAMPLE_ROWS, 1), 0) // gsz

    @pl.when(s == 0)
    def _():
        m_scr[...] = jnp.full_like(m_scr, -jnp.inf)
        l_scr[...] = jnp.zeros_like(l_scr)
        acc_scr[...] = jnp.zeros_like(acc_scr)

    def update(lg, pv_of):
        m_old = m_scr[...]
        m_new = jnp.maximum(m_old, jnp.max(lg, axis=1, keepdims=True))
        alpha = jnp.exp(m_old - m_new)
        p = jnp.exp(lg - m_new)
        l_scr[...] = alpha * l_scr[...] + jnp.sum(p, axis=1, keepdims=True)
        acc_scr[...] = alpha * acc_scr[...] + pv_of(p)
        m_scr[...] = m_new

    qb = q.astype(BF16)
    lg = jnp.concatenate([_dot_nt(qb, k_refs[i][...].astype(BF16)) for i in range(npg)], axis=1) * scale
    bias = jnp.concatenate([bias_ref[0, i:i + 1, :] for i in range(npg)], axis=1)
    col = lax.broadcasted_iota(I32, lg.shape, 1)
    lg = jnp.where(jnp.bitwise_and(col, DSA_KV_HEADS - 1) == kv_of_head, lg + bias, MASK_NEG)

    def pv_pages(p):
        pb = p.astype(BF16)
        acc = jnp.zeros((SAMPLE_ROWS, hd), F32)
        for i in range(npg):
            acc = acc + _dot(pb[:, i * prow:(i + 1) * prow], v_refs[i][...].astype(BF16))
        return acc

    update(lg, pv_pages)

    @pl.when(s == pl.num_programs(1) - 1)
    def _():
        kvs = kvs_ref[pl.ds(b, 1), :]
        k_self = jnp.where(kv_of_head == 0, kvs[:, :hd], kvs[:, hd:2 * hd])
        v_self = jnp.where(kv_of_head == 0, kvs[:, 2 * hd:3 * hd], kvs[:, 3 * hd:])
        lg_self = jnp.sum(q * k_self, axis=1, keepdims=True) * scale + bself_ref[0, :, :1]
        update(lg_self, lambda p: p * v_self)
        out = acc_scr[...] / l_scr[...]
        for h in range(DSA_HEADS):
            o_ref[0, :, h * hd:(h + 1) * hd] = out[h:h + 1]


def _page_attend(page_table, dq, kvs, bias, bias_self, cache_k, cache_v, layer):
    assert DSA_KV_HEADS == 2
    bn, npages = page_table.shape
    pg = min(MAX_PAGES_PER_STEP, npages)
    prow, hd = cache_k.shape[-2:]
    const = lambda b, s, pt: (0, 0)

    def page_spec(i):
        return pl.BlockSpec((None, None, prow, hd), lambda b, s, pt: (layer, pt[b, s * pg + i], 0, 0))

    return pl.pallas_call(
        functools.partial(_page_attend_kernel, npg=pg),
        grid_spec=pltpu.PrefetchScalarGridSpec(
            num_scalar_prefetch=1, grid=(bn, npages // pg),
            in_specs=[pl.BlockSpec(dq.shape, const), pl.BlockSpec(kvs.shape, const),
                      pl.BlockSpec((1, pg, prow), lambda b, s, pt: (b, s, 0)),
                      pl.BlockSpec((1, 1, LANES), lambda b, s, pt: (b, 0, 0))]
            + [page_spec(i) for i in range(pg)] + [page_spec(i) for i in range(pg)],
            out_specs=pl.BlockSpec((1, 1, dq.shape[1]), lambda b, s, pt: (b, 0, 0)),
            scratch_shapes=[pltpu.VMEM((SAMPLE_ROWS, 1), F32), pltpu.VMEM((SAMPLE_ROWS, 1), F32),
                            pltpu.VMEM((SAMPLE_ROWS, hd), F32)]),
        out_shape=jax.ShapeDtypeStruct((bn, 1, dq.shape[1]), F32),
        compiler_params=_cparams(("arbitrary", "arbitrary"), 2 * pg * _nbytes((prow, hd), F32)),
        name="page_attend",
    )(page_table, dq, kvs, bias, bias_self, *([cache_k] * pg), *([cache_v] * pg))


def _rope_tables(pos):
    half = RET_DK // 2
    inv = 1.0 / (ROPE_BASE ** jnp.linspace(0.0, 1.0, half, dtype=F32))
    ang = pos.astype(F32)[:, None] * inv[None, :]
    return jnp.cos(ang), jnp.sin(ang)


def _stacked_bf16(a):
    arr = a.astype(BF16)
    return lambda l: _LayerWeight(arr, l, 0, arr.shape[2])


def _in_proj_weights(w_in):
    depth, d, _ = w_in.shape
    hq = RET_HEADS * RET_DK
    hv = RET_HEADS * RET_DV
    o = 0
    cuts = {}
    for name, width in (('qk', 2 * hq), ('v', hv), ('g', hv)):
        cuts[name] = (o, width)
        o += width
    s5w = (w_in.shape[2] - 2 * hq - 2 * hv - DSA_HEADS * DSA_HEAD_DIM - 2 * DSA_KV_HEADS * DSA_HEAD_DIM
           - IDX_HEADS * IDX_DIM - IDX_DIM - IDX_HEADS) // (1 + 2 * N_BRANCHES)
    for name, width in (('su', s5w), ('dq', DSA_HEADS * DSA_HEAD_DIM), ('kv', 2 * DSA_KV_HEADS * DSA_HEAD_DIM),
                        ('iq', IDX_HEADS * IDX_DIM), ('ikw', IDX_DIM + IDX_HEADS), ('gates', N_BRANCHES * d)):
        cuts[name] = (o, width)
        o += width
    assert o == w_in.shape[2]
    cut = lambda name: w_in[:, :, cuts[name][0]:cuts[name][0] + cuts[name][1]]
    iq = jnp.pad(cut('iq').reshape(depth, d, IDX_HEADS, IDX_DIM), ((0, 0), (0, 0), (0, 0), (0, LANES - IDX_DIM)))
    repacked = {'iq': iq.reshape(depth, d, IDX_HEADS * LANES),
                'ikw': jnp.pad(cut('ikw'), ((0, 0), (0, 0), (0, LANES - IDX_DIM - IDX_HEADS))),
                'gates': cut('gates')}

    def layer(l):
        wb = {k: _LayerWeight(w_in, l, *cuts[k]) for k in ('qk', 'v', 'g', 'su', 'dq', 'kv')}
        wb.update({k: _LayerWeight(a, l, 0, a.shape[2]) for k, a in repacked.items()})
        return wb

    return layer


def _project(xb, wb, sample):
    act = F32 if sample else BF16
    p = {
        'qk': _mm(xb, wb['qk'], (F32,), name="proj_qk"),
        'v': _mm(xb, wb['v'], (act,), name="proj_v"),
        'g': _mm(xb, wb['g'], (F32,), name="proj_g"),
        'su': _mm(xb, wb['su'], None, oct_layout=True, name="proj_su"),
        'dq': _mm(xb, wb['dq'], (act,), name="proj_dq"),
        'iq': _mm(xb, wb['iq'], (act,), name="proj_iq"),
        'gates': _mm(xb, wb['gates'], (F32,), name="proj_gates"),
    }
    p['kv'], p['kvb'] = _mm(xb, wb['kv'], (F32, BF16), name="proj_kv")
    p['ikw'], p['ikb'] = _mm(xb, wb['ikw'], (F32, BF16), name="proj_ikw")
    return p


def _s5_state_out(h, groups):
    bn = h.shape[0]
    h = h.reshape(bn, groups // S5_OCT, 2, S5_OCT, S5_STATE)
    return h[:, :, 0].reshape(bn, groups, S5_STATE), h[:, :, 1].reshape(bn, groups, S5_STATE)


def kernel(x_prompt, x_sample, cache_k, cache_v, cache_idx_k, state_ret, state_s5_re, state_s5_im, page_table, ln1_g, ln1_b, ffn1_wg, ffn1_wu, ffn1_wd, w_in, s5_a_re, s5_a_im, s5_log_dt, s5_b_re, s5_b_im, s5_c_re, s5_c_im, s5_d, w_glu, w_ret_o, w_s5_o, w_dsa_o, w_out, ln2_g, ln2_b, ffn2_wg, ffn2_wu, ffn2_wd, ln3_g, ln3_b):
    bp, lp, d = x_prompt.shape
    bs, ls, _ = x_sample.shape
    depth = w_in.shape[0]
    assert ls == 1 and bs <= SAMPLE_ROWS
    npages = page_table.shape[1]
    past = npages * PAGE_SIZE
    groups = s5_a_re.shape[1]
    alpha = (2 * depth) ** 0.25
    kvw = DSA_KV_HEADS * DSA_HEAD_DIM
    nsel_s = min(DSA_TOPK, (past + ls) // 4)

    cos_p, sin_p = _rope_tables(jnp.arange(lp))
    cos_s, sin_s = _rope_tables(past + jnp.arange(ls))
    cache_k = cache_k.reshape(depth, -1, PAGE_SIZE * DSA_KV_HEADS, DSA_HEAD_DIM)
    cache_v = cache_v.reshape(depth, -1, PAGE_SIZE * DSA_KV_HEADS, DSA_HEAD_DIM)
    cache_idx_t = jnp.swapaxes(cache_idx_k, 2, 3)

    in_proj = _in_proj_weights(w_in.astype(BF16))
    ffn1 = [_stacked_bf16(a) for a in (ffn1_wg, ffn1_wu, ffn1_wd)]
    ffn2 = [_stacked_bf16(a) for a in (ffn2_wg, ffn2_wu, ffn2_wd)]
    mixer_out = [_stacked_bf16(a) for a in (w_glu, w_ret_o, w_s5_o, w_dsa_o, w_out)]

    xp = x_prompt.reshape(bp * lp, d)
    xs = jnp.pad(x_sample.reshape(bs, d), ((0, SAMPLE_ROWS - bs), (0, 0)))
    outs_p, outs_s = [], []
    for l in range(depth):
        wb = in_proj(l)
        tb = _s5_tables(s5_a_re[l], s5_a_im[l], s5_log_dt[l], s5_b_re[l], s5_b_im[l], s5_c_re[l], s5_c_im[l],
                        s5_d[l])
        w1 = [w(l) for w in ffn1]
        w2 = [w(l) for w in ffn2]
        wglu, wro, wso, wdo, wo = [w(l) for w in mixer_out]

        xp, xpb = _ffn_ln(xp, *w1, ln1_g[l], ln1_b[l], alpha)
        p = _project(xpb, wb, sample=False)
        o_ret, ret_p = _retention(p['qk'], p['v'], p['g'], cos_p, sin_p, bp)
        y_s5, h_p = _s5(p['su'], tb, bp)
        z = _glu(y_s5, wglu)
        att = _dsa_prompt(p['dq'], p['iq'], p['ikw'], p['ikb'], p['kvb'], bp)
        merged = _merge(o_ret, z, att, p['gates'], wro, wso, wdo)
        xp, xpb = _out_ln(xp, merged, wo, ln2_g[l], ln2_b[l], alpha)
        xp, xpb = _ffn_ln(xp, *w2, ln3_g[l], ln3_b[l], alpha)
        s5r_p, s5i_p = _s5_state_out(h_p, groups)
        outs_p.append((p['kv'][:, :kvw].reshape(bp, lp, DSA_KV_HEADS, DSA_HEAD_DIM),
                       p['kv'][:, kvw:].reshape(bp, lp, DSA_KV_HEADS, DSA_HEAD_DIM),
                       p['ikw'][:, :IDX_DIM].reshape(bp, lp, IDX_DIM), ret_p, s5r_p, s5i_p))

        xs, xsb = _ffn_ln(xs, *w1, ln1_g[l], ln1_b[l], alpha)
        q = _project(xsb, wb, sample=True)
        o_ret_s, ret_s = _retention_step(q['qk'], q['v'], q['g'], cos_s, sin_s, state_ret[l])
        o_ret_s = jnp.pad(o_ret_s[:, 0], ((0, SAMPLE_ROWS - bs), (0, 0))).astype(BF16)
        h0 = jnp.concatenate([state_s5_re[l].reshape(bs, groups // S5_OCT, S5_OCT * S5_STATE),
                              state_s5_im[l].reshape(bs, groups // S5_OCT, S5_OCT * S5_STATE)], axis=-1)
        h0 = jnp.pad(jnp.swapaxes(h0, 0, 1), ((0, 0), (0, SAMPLE_ROWS - bs), (0, 0)))
        y_s, h_s = _s5_step(q['su'], h0, tb)
        z_s = _glu(y_s, wglu)
        scores = _page_scores(page_table, q['iq'], q['ikw'], cache_idx_t, l)
        bias, bias_self = _page_select(scores, q['iq'], q['ikw'], nsel_s)
        att_s = _page_attend(page_table, q['dq'], q['kv'], bias, bias_self, cache_k, cache_v, l)
        att_s = jnp.pad(att_s[:, 0], ((0, SAMPLE_ROWS - bs), (0, 0))).astype(BF16)
        merged_s = _merge(o_ret_s, z_s, att_s, q['gates'], wro, wso, wdo)
        xs, xsb = _out_ln(xs, merged_s, wo, ln2_g[l], ln2_b[l], alpha)
        xs, xsb = _ffn_ln(xs, *w2, ln3_g[l], ln3_b[l], alpha)
        s5r_s, s5i_s = _s5_state_out(jnp.swapaxes(h_s, 0, 1)[:bs, :, None, :], groups)
        outs_s.append((q['kv'][:bs, :kvw].reshape(bs, ls, DSA_KV_HEADS, DSA_HEAD_DIM),
                       q['kv'][:bs, kvw:].reshape(bs, ls, DSA_KV_HEADS, DSA_HEAD_DIM),
                       q['ikw'][:bs, :IDX_DIM].reshape(bs, ls, IDX_DIM), ret_s, s5r_s, s5i_s))

    k_p, v_p, ik_p, ret_p, s5r_p, s5i_p = [jnp.stack(a) for a in zip(*outs_p)]
    k_s, v_s, ik_s, ret_s, s5r_s, s5i_s = [jnp.stack(a) for a in zip(*outs_s)]
    return (xp.reshape(bp, lp, d), xs[:bs].reshape(bs, ls, d), k_p, v_p, ik_p, k_s, v_s, ik_s,
            ret_p, ret_s, s5r_p, s5i_p, s5r_s, s5i_s)
```

```python
import functools
import math
from typing import NamedTuple

import numpy as np
import jax
import jax.numpy as jnp
from jax import lax
from jax.experimental import pallas as pl
from jax.experimental.pallas import tpu as pltpu

F32 = jnp.float32
BF16 = jnp.bfloat16
I32 = jnp.int32

PAGE_SIZE = 128
RET_HEADS = 4
RET_DK = 256
RET_DV = 512
RET_CHUNK = 128
ROPE_BASE = 10000.0
S5_GROUP = 16
S5_STATE = 64
DSA_HEADS = 8
DSA_KV_HEADS = 2
DSA_HEAD_DIM = 128
IDX_HEADS = 8
IDX_DIM = 64
DSA_TOPK = 256
LN_EPS = 1e-5
GN_EPS = 1e-5
N_BRANCHES = 3

LANES = 128
SUBLANES = 8
VMEM_LIMIT_CAP = 56 * 1024 * 1024
S5_CHUNK = 16
S5_OCT = LANES // S5_GROUP
SAMPLE_ROWS = 16
MASK_NEG = -1e30
INT_MIN = -2 ** 31


def _cparams(semantics, *block_bytes):
    est = 2 * sum(block_bytes) + (8 << 20)
    return pltpu.CompilerParams(dimension_semantics=semantics,
                                vmem_limit_bytes=int(min(max(est, 32 << 20), VMEM_LIMIT_CAP)))


def _nbytes(shape, dtype):
    return int(np.prod(shape)) * jnp.dtype(dtype).itemsize


class _LayerWeight(NamedTuple):
    arr: jax.Array
    layer: int
    col0: int
    ncols: int

    @property
    def shape(self):
        return (self.arr.shape[1], self.ncols)

    def spec(self, blk, idx):
        assert self.col0 % blk[1] == 0
        layer, off = self.layer, self.col0 // blk[1]

        def index_map(*g):
            r, c = idx(*g)
            return (layer, r, c + off)

        return pl.BlockSpec((None,) + tuple(blk), index_map)


def _dot(a, b):
    return jnp.dot(a, b, preferred_element_type=F32)


def _dot_nt(a, b):
    return lax.dot_general(a, b, (((1,), (1,)), ((), ())), preferred_element_type=F32)


def _layernorm(y, g, b):
    mu = jnp.mean(y, axis=-1, keepdims=True)
    yc = y - mu
    var = jnp.mean(yc * yc, axis=-1, keepdims=True)
    return yc * lax.rsqrt(var + LN_EPS) * g + b


def _col_of_row(r):
    n = r.shape[1]
    eye = lax.broadcasted_iota(I32, (n, n), 0) == lax.broadcasted_iota(I32, (n, n), 1)
    return jnp.sum(jnp.where(eye, jnp.broadcast_to(r, (n, n)), 0.0), axis=1, keepdims=True)


def _mm_kernel(x_ref, w_ref, *o_refs, gate):
    acc = _dot(x_ref[...], w_ref[...])
    if gate:
        acc = jax.nn.sigmoid(acc)
    for o_ref in o_refs:
        o_ref[...] = acc.astype(o_ref.dtype)


def _mm_oct_kernel(x_ref, w_ref, o_ref):
    acc = _dot(x_ref[...], w_ref[...])
    for i in range(o_ref.shape[0]):
        o_ref[i] = acc[:, i * LANES:(i + 1) * LANES]


def _mm(x, w, out_dtypes, oct_layout=False, gate=False, name="mm"):
    m, k = x.shape
    n = w.shape[1]
    tm = min(m, 1024)
    tn = n if n <= 1024 else 512
    while w.col0 % tn:
        tn //= 2
    assert m % tm == 0 and n % tn == 0 and tn % LANES == 0
    w_resident = (n // tn) * m * k + k * n <= (m // tm) * k * n + m * k
    if w_resident:
        grid = (n // tn, m // tm)
        xi, wi, oi = (lambda j, i: (i, 0)), (lambda j, i: (0, j)), (lambda j, i: (i, j))
        ooct = lambda j, i: (j, i, 0)
    else:
        grid = (m // tm, n // tn)
        xi, wi, oi = (lambda i, j: (i, 0)), (lambda i, j: (0, j)), (lambda i, j: (i, j))
        ooct = lambda i, j: (j, i, 0)
    in_specs = [pl.BlockSpec((tm, k), xi), w.spec((k, tn), wi)]
    blk = [_nbytes((tm, k), BF16), _nbytes((k, tn), BF16)]
    if oct_layout:
        out_shape = jax.ShapeDtypeStruct((n // LANES, m, LANES), F32)
        out_specs = pl.BlockSpec((tn // LANES, tm, LANES), ooct)
        body = _mm_oct_kernel
        blk.append(_nbytes((tm, tn), F32))
    else:
        out_shape = tuple(jax.ShapeDtypeStruct((m, n), d) for d in out_dtypes)
        out_specs = tuple(pl.BlockSpec((tm, tn), oi) for _ in out_dtypes)
        body = functools.partial(_mm_kernel, gate=gate)
        blk += [_nbytes((tm, tn), d) for d in out_dtypes]
    out = pl.pallas_call(body, grid=grid, in_specs=in_specs, out_specs=out_specs, out_shape=out_shape,
                         compiler_params=_cparams(("arbitrary", "arbitrary"), *blk), name=name)(x, w.arr)
    return out if oct_layout else (out[0] if len(out_dtypes) == 1 else out)


def _ffn_kernel(x_ref, wg_ref, wu_ref, wd_ref, g_ref, b_ref, o_ref, ob_ref, xb_scr, acc_scr, *, alpha):
    f = pl.program_id(1)

    @pl.when(f == 0)
    def _():
        xb_scr[...] = x_ref[...].astype(BF16)
        acc_scr[...] = jnp.zeros_like(acc_scr)

    xb = xb_scr[...]
    hg = _dot(xb, wg_ref[...])
    hu = _dot(xb, wu_ref[...])
    h = hg * jax.nn.sigmoid(hg) * hu
    acc_scr[...] += _dot(h.astype(BF16), wd_ref[...])

    @pl.when(f == pl.num_programs(1) - 1)
    def _():
        y = _layernorm(alpha * x_ref[...] + 0.5 * acc_scr[...], g_ref[...], b_ref[...])
        o_ref[...] = y
        ob_ref[...] = y.astype(BF16)


def _ffn_ln(x, wg, wu, wd, g, b, alpha):
    m, d = x.shape
    dff = wg.shape[1]
    tm = min(m, 512)
    tf = 512
    assert m % tm == 0 and dff % tf == 0
    row = lambda i, f: (i, 0)
    blk = [_nbytes((tm, d), F32), 3 * _nbytes((d, tf), BF16), _nbytes((tm, d), F32), _nbytes((tm, d), BF16),
           _nbytes((tm, d), F32)]
    return pl.pallas_call(
        functools.partial(_ffn_kernel, alpha=alpha),
        grid=(m // tm, dff // tf),
        in_specs=[pl.BlockSpec((tm, d), row),
                  wg.spec((d, tf), lambda i, f: (0, f)),
                  wu.spec((d, tf), lambda i, f: (0, f)),
                  wd.spec((tf, d), lambda i, f: (f, 0)),
                  pl.BlockSpec((1, d), lambda i, f: (0, 0)),
                  pl.BlockSpec((1, d), lambda i, f: (0, 0))],
        out_specs=(pl.BlockSpec((tm, d), row), pl.BlockSpec((tm, d), row)),
        out_shape=(jax.ShapeDtypeStruct((m, d), F32), jax.ShapeDtypeStruct((m, d), BF16)),
        scratch_shapes=[pltpu.VMEM((tm, d), BF16), pltpu.VMEM((tm, d), F32)],
        compiler_params=_cparams(("arbitrary", "arbitrary"), *blk),
        name="ffn_ln",
    )(x, wg.arr, wu.arr, wd.arr, g.reshape(1, d), b.reshape(1, d))


def _glu_kernel(y_ref, w_ref, o_ref):
    y = jnp.concatenate([y_ref[i] for i in range(y_ref.shape[0])], axis=-1)
    o_ref[...] = (y * jax.nn.sigmoid(_dot(y.astype(BF16), w_ref[...]))).astype(o_ref.dtype)


def _glu(y_oct, w):
    no, m, _ = y_oct.shape
    n = w.shape[1]
    tm = min(m, 512)
    blk = [_nbytes((no, tm, LANES), F32), _nbytes(w.shape, BF16), _nbytes((tm, n), BF16)]
    return pl.pallas_call(
        _glu_kernel, grid=(m // tm,),
        in_specs=[pl.BlockSpec((no, tm, LANES), lambda i: (0, i, 0)), w.spec(w.shape, lambda i: (0, 0))],
        out_specs=pl.BlockSpec((tm, n), lambda i: (i, 0)),
        out_shape=jax.ShapeDtypeStruct((m, n), BF16),
        compiler_params=_cparams(("arbitrary",), *blk),
        name="s5_glu",
    )(y_oct, w.arr)


def _merge_kernel(o_ref, z_ref, a_ref, g0_ref, g1_ref, g2_ref, wr_ref, ws_ref, wd_ref, m_ref):
    ret = _dot(o_ref[...], wr_ref[...])
    s5 = _dot(z_ref[...], ws_ref[...])
    dsa = _dot(a_ref[...], wd_ref[...])
    merged = g0_ref[...].astype(F32) * ret + g1_ref[...].astype(F32) * s5 + g2_ref[...].astype(F32) * dsa
    m_ref[...] = merged.astype(m_ref.dtype)


def _merge(o, z, a, gates, wr, ws, wd):
    m = o.shape[0]
    d = wr.shape[1]
    tm = min(m, 512)
    tn = min(d, 512)
    nb = d // tn
    blk = [_nbytes((tm, o.shape[1]), BF16), 2 * _nbytes((tm, z.shape[1]), BF16), 3 * _nbytes((tm, tn), BF16),
           _nbytes((o.shape[1], tn), BF16), 2 * _nbytes((z.shape[1], tn), BF16), _nbytes((tm, tn), BF16)]
    gate_spec = lambda br: pl.BlockSpec((tm, tn), lambda i, j: (i, br * nb + j))
    return pl.pallas_call(
        _merge_kernel, grid=(m // tm, nb),
        in_specs=[pl.BlockSpec((tm, o.shape[1]), lambda i, j: (i, 0)),
                  pl.BlockSpec((tm, z.shape[1]), lambda i, j: (i, 0)),
                  pl.BlockSpec((tm, a.shape[1]), lambda i, j: (i, 0)),
                  gate_spec(0), gate_spec(1), gate_spec(2),
                  wr.spec((wr.shape[0], tn), lambda i, j: (0, j)),
                  ws.spec((ws.shape[0], tn), lambda i, j: (0, j)),
                  wd.spec((wd.shape[0], tn), lambda i, j: (0, j))],
        out_specs=pl.BlockSpec((tm, tn), lambda i, j: (i, j)),
        out_shape=jax.ShapeDtypeStruct((m, d), BF16),
        compiler_params=_cparams(("arbitrary", "arbitrary"), *blk),
        name="branch_merge",
    )(o, z, a, gates, gates, gates, wr.arr, ws.arr, wd.arr)


def _out_ln_kernel(x_ref, m_ref, w_ref, g_ref, b_ref, o_ref, ob_ref, *, alpha):
    y = _layernorm(alpha * x_ref[...] + _dot(m_ref[...], w_ref[...]), g_ref[...], b_ref[...])
    o_ref[...] = y
    ob_ref[...] = y.astype(BF16)


def _out_ln(x, merged, w, g, b, alpha):
    m, d = x.shape
    tm = min(m, 512)
    row = lambda i: (i, 0)
    blk = [2 * _nbytes((tm, d), F32), 2 * _nbytes((tm, d), BF16), _nbytes((d, d), BF16)]
    return pl.pallas_call(
        functools.partial(_out_ln_kernel, alpha=alpha), grid=(m // tm,),
        in_specs=[pl.BlockSpec((tm, d), row), pl.BlockSpec((tm, d), row), w.spec((d, d), lambda i: (0, 0)),
                  pl.BlockSpec((1, d), lambda i: (0, 0)), pl.BlockSpec((1, d), lambda i: (0, 0))],
        out_specs=(pl.BlockSpec((tm, d), row), pl.BlockSpec((tm, d), row)),
        out_shape=(jax.ShapeDtypeStruct((m, d), F32), jax.ShapeDtypeStruct((m, d), BF16)),
        compiler_params=_cparams(("arbitrary",), *blk),
        name="out_ln",
    )(x, merged, w.arr, g.reshape(1, d), b.reshape(1, d))


def _rope(x, cos, sin):
    half = x.shape[-1] // 2
    x1, x2 = x[:, :half], x[:, half:]
    return jnp.concatenate([x1 * cos - x2 * sin, x1 * sin + x2 * cos], axis=-1)


def _group_norm_gate(o, gate):
    mu = jnp.mean(o, axis=-1, keepdims=True)
    oc = o - mu
    var = jnp.mean(oc * oc, axis=-1, keepdims=True)
    return gate * jax.nn.sigmoid(gate) * (oc * lax.rsqrt(var + GN_EPS))


def _ret_kernel(qk_ref, v_ref, g_ref, cos_ref, sin_ref, o_ref, st_ref):
    @pl.when(pl.program_id(1) == 0)
    def _():
        st_ref[...] = jnp.zeros_like(st_ref)

    c = qk_ref.shape[0]
    cos, sin = cos_ref[...], sin_ref[...]
    ri = lax.broadcasted_iota(I32, (c, c), 0).astype(F32)
    ci = lax.broadcasted_iota(I32, (c, c), 1).astype(F32)
    diff = ri - ci
    ti = lax.broadcasted_iota(I32, (c, 1), 0).astype(F32)
    for h in range(RET_HEADS):
        lg = math.log(1.0 - 2.0 ** (-5.0 - h))
        dmask = jnp.where(diff >= 0, jnp.exp(jnp.maximum(diff, 0.0) * lg), 0.0)
        q_dec = jnp.exp((ti + 1.0) * lg)
        k_dec = jnp.exp((c - 1.0 - ti) * lg)
        c_dec = math.exp(c * lg)
        q = _rope(qk_ref[:, h * RET_DK:(h + 1) * RET_DK], cos, sin)
        k = _rope(qk_ref[:, (RET_HEADS + h) * RET_DK:(RET_HEADS + h + 1) * RET_DK], cos, sin) * (RET_DK ** -0.5)
        v = v_ref[:, h * RET_DV:(h + 1) * RET_DV]
        s = st_ref[0, h]
        qb = q.astype(BF16)
        sc = _dot_nt(qb, k.astype(BF16)) * dmask
        o = _dot(sc.astype(BF16), v) + _dot(qb, s.astype(BF16)) * q_dec
        kd_t = jnp.transpose(k * k_dec).astype(BF16)
        st_ref[0, h] = c_dec * s + _dot(kd_t, v)
        o_ref[:, h * RET_DV:(h + 1) * RET_DV] = _group_norm_gate(
            o, g_ref[:, h * RET_DV:(h + 1) * RET_DV]).astype(o_ref.dtype)


def _retention(qk, v, gate, cos, sin, bn):
    m = qk.shape[0]
    length = m // bn
    c = RET_CHUNK
    nc = length // c
    hv = RET_HEADS * RET_DV
    row = lambda b, j: (b * nc + j, 0)
    blk = [_nbytes((c, qk.shape[1]), F32), _nbytes((c, hv), BF16), _nbytes((c, hv), F32), _nbytes((c, hv), BF16),
           _nbytes((RET_HEADS, RET_DK, RET_DV), F32)]
    return pl.pallas_call(
        _ret_kernel, grid=(bn, nc),
        in_specs=[pl.BlockSpec((c, qk.shape[1]), row), pl.BlockSpec((c, hv), row), pl.BlockSpec((c, hv), row),
                  pl.BlockSpec((c, RET_DK // 2), lambda b, j: (j, 0)),
                  pl.BlockSpec((c, RET_DK // 2), lambda b, j: (j, 0))],
        out_specs=(pl.BlockSpec((c, hv), row),
                   pl.BlockSpec((1, RET_HEADS, RET_DK, RET_DV), lambda b, j: (b, 0, 0, 0))),
        out_shape=(jax.ShapeDtypeStruct((m, hv), BF16),
                   jax.ShapeDtypeStruct((bn, RET_HEADS, RET_DK, RET_DV), F32)),
        compiler_params=_cparams(("arbitrary", "arbitrary"), *blk),
        name="retention",
    )(qk, v, gate, cos, sin)


def _ret_step_kernel(qk_ref, v_ref, g_ref, cos_ref, sin_ref, s0_ref, o_ref, st_ref):
    row = pl.ds(pl.program_id(0), 1)
    cos, sin = cos_ref[...], sin_ref[...]
    for h in range(RET_HEADS):
        decay = 1.0 - 2.0 ** (-5.0 - h)
        q = _rope(qk_ref[row, h * RET_DK:(h + 1) * RET_DK], cos, sin)
        k = _rope(qk_ref[row, (RET_HEADS + h) * RET_DK:(RET_HEADS + h + 1) * RET_DK], cos, sin) * (RET_DK ** -0.5)
        v = v_ref[row, h * RET_DV:(h + 1) * RET_DV]
        s0 = s0_ref[0, h]
        st_ref[0, h] = decay * s0 + _col_of_row(k) * v
        o = (jnp.sum(q * k, axis=1, keepdims=True) * v
             + jnp.sum(_col_of_row(q) * s0, axis=0, keepdims=True) * decay)
        o_ref[0, :, h * RET_DV:(h + 1) * RET_DV] = _group_norm_gate(o, g_ref[row, h * RET_DV:(h + 1) * RET_DV])


def _retention_step(qk, v, gate, cos, sin, s0):
    bn = s0.shape[0]
    const = lambda b: (0, 0)
    state = pl.BlockSpec((1,) + s0.shape[1:], lambda b: (b, 0, 0, 0))
    return pl.pallas_call(
        _ret_step_kernel, grid=(bn,),
        in_specs=[pl.BlockSpec(qk.shape, const), pl.BlockSpec(v.shape, const), pl.BlockSpec(gate.shape, const),
                  pl.BlockSpec(cos.shape, const), pl.BlockSpec(sin.shape, const), state],
        out_specs=(pl.BlockSpec((1, 1, v.shape[1]), lambda b: (b, 0, 0)), state),
        out_shape=(jax.ShapeDtypeStruct((bn, 1, v.shape[1]), F32), jax.ShapeDtypeStruct(s0.shape, F32)),
        compiler_params=_cparams(("arbitrary",), 2 * _nbytes(s0.shape[1:], F32)),
        name="retention_step",
    )(qk, v, gate, cos, sin, s0)


def _s5_tables(a_re, a_im, log_dt, b_re, b_im, c_re, c_im, d):
    g, p = a_re.shape
    nc = b_re.shape[-1]
    no = g // S5_OCT
    t = S5_CHUNK
    dt = jnp.exp(log_dt)[:, None]
    mag = jnp.exp(a_re * dt)
    ab_re = mag * jnp.cos(a_im * dt)
    ab_im = mag * jnp.sin(a_im * dt)
    den = a_re * a_re + a_im * a_im
    x_re = ab_re - 1.0
    f_re = (x_re * a_re + ab_im * a_im) / den
    f_im = (ab_im * a_re - x_re * a_im) / den
    bb_re = f_re[..., None] * b_re - f_im[..., None] * b_im
    bb_im = f_re[..., None] * b_im + f_im[..., None] * b_re

    def powers(n):
        n = n.astype(F32)[:, None, None]
        pmag = jnp.exp(n * (a_re * dt)[None])
        return pmag * jnp.cos(n * (a_im * dt)[None]), pmag * jnp.sin(n * (a_im * dt)[None])

    pw_re, pw_im = powers(jnp.arange(t + 1))
    rev_re, rev_im = powers(t - 1 - jnp.arange(t))
    bt_re, bt_im = jnp.swapaxes(bb_re, 1, 2), jnp.swapaxes(bb_im, 1, 2)

    def packed(v_re, v_im):
        out = []
        for v in (v_re, v_im):
            v = v.reshape(v.shape[0], no, S5_OCT, nc, p)
            out.append(jnp.swapaxes(v, 0, 1).reshape(no, v.shape[0] * LANES, p))
        return jnp.concatenate(out, axis=-1)

    def b_packed(power_re, power_im):
        pr, pi = power_re[:, :, None, :], power_im[:, :, None, :]
        return packed(pr * bt_re[None] - pi * bt_im[None], pr * bt_im[None] + pi * bt_re[None])

    def c_packed(power_re, power_im):
        pr, pi = power_re[:, :, None, :], power_im[:, :, None, :]
        return packed(c_re[None] * pr - c_im[None] * pi, -(c_re[None] * pi + c_im[None] * pr))

    def state_row(v_re, v_im):
        return jnp.concatenate([v_re.reshape(no, 1, S5_OCT * p), v_im.reshape(no, 1, S5_OCT * p)], axis=-1)

    d_row = d.reshape(no, 1, LANES)
    return dict(
        b_end=b_packed(rev_re, rev_im),
        c_in=c_packed(pw_re[1:], pw_im[1:]),
        a_row=state_row(pw_re[t], pw_im[t]),
        d_row=jnp.tile(d_row, (1, 1, t)),
        b_one=b_packed(pw_re[:1], pw_im[:1]),
        c_one=c_packed(pw_re[:1], pw_im[:1]),
        a_one=state_row(ab_re, ab_im),
        d_one=d_row,
    )


def _expand_groups(packed_ref, rows):
    n = rows.stop - rows.start
    g_row = (lax.broadcasted_iota(I32, (n, LANES), 0) // S5_GROUP) % S5_OCT
    g_half = lax.broadcasted_iota(I32, (n, LANES), 1) // S5_STATE
    x = packed_ref[0, rows, :]
    swapped = pltpu.roll(x, S5_STATE, axis=1)
    tiles = []
    for src in (jnp.where(g_half == 0, x, swapped), jnp.where(g_half == 0, swapped, x)):
        for pair in range(S5_OCT // 2):
            tiles.append(jnp.where(g_row == 2 * pair + g_half, src, 0.0))
    return jnp.concatenate(tiles, axis=-1)


def _s5_kernel(u_ref, cone_ref, bend_ref, cin_ref, a_ref, d_ref, y_ref, h_ref,
               toep_scr, bfull_scr, cfull_scr, e_scr, s_scr):
    rows, t = e_scr.shape[0], S5_CHUNK

    @pl.when(pl.program_id(1) == 0)
    def _():
        toep_scr[...] = jnp.zeros_like(toep_scr)
        same_group = (lax.broadcasted_iota(I32, (LANES, LANES), 0) // S5_GROUP
                      == lax.broadcasted_iota(I32, (LANES, LANES), 1) // S5_GROUP)
        for s in range(t):
            sl = slice(s * LANES, (s + 1) * LANES)
            bfull_scr[sl, :] = _expand_groups(bend_ref, sl).astype(BF16)
            cfull_scr[sl, :] = _expand_groups(cin_ref, sl).astype(BF16)
            lag = t - 1 - s
            blk = lax.dot_general(bend_ref[0, sl, :], cone_ref[0], (((1,), (1,)), ((), ())),
                                  precision=lax.Precision.HIGHEST, preferred_element_type=F32)
            blk = jnp.where(same_group, blk, 0.0).astype(BF16)
            for r in range(t - lag):
                toep_scr[r * LANES:(r + 1) * LANES, (r + lag) * LANES:(r + lag + 1) * LANES] = blk

    u = jnp.concatenate([u_ref[0, pl.ds(i, rows, stride=t), :] for i in range(t)], axis=-1)
    ub = u.astype(BF16)
    e_scr[...] = _dot(ub, bfull_scr[...])
    half = a_ref.shape[-1] // 2
    a_re, a_im = a_ref[0, :, :half], a_ref[0, :, half:]

    def step(k, carry):
        s_re, s_im = carry
        s_scr[pl.ds(k, 1), :] = jnp.concatenate([s_re, s_im], axis=-1)
        e = e_scr[pl.ds(k, 1), :]
        return (a_re * s_re - a_im * s_im + e[:, :half], a_re * s_im + a_im * s_re + e[:, half:])

    zero = jnp.zeros((1, half), F32)
    s_re, s_im = lax.fori_loop(0, rows, step, (zero, zero))
    h_ref[0, 0] = jnp.concatenate([s_re, s_im], axis=-1)
    y = jax.nn.gelu(_dot(ub, toep_scr[...]) + _dot_nt(s_scr[...].astype(BF16), cfull_scr[...]) + d_ref[0] * u)
    for i in range(t):
        y_ref[0, pl.ds(i, rows, stride=t), :] = y[:, i * LANES:(i + 1) * LANES]


def _s5(u_oct, tb, bn):
    no, m, _ = u_oct.shape
    t = S5_CHUNK
    steps = m // bn
    rows = steps // t
    w = t * LANES
    ns = tb['a_row'].shape[-1]
    pk = tb['b_end'].shape[-1]
    oct_blk = lambda shape: pl.BlockSpec((1,) + shape, lambda o, b: (o,) + (0,) * len(shape))
    blk = [2 * _nbytes((rows, w), F32), 2 * _nbytes((w, pk), F32),
           (_nbytes((w, w), BF16) + 2 * _nbytes((w, ns), BF16) + 2 * _nbytes((rows, ns), F32)) // 2]
    return pl.pallas_call(
        _s5_kernel, grid=(no, bn),
        in_specs=[pl.BlockSpec((1, steps, LANES), lambda o, b: (o, b, 0)),
                  oct_blk((LANES, pk)), oct_blk((w, pk)), oct_blk((w, pk)), oct_blk((1, ns)),
                  oct_blk((1, w))],
        out_specs=(pl.BlockSpec((1, steps, LANES), lambda o, b: (o, b, 0)),
                   pl.BlockSpec((1, 1, 1, ns), lambda o, b: (b, o, 0, 0))),
        out_shape=(jax.ShapeDtypeStruct(u_oct.shape, F32), jax.ShapeDtypeStruct((bn, no, 1, ns), F32)),
        scratch_shapes=[pltpu.VMEM((w, w), BF16), pltpu.VMEM((w, ns), BF16), pltpu.VMEM((w, ns), BF16),
                        pltpu.VMEM((rows, ns), F32), pltpu.VMEM((rows, ns), F32)],
        compiler_params=_cparams(("arbitrary", "arbitrary"), *blk),
        name="s5_scan",
    )(u_oct, tb['c_one'], tb['b_end'], tb['c_in'], tb['a_row'], tb['d_row'])


def _s5_step_kernel(u_ref, h0_ref, b_ref, c_ref, a_ref, d_ref, y_ref, h_ref):
    u = u_ref[0]
    half = a_ref.shape[-1] // 2
    a_re, a_im = a_ref[0, :, :half], a_ref[0, :, half:]
    h0 = h0_ref[0]
    h0_re, h0_im = h0[:, :half], h0[:, half:]
    whole = slice(0, LANES)
    bu = _dot(u.astype(BF16), _expand_groups(b_ref, whole).astype(BF16))
    h_re = a_re * h0_re - a_im * h0_im + bu[:, :half]
    h_im = a_re * h0_im + a_im * h0_re + bu[:, half:]
    h = jnp.concatenate([h_re, h_im], axis=-1)
    h_ref[0] = h
    y_ref[0] = jax.nn.gelu(_dot_nt(h.astype(BF16), _expand_groups(c_ref, whole).astype(BF16)) + d_ref[0] * u)


def _s5_step(u_oct, h0, tb):
    no, rows, _ = u_oct.shape
    ns = h0.shape[-1]
    pk = tb['b_one'].shape[-1]
    o3 = lambda shape: pl.BlockSpec((1,) + shape, lambda o: (o, 0, 0))
    return pl.pallas_call(
        _s5_step_kernel, grid=(no,),
        in_specs=[o3((rows, LANES)), o3((rows, ns)), o3((LANES, pk)), o3((LANES, pk)), o3((1, ns)), o3((1, LANES))],
        out_specs=(o3((rows, LANES)), o3((rows, ns))),
        out_shape=(jax.ShapeDtypeStruct(u_oct.shape, F32), jax.ShapeDtypeStruct(h0.shape, F32)),
        compiler_params=_cparams(("arbitrary",), _nbytes((LANES, ns), BF16) * 2),
        name="s5_step",
    )(u_oct, h0, tb['b_one'], tb['c_one'], tb['a_one'], tb['d_one'])


def _sort_key(x):
    bits = pltpu.bitcast(x, I32)
    return jnp.where(bits >= 0, bits, bits ^ jnp.int32(0x7FFFFFFF))


def _kth_largest_key(count_ge, nsel, shape):
    res = jnp.where(count_ge(jnp.zeros(shape, I32)) >= nsel, jnp.int32(0), jnp.int32(INT_MIN))

    def bit_step(i, res):
        cand = res | jnp.left_shift(jnp.int32(1), 30 - i)
        return jnp.where(count_ge(cand) >= nsel, cand, res)

    return lax.fori_loop(0, 31, bit_step, res)


def _dsa_kernel(dq_ref, iq_ref, ikw_ref, ikb_ref, kv_ref, o_ref, key_scr, bias_scr, m_scr, l_scr, acc_scr,
                *, nsel):
    tq = dq_ref.shape[0]
    jq = pl.program_id(1)
    nkb = jq + 1
    krow = lax.broadcasted_iota(I32, (tq, tq), 0)
    qcol = lax.broadcasted_iota(I32, (tq, tq), 1)
    qpos = jq * tq + qcol
    hd = DSA_HEAD_DIM
    gsz = DSA_HEADS // DSA_KV_HEADS

    def over_keys(x, op):
        part = op(x.reshape(x.shape[0] // SUBLANES, SUBLANES, x.shape[1]), axis=0)
        return op(part, axis=0, keepdims=True)

    iq_st = jnp.concatenate([iq_ref[:, h * LANES:(h + 1) * LANES] for h in range(IDX_HEADS)], axis=0)
    w_t = jnp.transpose(ikw_ref[...])

    def score_block(kb, carry):
        ik = ikb_ref[pl.ds(pl.multiple_of(kb * tq, tq), tq), :]
        sh = jnp.maximum(_dot_nt(ik, iq_st), 0.0)
        acc = jnp.zeros((tq, tq), F32)
        for h in range(IDX_HEADS):
            acc = acc + sh[:, h * tq:(h + 1) * tq] * w_t[IDX_DIM + h:IDX_DIM + h + 1, :]
        acc = jnp.where(kb * tq + krow <= qpos, acc, -jnp.inf)
        key_scr[kb] = _sort_key(acc)
        return carry

    lax.fori_loop(0, nkb, score_block, 0)

    acc_rows = 4 * SUBLANES

    def count(pred):
        def body(kb, c):
            hit = jnp.where(pred(key_scr[kb]), 1.0, 0.0)
            return c + jnp.sum(hit.reshape(tq // acc_rows, acc_rows, tq), axis=0)
        part = lax.fori_loop(0, nkb, body, jnp.zeros((acc_rows, tq), F32))
        return jnp.sum(part, axis=0, keepdims=True)

    kth = _kth_largest_key(lambda cand: count(lambda key: key >= cand), float(nsel), (1, tq))
    at_least = count(lambda key: key >= kth)

    def select_all_ge():
        def block(kb, carry):
            sel = (key_scr[kb] >= kth) & (kb * tq + krow <= qpos)
            bias_scr[kb] = jnp.where(sel, 0.0, MASK_NEG)
            return carry
        lax.fori_loop(0, nkb, block, 0)

    def select_with_ties():
        need = float(nsel) - count(lambda key: key > kth)
        tri = jnp.where(krow >= qcol, 1.0, 0.0).astype(BF16)

        def block(kb, taken):
            key = key_scr[kb]
            tie = jnp.where(key == kth, 1.0, 0.0)
            rank = taken + _dot(tri, tie.astype(BF16))
            sel = ((key > kth) | ((tie > 0.0) & (rank <= need))) & (kb * tq + krow <= qpos)
            bias_scr[kb] = jnp.where(sel, 0.0, MASK_NEG)
            return taken + over_keys(tie, jnp.sum)

        lax.fori_loop(0, nkb, block, jnp.zeros((1, tq), F32))

    lax.cond(jnp.max(at_least) > float(nsel), select_with_ties, select_all_ge)

    scale2 = (hd ** -0.5) * math.log2(math.e)
    for g in range(DSA_KV_HEADS):
        qg = jnp.concatenate([dq_ref[:, (g * gsz + i) * hd:(g * gsz + i + 1) * hd] for i in range(gsz)], axis=0)
        m_scr[...] = jnp.full_like(m_scr, -jnp.inf)
        l_scr[...] = jnp.zeros_like(l_scr)
        acc_scr[...] = jnp.zeros_like(acc_scr)

        def attend_block(kb, carry):
            ks = pl.ds(pl.multiple_of(kb * tq, tq), tq)
            kblk = kv_ref[ks, g * hd:(g + 1) * hd]
            vblk = kv_ref[ks, (DSA_KV_HEADS + g) * hd:(DSA_KV_HEADS + g + 1) * hd]
            bias = bias_scr[kb]
            lg = _dot_nt(kblk, qg) * scale2 + jnp.concatenate([bias] * gsz, axis=1)
            m_old = m_scr[...]
            m_new = jnp.maximum(m_old, over_keys(lg, jnp.max))
            alpha = jnp.exp2(m_old - m_new)
            p = jnp.exp2(lg - m_new)
            l_scr[...] = alpha * l_scr[...] + over_keys(p, jnp.sum)
            v_t = jnp.transpose(vblk.astype(F32)).astype(BF16)
            acc_scr[...] = alpha * acc_scr[...] + _dot(v_t, p.astype(BF16))
            m_scr[...] = m_new
            return carry

        lax.fori_loop(0, nkb, attend_block, 0)
        out_t = acc_scr[...] / l_scr[...]
        for i in range(gsz):
            o_ref[:, (g * gsz + i) * hd:(g * gsz + i + 1) * hd] = jnp.transpose(
                out_t[:, i * tq:(i + 1) * tq]).astype(o_ref.dtype)


def _dsa_prompt(dq, iq, ikw, ikb, kvb, bn):
    m = dq.shape[0]
    length = m // bn
    tq = 256
    nq = length // tq
    nsel = min(DSA_TOPK, length // 4)
    qrow = lambda b, j: (b * nq + j, 0)
    full = lambda b, j: (b, 0)
    gsz = DSA_HEADS // DSA_KV_HEADS
    blk = [_nbytes((tq, dq.shape[1]), BF16) * 3, _nbytes((length, LANES), BF16), _nbytes((length, kvb.shape[1]), BF16),
           _nbytes((nq, tq, tq), F32)]
    return pl.pallas_call(
        functools.partial(_dsa_kernel, nsel=nsel), grid=(bn, nq),
        in_specs=[pl.BlockSpec((tq, dq.shape[1]), qrow), pl.BlockSpec((tq, iq.shape[1]), qrow),
                  pl.BlockSpec((tq, LANES), qrow), pl.BlockSpec((length, LANES), full),
                  pl.BlockSpec((length, kvb.shape[1]), full)],
        out_specs=pl.BlockSpec((tq, dq.shape[1]), qrow),
        out_shape=jax.ShapeDtypeStruct(dq.shape, BF16),
        scratch_shapes=[pltpu.VMEM((nq, tq, tq), I32), pltpu.VMEM((nq, tq, tq), F32),
                        pltpu.VMEM((1, gsz * tq), F32), pltpu.VMEM((1, gsz * tq), F32),
                        pltpu.VMEM((DSA_HEAD_DIM, gsz * tq), F32)],
        compiler_params=_cparams(("arbitrary", "arbitrary"), *blk),
        name="dsa_prompt",
    )(dq, iq, ikw, ikb, kvb)


MAX_PAGES_PER_STEP = 16


def _idx_heads(iq_row):
    return jnp.concatenate([iq_row[:, h * LANES:h * LANES + IDX_DIM] for h in range(IDX_HEADS)], axis=0)


def _page_score_kernel(pt_ref, iq_ref, ikw_ref, *rest):
    page_refs, o_ref = rest[:-1], rest[-1]
    b = pl.program_id(0)
    iq_h = jnp.concatenate([_idx_heads(iq_ref[pl.ds(b, 1), :]),
                            jnp.zeros((SAMPLE_ROWS - IDX_HEADS, IDX_DIM), F32)], axis=0).astype(BF16)
    w_col = _col_of_row(ikw_ref[pl.ds(b, 1), IDX_DIM:IDX_DIM + SAMPLE_ROWS])
    pages = jnp.concatenate([page[...].astype(BF16) for page in page_refs], axis=1)
    sh = jnp.maximum(_dot(iq_h, pages), 0.0)
    sc = jnp.sum(sh * w_col, axis=0, keepdims=True)
    for i in range(len(page_refs)):
        o_ref[0, i:i + 1, :] = sc[:, i * PAGE_SIZE:(i + 1) * PAGE_SIZE]


def _page_scores(page_table, iq, ikw, cache_idx_t, layer):
    bn, npages = page_table.shape
    pg = min(MAX_PAGES_PER_STEP, npages)
    const = lambda b, s, pt: (0, 0)

    def page_spec(i):
        return pl.BlockSpec((None, None, IDX_DIM, PAGE_SIZE), lambda b, s, pt: (layer, pt[b, s * pg + i], 0, 0))

    return pl.pallas_call(
        _page_score_kernel,
        grid_spec=pltpu.PrefetchScalarGridSpec(
            num_scalar_prefetch=1, grid=(bn, npages // pg),
            in_specs=[pl.BlockSpec(iq.shape, const), pl.BlockSpec(ikw.shape, const)]
            + [page_spec(i) for i in range(pg)],
            out_specs=pl.BlockSpec((1, pg, PAGE_SIZE), lambda b, s, pt: (b, s, 0))),
        out_shape=jax.ShapeDtypeStruct((bn, npages, PAGE_SIZE), F32),
        compiler_params=_cparams(("arbitrary", "arbitrary")),
        name="page_scores",
    )(page_table, iq, ikw, *([cache_idx_t] * pg))


def _page_select_kernel(sc_ref, iq_ref, ikw_ref, bias_ref, bias_self_ref, *, nsel):
    b = pl.program_id(0)
    npages, psz = sc_ref.shape[1:]
    iq_h = _idx_heads(iq_ref[pl.ds(b, 1), :])
    ikw = ikw_ref[pl.ds(b, 1), :]
    w_col = _col_of_row(ikw[:, IDX_DIM:IDX_DIM + IDX_HEADS])
    s_self = jnp.sum(jnp.maximum(jnp.sum(iq_h * ikw[:, :IDX_DIM], axis=1, keepdims=True), 0.0) * w_col,
                     axis=0, keepdims=True)
    key = _sort_key(sc_ref[0])
    key_self = _sort_key(s_self)

    def total(x):
        return jnp.sum(jnp.sum(x, axis=1, keepdims=True), axis=0, keepdims=True)

    def count_ge(cand):
        return total(jnp.where(key >= cand, 1.0, 0.0)) + jnp.where(key_self >= cand, 1.0, 0.0)

    kth = _kth_largest_key(count_ge, float(nsel), (1, 1))
    need = float(nsel) - (total(jnp.where(key > kth, 1.0, 0.0)) + jnp.where(key_self > kth, 1.0, 0.0))
    tie = jnp.where(key == kth, 1.0, 0.0)
    r_in = lax.broadcasted_iota(I32, (psz, psz), 0)
    c_in = lax.broadcasted_iota(I32, (psz, psz), 1)
    in_page = _dot(tie.astype(BF16), jnp.where(r_in <= c_in, 1.0, 0.0).astype(BF16))
    per_page = jnp.broadcast_to(jnp.sum(tie, axis=1, keepdims=True), (npages, psz)).astype(BF16)
    r_pg = lax.broadcasted_iota(I32, (npages, npages), 0)
    c_pg = lax.broadcasted_iota(I32, (npages, npages), 1)
    before = _dot(jnp.where(c_pg < r_pg, 1.0, 0.0).astype(BF16), per_page)
    sel = (key > kth) | ((tie > 0.0) & (before + in_page <= need))
    r_dup = lax.broadcasted_iota(I32, (psz, bias_ref.shape[2]), 0)
    c_dup = lax.broadcasted_iota(I32, (psz, bias_ref.shape[2]), 1)
    dup = jnp.where(c_dup // DSA_KV_HEADS == r_dup, 1.0, 0.0).astype(BF16)
    spread = _dot(jnp.where(sel, 1.0, 0.0).astype(BF16), dup)
    bias_ref[0] = jnp.where(spread > 0.5, 0.0, MASK_NEG)
    sel_self = (key_self > kth) | ((key_self == kth) & (total(tie) + 1.0 <= need))
    bias_self_ref[0] = jnp.broadcast_to(jnp.where(sel_self, 0.0, MASK_NEG), (1, LANES))


def _page_select(scores, iq, ikw, nsel):
    bn, npages, psz = scores.shape
    const = lambda b: (0, 0)
    wide = DSA_KV_HEADS * psz
    return pl.pallas_call(
        functools.partial(_page_select_kernel, nsel=nsel), grid=(bn,),
        in_specs=[pl.BlockSpec((1, npages, psz), lambda b: (b, 0, 0)),
                  pl.BlockSpec(iq.shape, const), pl.BlockSpec(ikw.shape, const)],
        out_specs=(pl.BlockSpec((1, npages, wide), lambda b: (b, 0, 0)),
                   pl.BlockSpec((1, 1, LANES), lambda b: (b, 0, 0))),
        out_shape=(jax.ShapeDtypeStruct((bn, npages, wide), F32), jax.ShapeDtypeStruct((bn, 1, LANES), F32)),
        compiler_params=_cparams(("arbitrary",)),
        name="page_select",
    )(scores, iq, ikw)


def _page_attend_kernel(pt_ref, dq_ref, kvs_ref, bias_ref, bself_ref, *rest, npg):
    k_refs, v_refs = rest[:npg], rest[npg:2 * npg]
    o_ref, m_scr, l_scr, acc_scr = rest[2 * npg:]
    b = pl.program_id(0)
    s = pl.program_id(1)
    hd = DSA_HEAD_DIM
    gsz = DSA_HEADS // DSA_KV_HEADS
    scale = hd ** -0.5
    prow = k_refs[0].shape[0]
    dq_row = dq_ref[pl.ds(b, 1), :]
    q = jnp.concatenate([dq_row[:, h * hd:(h + 1) * hd] for h in range(DSA_HEADS)]
                        + [jnp.zeros((SAMPLE_ROWS - DSA_HEADS, hd), F32)], axis=0)
    kv_of_head = lax.broadcasted_iota(I32, (SAMPLE_ROWS, 1), 0) // gsz

    @pl.when(s == 0)
    def _():
        m_scr[...] = jnp.full_like(m_scr, -jnp.inf)
        l_scr[...] = jnp.zeros_like(l_scr)
        acc_scr[...] = jnp.zeros_like(acc_scr)

    def update(lg, pv_of):
        m_old = m_scr[...]
        m_new = jnp.maximum(m_old, jnp.max(lg, axis=1, keepdims=True))
        alpha = jnp.exp(m_old - m_new)
        p = jnp.exp(lg - m_new)
        l_scr[...] = alpha * l_scr[...] + jnp.sum(p, axis=1, keepdims=True)
        acc_scr[...] = alpha * acc_scr[...] + pv_of(p)
        m_scr[...] = m_new

    qb = q.astype(BF16)
    lg = jnp.concatenate([_dot_nt(qb, k_refs[i][...].astype(BF16)) for i in range(npg)], axis=1) * scale
    bias = jnp.concatenate([bias_ref[0, i:i + 1, :] for i in range(npg)], axis=1)
    col = lax.broadcasted_iota(I32, lg.shape, 1)
    lg = jnp.where(jnp.bitwise_and(col, DSA_KV_HEADS - 1) == kv_of_head, lg + bias, MASK_NEG)

    def pv_pages(p):
        pb = p.astype(BF16)
        acc = jnp.zeros((SAMPLE_ROWS, hd), F32)
        for i in range(npg):
            acc = acc + _dot(pb[:, i * prow:(i + 1) * prow], v_refs[i][...].astype(BF16))
        return acc

    update(lg, pv_pages)

    @pl.when(s == pl.num_programs(1) - 1)
    def _():
        kvs = kvs_ref[pl.ds(b, 1), :]
        k_self = jnp.where(kv_of_head == 0, kvs[:, :hd], kvs[:, hd:2 * hd])
        v_self = jnp.where(kv_of_head == 0, kvs[:, 2 * hd:3 * hd], kvs[:, 3 * hd:])
        lg_self = jnp.sum(q * k_self, axis=1, keepdims=True) * scale + bself_ref[0, :, :1]
        update(lg_self, lambda p: p * v_self)
        out = acc_scr[...] / l_scr[...]
        for h in range(DSA_HEADS):
            o_ref[0, :, h * hd:(h + 1) * hd] = out[h:h + 1]


def _page_attend(page_table, dq, kvs, bias, bias_self, cache_k, cache_v, layer):
    assert DSA_KV_HEADS == 2
    bn, npages = page_table.shape
    pg = min(MAX_PAGES_PER_STEP, npages)
    prow, hd = cache_k.shape[-2:]
    const = lambda b, s, pt: (0, 0)

    def page_spec(i):
        return pl.BlockSpec((None, None, prow, hd), lambda b, s, pt: (layer, pt[b, s * pg + i], 0, 0))

    return pl.pallas_call(
        functools.partial(_page_attend_kernel, npg=pg),
        grid_spec=pltpu.PrefetchScalarGridSpec(
            num_scalar_prefetch=1, grid=(bn, npages // pg),
            in_specs=[pl.BlockSpec(dq.shape, const), pl.BlockSpec(kvs.shape, const),
                      pl.BlockSpec((1, pg, prow), lambda b, s, pt: (b, s, 0)),
                      pl.BlockSpec((1, 1, LANES), lambda b, s, pt: (b, 0, 0))]
            + [page_spec(i) for i in range(pg)] + [page_spec(i) for i in range(pg)],
            out_specs=pl.BlockSpec((1, 1, dq.shape[1]), lambda b, s, pt: (b, 0, 0)),
            scratch_shapes=[pltpu.VMEM((SAMPLE_ROWS, 1), F32), pltpu.VMEM((SAMPLE_ROWS, 1), F32),
                            pltpu.VMEM((SAMPLE_ROWS, hd), F32)]),
        out_shape=jax.ShapeDtypeStruct((bn, 1, dq.shape[1]), F32),
        compiler_params=_cparams(("arbitrary", "arbitrary"), 2 * pg * _nbytes((prow, hd), F32)),
        name="page_attend",
    )(page_table, dq, kvs, bias, bias_self, *([cache_k] * pg), *([cache_v] * pg))


def _rope_tables(pos):
    half = RET_DK // 2
    inv = 1.0 / (ROPE_BASE ** jnp.linspace(0.0, 1.0, half, dtype=F32))
    ang = pos.astype(F32)[:, None] * inv[None, :]
    return jnp.cos(ang), jnp.sin(ang)


def _stacked_bf16(a):
    arr = a.astype(BF16)
    return lambda l: _LayerWeight(arr, l, 0, arr.shape[2])


def _in_proj_weights(w_in):
    depth, d, _ = w_in.shape
    hq = RET_HEADS * RET_DK
    hv = RET_HEADS * RET_DV
    o = 0
    cuts = {}
    for name, width in (('qk', 2 * hq), ('v', hv), ('g', hv)):
        cuts[name] = (o, width)
        o += width
    s5w = (w_in.shape[2] - 2 * hq - 2 * hv - DSA_HEADS * DSA_HEAD_DIM - 2 * DSA_KV_HEADS * DSA_HEAD_DIM
           - IDX_HEADS * IDX_DIM - IDX_DIM - IDX_HEADS) // (1 + 2 * N_BRANCHES)
    for name, width in (('su', s5w), ('dq', DSA_HEADS * DSA_HEAD_DIM), ('kv', 2 * DSA_KV_HEADS * DSA_HEAD_DIM),
                        ('iq', IDX_HEADS * IDX_DIM), ('ikw', IDX_DIM + IDX_HEADS), ('gates', N_BRANCHES * d)):
        cuts[name] = (o, width)
        o += width
    assert o == w_in.shape[2]
    cut = lambda name: w_in[:, :, cuts[name][0]:cuts[name][0] + cuts[name][1]]
    iq = jnp.pad(cut('iq').reshape(depth, d, IDX_HEADS, IDX_DIM), ((0, 0), (0, 0), (0, 0), (0, LANES - IDX_DIM)))
    repacked = {'iq': iq.reshape(depth, d, IDX_HEADS * LANES),
                'ikw': jnp.pad(cut('ikw'), ((0, 0), (0, 0), (0, LANES - IDX_DIM - IDX_HEADS))),
                'gates': cut('gates')}

    def layer(l):
        wb = {k: _LayerWeight(w_in, l, *cuts[k]) for k in ('qk', 'v', 'g', 'su', 'dq', 'kv')}
        wb.update({k: _LayerWeight(a, l, 0, a.shape[2]) for k, a in repacked.items()})
        return wb

    return layer


def _project(xb, wb, sample):
    act = F32 if sample else BF16
    p = {
        'qk': _mm(xb, wb['qk'], (F32,), name="proj_qk"),
        'v': _mm(xb, wb['v'], (act,), name="proj_v"),
        'g': _mm(xb, wb['g'], (F32,), name="proj_g"),
        'su': _mm(xb, wb['su'], None, oct_layout=True, name="proj_su"),
        'dq': _mm(xb, wb['dq'], (act,), name="proj_dq"),
        'iq': _mm(xb, wb['iq'], (act,), name="proj_iq"),
        'gates': _mm(xb, wb['gates'], (BF16,), gate=True, name="proj_gates"),
    }
    p['kv'], p['kvb'] = _mm(xb, wb['kv'], (F32, BF16), name="proj_kv")
    p['ikw'], p['ikb'] = _mm(xb, wb['ikw'], (F32, BF16), name="proj_ikw")
    return p


def _s5_state_out(h, groups):
    bn = h.shape[0]
    h = h.reshape(bn, groups // S5_OCT, 2, S5_OCT, S5_STATE)
    return h[:, :, 0].reshape(bn, groups, S5_STATE), h[:, :, 1].reshape(bn, groups, S5_STATE)


def kernel(x_prompt, x_sample, cache_k, cache_v, cache_idx_k, state_ret, state_s5_re, state_s5_im, page_table, ln1_g, ln1_b, ffn1_wg, ffn1_wu, ffn1_wd, w_in, s5_a_re, s5_a_im, s5_log_dt, s5_b_re, s5_b_im, s5_c_re, s5_c_im, s5_d, w_glu, w_ret_o, w_s5_o, w_dsa_o, w_out, ln2_g, ln2_b, ffn2_wg, ffn2_wu, ffn2_wd, ln3_g, ln3_b):
    bp, lp, d = x_prompt.shape
    bs, ls, _ = x_sample.shape
    depth = w_in.shape[0]
    assert ls == 1 and bs <= SAMPLE_ROWS
    npages = page_table.shape[1]
    past = npages * PAGE_SIZE
    groups = s5_a_re.shape[1]
    alpha = (2 * depth) ** 0.25
    kvw = DSA_KV_HEADS * DSA_HEAD_DIM
    nsel_s = min(DSA_TOPK, (past + ls) // 4)

    cos_p, sin_p = _rope_tables(jnp.arange(lp))
    cos_s, sin_s = _rope_tables(past + jnp.arange(ls))
    cache_k = cache_k.reshape(depth, -1, PAGE_SIZE * DSA_KV_HEADS, DSA_HEAD_DIM)
    cache_v = cache_v.reshape(depth, -1, PAGE_SIZE * DSA_KV_HEADS, DSA_HEAD_DIM)
    cache_idx_t = jnp.swapaxes(cache_idx_k, 2, 3)

    in_proj = _in_proj_weights(w_in.astype(BF16))
    ffn1 = [_stacked_bf16(a) for a in (ffn1_wg, ffn1_wu, ffn1_wd)]
    ffn2 = [_stacked_bf16(a) for a in (ffn2_wg, ffn2_wu, ffn2_wd)]
    mixer_out = [_stacked_bf16(a) for a in (w_glu, w_ret_o, w_s5_o, w_dsa_o, w_out)]

    xp = x_prompt.reshape(bp * lp, d)
    xs = jnp.pad(x_sample.reshape(bs, d), ((0, SAMPLE_ROWS - bs), (0, 0)))
    outs_p, outs_s = [], []
    for l in range(depth):
        wb = in_proj(l)
        tb = _s5_tables(s5_a_re[l], s5_a_im[l], s5_log_dt[l], s5_b_re[l], s5_b_im[l], s5_c_re[l], s5_c_im[l],
                        s5_d[l])
        w1 = [w(l) for w in ffn1]
        w2 = [w(l) for w in ffn2]
        wglu, wro, wso, wdo, wo = [w(l) for w in mixer_out]

        xp, xpb = _ffn_ln(xp, *w1, ln1_g[l], ln1_b[l], alpha)
        p = _project(xpb, wb, sample=False)
        o_ret, ret_p = _retention(p['qk'], p['v'], p['g'], cos_p, sin_p, bp)
        y_s5, h_p = _s5(p['su'], tb, bp)
        z = _glu(y_s5, wglu)
        att = _dsa_prompt(p['dq'], p['iq'], p['ikw'], p['ikb'], p['kvb'], bp)
        merged = _merge(o_ret, z, att, p['gates'], wro, wso, wdo)
        xp, xpb = _out_ln(xp, merged, wo, ln2_g[l], ln2_b[l], alpha)
        xp, xpb = _ffn_ln(xp, *w2, ln3_g[l], ln3_b[l], alpha)
        s5r_p, s5i_p = _s5_state_out(h_p, groups)
        outs_p.append((p['kv'][:, :kvw].reshape(bp, lp, DSA_KV_HEADS, DSA_HEAD_DIM),
                       p['kv'][:, kvw:].reshape(bp, lp, DSA_KV_HEADS, DSA_HEAD_DIM),
                       p['ikw'][:, :IDX_DIM].reshape(bp, lp, IDX_DIM), ret_p, s5r_p, s5i_p))

        xs, xsb = _ffn_ln(xs, *w1, ln1_g[l], ln1_b[l], alpha)
        q = _project(xsb, wb, sample=True)
        o_ret_s, ret_s = _retention_step(q['qk'], q['v'], q['g'], cos_s, sin_s, state_ret[l])
        o_ret_s = jnp.pad(o_ret_s[:, 0], ((0, SAMPLE_ROWS - bs), (0, 0))).astype(BF16)
        h0 = jnp.concatenate([state_s5_re[l].reshape(bs, groups // S5_OCT, S5_OCT * S5_STATE),
                              state_s5_im[l].reshape(bs, groups // S5_OCT, S5_OCT * S5_STATE)], axis=-1)
        h0 = jnp.pad(jnp.swapaxes(h0, 0, 1), ((0, 0), (0, SAMPLE_ROWS - bs), (0, 0)))
        y_s, h_s = _s5_step(q['su'], h0, tb)
        z_s = _glu(y_s, wglu)
        scores = _page_scores(page_table, q['iq'], q['ikw'], cache_idx_t, l)
        bias, bias_self = _page_select(scores, q['iq'], q['ikw'], nsel_s)
        att_s = _page_attend(page_table, q['dq'], q['kv'], bias, bias_self, cache_k, cache_v, l)
        att_s = jnp.pad(att_s[:, 0], ((0, SAMPLE_ROWS - bs), (0, 0))).astype(BF16)
        merged_s = _merge(o_ret_s, z_s, att_s, q['gates'], wro, wso, wdo)
        xs, xsb = _out_ln(xs, merged_s, wo, ln2_g[l], ln2_b[l], alpha)
        xs, xsb = _ffn_ln(xs, *w2, ln3_g[l], ln3_b[l], alpha)
        s5r_s, s5i_s = _s5_state_out(jnp.swapaxes(h_s, 0, 1)[:bs, :, None, :], groups)
        outs_s.append((q['kv'][:bs, :kvw].reshape(bs, ls, DSA_KV_HEADS, DSA_HEAD_DIM),
                       q['kv'][:bs, kvw:].reshape(bs, ls, DSA_KV_HEADS, DSA_HEAD_DIM),
                       q['ikw'][:bs, :IDX_DIM].reshape(bs, ls, IDX_DIM), ret_s, s5r_s, s5i_s))

    k_p, v_p, ik_p, ret_p, s5r_p, s5i_p = [jnp.stack(a) for a in zip(*outs_p)]
    k_s, v_s, ik_s, ret_s, s5r_s, s5i_s = [jnp.stack(a) for a in zip(*outs_s)]
    return (xp.reshape(bp, lp, d), xs[:bs].reshape(bs, ls, d), k_p, v_p, ik_p, k_s, v_s, ik_s,
            ret_p, ret_s, s5r_p, s5i_p, s5r_s, s5i_s)
```

```python
import functools
import math
from typing import NamedTuple

import numpy as np
import jax
import jax.numpy as jnp
from jax import lax
from jax.experimental import pallas as pl
from jax.experimental.pallas import tpu as pltpu

F32 = jnp.float32
BF16 = jnp.bfloat16
I32 = jnp.int32

PAGE_SIZE = 128
RET_HEADS = 4
RET_DK = 256
RET_DV = 512
RET_CHUNK = 128
ROPE_BASE = 10000.0
S5_GROUP = 16
S5_STATE = 64
DSA_HEADS = 8
DSA_KV_HEADS = 2
DSA_HEAD_DIM = 128
IDX_HEADS = 8
IDX_DIM = 64
DSA_TOPK = 256
LN_EPS = 1e-5
GN_EPS = 1e-5
N_BRANCHES = 3

LANES = 128
SUBLANES = 8
VMEM_LIMIT_CAP = 56 * 1024 * 1024
S5_CHUNK = 16
S5_OCT = LANES // S5_GROUP
SAMPLE_ROWS = 16
MASK_NEG = -1e30
INT_MIN = -2 ** 31


def _cparams(semantics, *block_bytes):
    est = 2 * sum(block_bytes) + (8 << 20)
    return pltpu.CompilerParams(dimension_semantics=semantics,
                                vmem_limit_bytes=int(min(max(est, 32 << 20), VMEM_LIMIT_CAP)))


def _nbytes(shape, dtype):
    return int(np.prod(shape)) * jnp.dtype(dtype).itemsize


class _LayerWeight(NamedTuple):
    arr: jax.Array
    layer: int
    col0: int
    ncols: int

    @property
    def shape(self):
        return (self.arr.shape[1], self.ncols)

    def spec(self, blk, idx):
        assert self.col0 % blk[1] == 0
        layer, off = self.layer, self.col0 // blk[1]

        def index_map(*g):
            r, c = idx(*g)
            return (layer, r, c + off)

        return pl.BlockSpec((None,) + tuple(blk), index_map)


def _dot(a, b):
    return jnp.dot(a, b, preferred_element_type=F32)


def _dot_nt(a, b):
    return lax.dot_general(a, b, (((1,), (1,)), ((), ())), preferred_element_type=F32)


def _layernorm(y, g, b):
    mu = jnp.mean(y, axis=-1, keepdims=True)
    yc = y - mu
    var = jnp.mean(yc * yc, axis=-1, keepdims=True)
    return yc * lax.rsqrt(var + LN_EPS) * g + b


def _col_of_row(r):
    n = r.shape[1]
    eye = lax.broadcasted_iota(I32, (n, n), 0) == lax.broadcasted_iota(I32, (n, n), 1)
    return jnp.sum(jnp.where(eye, jnp.broadcast_to(r, (n, n)), 0.0), axis=1, keepdims=True)


def _mm_kernel(x_ref, w_ref, *o_refs, gate):
    acc = _dot(x_ref[...], w_ref[...])
    if gate:
        acc = jax.nn.sigmoid(acc)
    for o_ref in o_refs:
        o_ref[...] = acc.astype(o_ref.dtype)


def _mm_oct_kernel(x_ref, w_ref, o_ref):
    acc = _dot(x_ref[...], w_ref[...])
    for i in range(o_ref.shape[0]):
        o_ref[i] = acc[:, i * LANES:(i + 1) * LANES]


def _mm(x, w, out_dtypes, oct_layout=False, gate=False, name="mm"):
    m, k = x.shape
    n = w.shape[1]
    tm = min(m, 1024)
    tn = n if n <= 1024 else 512
    while w.col0 % tn:
        tn //= 2
    assert m % tm == 0 and n % tn == 0 and tn % LANES == 0
    w_resident = (n // tn) * m * k + k * n <= (m // tm) * k * n + m * k
    if w_resident:
        grid = (n // tn, m // tm)
        xi, wi, oi = (lambda j, i: (i, 0)), (lambda j, i: (0, j)), (lambda j, i: (i, j))
        ooct = lambda j, i: (j, i, 0)
    else:
        grid = (m // tm, n // tn)
        xi, wi, oi = (lambda i, j: (i, 0)), (lambda i, j: (0, j)), (lambda i, j: (i, j))
        ooct = lambda i, j: (j, i, 0)
    in_specs = [pl.BlockSpec((tm, k), xi), w.spec((k, tn), wi)]
    blk = [_nbytes((tm, k), BF16), _nbytes((k, tn), BF16)]
    if oct_layout:
        out_shape = jax.ShapeDtypeStruct((n // LANES, m, LANES), F32)
        out_specs = pl.BlockSpec((tn // LANES, tm, LANES), ooct)
        body = _mm_oct_kernel
        blk.append(_nbytes((tm, tn), F32))
    else:
        out_shape = tuple(jax.ShapeDtypeStruct((m, n), d) for d in out_dtypes)
        out_specs = tuple(pl.BlockSpec((tm, tn), oi) for _ in out_dtypes)
        body = functools.partial(_mm_kernel, gate=gate)
        blk += [_nbytes((tm, tn), d) for d in out_dtypes]
    out = pl.pallas_call(body, grid=grid, in_specs=in_specs, out_specs=out_specs, out_shape=out_shape,
                         compiler_params=_cparams(("arbitrary", "arbitrary"), *blk), name=name)(x, w.arr)
    return out if oct_layout else (out[0] if len(out_dtypes) == 1 else out)


def _ffn_kernel(x_ref, wg_ref, wu_ref, wd_ref, g_ref, b_ref, o_ref, ob_ref, xb_scr, acc_scr, *, alpha):
    f = pl.program_id(1)

    @pl.when(f == 0)
    def _():
        xb_scr[...] = x_ref[...].astype(BF16)
        acc_scr[...] = jnp.zeros_like(acc_scr)

    xb = xb_scr[...]
    hg = _dot(xb, wg_ref[...])
    hu = _dot(xb, wu_ref[...])
    h = hg * jax.nn.sigmoid(hg) * hu
    acc_scr[...] += _dot(h.astype(BF16), wd_ref[...])

    @pl.when(f == pl.num_programs(1) - 1)
    def _():
        y = _layernorm(alpha * x_ref[...] + 0.5 * acc_scr[...], g_ref[...], b_ref[...])
        o_ref[...] = y
        ob_ref[...] = y.astype(BF16)


def _ffn_ln(x, wg, wu, wd, g, b, alpha):
    m, d = x.shape
    dff = wg.shape[1]
    tm = min(m, 512)
    tf = 512
    assert m % tm == 0 and dff % tf == 0
    row = lambda i, f: (i, 0)
    blk = [_nbytes((tm, d), F32), 3 * _nbytes((d, tf), BF16), _nbytes((tm, d), F32), _nbytes((tm, d), BF16),
           _nbytes((tm, d), F32)]
    return pl.pallas_call(
        functools.partial(_ffn_kernel, alpha=alpha),
        grid=(m // tm, dff // tf),
        in_specs=[pl.BlockSpec((tm, d), row),
                  wg.spec((d, tf), lambda i, f: (0, f)),
                  wu.spec((d, tf), lambda i, f: (0, f)),
                  wd.spec((tf, d), lambda i, f: (f, 0)),
                  pl.BlockSpec((1, d), lambda i, f: (0, 0)),
                  pl.BlockSpec((1, d), lambda i, f: (0, 0))],
        out_specs=(pl.BlockSpec((tm, d), row), pl.BlockSpec((tm, d), row)),
        out_shape=(jax.ShapeDtypeStruct((m, d), F32), jax.ShapeDtypeStruct((m, d), BF16)),
        scratch_shapes=[pltpu.VMEM((tm, d), BF16), pltpu.VMEM((tm, d), F32)],
        compiler_params=_cparams(("arbitrary", "arbitrary"), *blk),
        name="ffn_ln",
    )(x, wg.arr, wu.arr, wd.arr, g.reshape(1, d), b.reshape(1, d))


def _glu_kernel(y_ref, w_ref, o_ref):
    y = jnp.concatenate([y_ref[i] for i in range(y_ref.shape[0])], axis=-1)
    o_ref[...] = (y * jax.nn.sigmoid(_dot(y.astype(BF16), w_ref[...]))).astype(o_ref.dtype)


def _glu(y_oct, w):
    no, m, _ = y_oct.shape
    n = w.shape[1]
    tm = min(m, 512)
    blk = [_nbytes((no, tm, LANES), F32), _nbytes(w.shape, BF16), _nbytes((tm, n), BF16)]
    return pl.pallas_call(
        _glu_kernel, grid=(m // tm,),
        in_specs=[pl.BlockSpec((no, tm, LANES), lambda i: (0, i, 0)), w.spec(w.shape, lambda i: (0, 0))],
        out_specs=pl.BlockSpec((tm, n), lambda i: (i, 0)),
        out_shape=jax.ShapeDtypeStruct((m, n), BF16),
        compiler_params=_cparams(("arbitrary",), *blk),
        name="s5_glu",
    )(y_oct, w.arr)


def _merge_kernel(o_ref, z_ref, a_ref, g0_ref, g1_ref, g2_ref, wr_ref, ws_ref, wd_ref, m_ref):
    ret = _dot(o_ref[...], wr_ref[...])
    s5 = _dot(z_ref[...], ws_ref[...])
    dsa = _dot(a_ref[...], wd_ref[...])
    merged = g0_ref[...].astype(F32) * ret + g1_ref[...].astype(F32) * s5 + g2_ref[...].astype(F32) * dsa
    m_ref[...] = merged.astype(m_ref.dtype)


def _merge(o, z, a, gates, wr, ws, wd):
    m = o.shape[0]
    d = wr.shape[1]
    tm = min(m, 512)
    tn = min(d, 512)
    nb = d // tn
    blk = [_nbytes((tm, o.shape[1]), BF16), 2 * _nbytes((tm, z.shape[1]), BF16), 3 * _nbytes((tm, tn), BF16),
           _nbytes((o.shape[1], tn), BF16), 2 * _nbytes((z.shape[1], tn), BF16), _nbytes((tm, tn), BF16)]
    gate_spec = lambda br: pl.BlockSpec((tm, tn), lambda i, j: (i, br * nb + j))
    return pl.pallas_call(
        _merge_kernel, grid=(m // tm, nb),
        in_specs=[pl.BlockSpec((tm, o.shape[1]), lambda i, j: (i, 0)),
                  pl.BlockSpec((tm, z.shape[1]), lambda i, j: (i, 0)),
                  pl.BlockSpec((tm, a.shape[1]), lambda i, j: (i, 0)),
                  gate_spec(0), gate_spec(1), gate_spec(2),
                  wr.spec((wr.shape[0], tn), lambda i, j: (0, j)),
                  ws.spec((ws.shape[0], tn), lambda i, j: (0, j)),
                  wd.spec((wd.shape[0], tn), lambda i, j: (0, j))],
        out_specs=pl.BlockSpec((tm, tn), lambda i, j: (i, j)),
        out_shape=jax.ShapeDtypeStruct((m, d), BF16),
        compiler_params=_cparams(("arbitrary", "arbitrary"), *blk),
        name="branch_merge",
    )(o, z, a, gates, gates, gates, wr.arr, ws.arr, wd.arr)


def _out_ln_kernel(x_ref, m_ref, w_ref, g_ref, b_ref, o_ref, ob_ref, *, alpha):
    y = _layernorm(alpha * x_ref[...] + _dot(m_ref[...], w_ref[...]), g_ref[...], b_ref[...])
    o_ref[...] = y
    ob_ref[...] = y.astype(BF16)


def _out_ln(x, merged, w, g, b, alpha):
    m, d = x.shape
    tm = min(m, 512)
    row = lambda i: (i, 0)
    blk = [2 * _nbytes((tm, d), F32), 2 * _nbytes((tm, d), BF16), _nbytes((d, d), BF16)]
    return pl.pallas_call(
        functools.partial(_out_ln_kernel, alpha=alpha), grid=(m // tm,),
        in_specs=[pl.BlockSpec((tm, d), row), pl.BlockSpec((tm, d), row), w.spec((d, d), lambda i: (0, 0)),
                  pl.BlockSpec((1, d), lambda i: (0, 0)), pl.BlockSpec((1, d), lambda i: (0, 0))],
        out_specs=(pl.BlockSpec((tm, d), row), pl.BlockSpec((tm, d), row)),
        out_shape=(jax.ShapeDtypeStruct((m, d), F32), jax.ShapeDtypeStruct((m, d), BF16)),
        compiler_params=_cparams(("arbitrary",), *blk),
        name="out_ln",
    )(x, merged, w.arr, g.reshape(1, d), b.reshape(1, d))


def _rope(x, cos, sin):
    half = x.shape[-1] // 2
    x1, x2 = x[:, :half], x[:, half:]
    return jnp.concatenate([x1 * cos - x2 * sin, x1 * sin + x2 * cos], axis=-1)


def _group_norm_gate(o, gate):
    mu = jnp.mean(o, axis=-1, keepdims=True)
    oc = o - mu
    var = jnp.mean(oc * oc, axis=-1, keepdims=True)
    return gate * jax.nn.sigmoid(gate) * (oc * lax.rsqrt(var + GN_EPS))


def _ret_kernel(qk_ref, v_ref, g_ref, cos_ref, sin_ref, o_ref, st_ref):
    @pl.when(pl.program_id(1) == 0)
    def _():
        st_ref[...] = jnp.zeros_like(st_ref)

    c = qk_ref.shape[0]
    cos, sin = cos_ref[...], sin_ref[...]
    ri = lax.broadcasted_iota(I32, (c, c), 0).astype(F32)
    ci = lax.broadcasted_iota(I32, (c, c), 1).astype(F32)
    diff = ri - ci
    ti = lax.broadcasted_iota(I32, (c, 1), 0).astype(F32)
    for h in range(RET_HEADS):
        lg = math.log(1.0 - 2.0 ** (-5.0 - h))
        dmask = jnp.where(diff >= 0, jnp.exp(jnp.maximum(diff, 0.0) * lg), 0.0)
        q_dec = jnp.exp((ti + 1.0) * lg)
        k_dec = jnp.exp((c - 1.0 - ti) * lg)
        c_dec = math.exp(c * lg)
        q = _rope(qk_ref[:, h * RET_DK:(h + 1) * RET_DK], cos, sin)
        k = _rope(qk_ref[:, (RET_HEADS + h) * RET_DK:(RET_HEADS + h + 1) * RET_DK], cos, sin) * (RET_DK ** -0.5)
        v = v_ref[:, h * RET_DV:(h + 1) * RET_DV]
        s = st_ref[0, h]
        qb = q.astype(BF16)
        sc = _dot_nt(qb, k.astype(BF16)) * dmask
        o = _dot(sc.astype(BF16), v) + _dot(qb, s.astype(BF16)) * q_dec
        kd_t = jnp.transpose(k * k_dec).astype(BF16)
        st_ref[0, h] = c_dec * s + _dot(kd_t, v)
        o_ref[:, h * RET_DV:(h + 1) * RET_DV] = _group_norm_gate(
            o, g_ref[:, h * RET_DV:(h + 1) * RET_DV]).astype(o_ref.dtype)


def _retention(qk, v, gate, cos, sin, bn):
    m = qk.shape[0]
    length = m // bn
    c = RET_CHUNK
    nc = length // c
    hv = RET_HEADS * RET_DV
    row = lambda b, j: (b * nc + j, 0)
    blk = [_nbytes((c, qk.shape[1]), F32), _nbytes((c, hv), BF16), _nbytes((c, hv), F32), _nbytes((c, hv), BF16),
           _nbytes((RET_HEADS, RET_DK, RET_DV), F32)]
    return pl.pallas_call(
        _ret_kernel, grid=(bn, nc),
        in_specs=[pl.BlockSpec((c, qk.shape[1]), row), pl.BlockSpec((c, hv), row), pl.BlockSpec((c, hv), row),
                  pl.BlockSpec((c, RET_DK // 2), lambda b, j: (j, 0)),
                  pl.BlockSpec((c, RET_DK // 2), lambda b, j: (j, 0))],
        out_specs=(pl.BlockSpec((c, hv), row),
                   pl.BlockSpec((1, RET_HEADS, RET_DK, RET_DV), lambda b, j: (b, 0, 0, 0))),
        out_shape=(jax.ShapeDtypeStruct((m, hv), BF16),
                   jax.ShapeDtypeStruct((bn, RET_HEADS, RET_DK, RET_DV), F32)),
        compiler_params=_cparams(("arbitrary", "arbitrary"), *blk),
        name="retention",
    )(qk, v, gate, cos, sin)


def _ret_step_kernel(qk_ref, v_ref, g_ref, cos_ref, sin_ref, s0_ref, o_ref, st_ref):
    row = pl.ds(pl.program_id(0), 1)
    cos, sin = cos_ref[...], sin_ref[...]
    for h in range(RET_HEADS):
        decay = 1.0 - 2.0 ** (-5.0 - h)
        q = _rope(qk_ref[row, h * RET_DK:(h + 1) * RET_DK], cos, sin)
        k = _rope(qk_ref[row, (RET_HEADS + h) * RET_DK:(RET_HEADS + h + 1) * RET_DK], cos, sin) * (RET_DK ** -0.5)
        v = v_ref[row, h * RET_DV:(h + 1) * RET_DV]
        s0 = s0_ref[0, h]
        st_ref[0, h] = decay * s0 + _col_of_row(k) * v
        o = (jnp.sum(q * k, axis=1, keepdims=True) * v
             + jnp.sum(_col_of_row(q) * s0, axis=0, keepdims=True) * decay)
        o_ref[0, :, h * RET_DV:(h + 1) * RET_DV] = _group_norm_gate(o, g_ref[row, h * RET_DV:(h + 1) * RET_DV])


def _retention_step(qk, v, gate, cos, sin, s0):
    bn = s0.shape[0]
    const = lambda b: (0, 0)
    state = pl.BlockSpec((1,) + s0.shape[1:], lambda b: (b, 0, 0, 0))
    return pl.pallas_call(
        _ret_step_kernel, grid=(bn,),
        in_specs=[pl.BlockSpec(qk.shape, const), pl.BlockSpec(v.shape, const), pl.BlockSpec(gate.shape, const),
                  pl.BlockSpec(cos.shape, const), pl.BlockSpec(sin.shape, const), state],
        out_specs=(pl.BlockSpec((1, 1, v.shape[1]), lambda b: (b, 0, 0)), state),
        out_shape=(jax.ShapeDtypeStruct((bn, 1, v.shape[1]), F32), jax.ShapeDtypeStruct(s0.shape, F32)),
        compiler_params=_cparams(("arbitrary",), 2 * _nbytes(s0.shape[1:], F32)),
        name="retention_step",
    )(qk, v, gate, cos, sin, s0)


def _s5_tables(a_re, a_im, log_dt, b_re, b_im, c_re, c_im, d):
    g, p = a_re.shape
    nc = b_re.shape[-1]
    no = g // S5_OCT
    t = S5_CHUNK
    dt = jnp.exp(log_dt)[:, None]
    mag = jnp.exp(a_re * dt)
    ab_re = mag * jnp.cos(a_im * dt)
    ab_im = mag * jnp.sin(a_im * dt)
    den = a_re * a_re + a_im * a_im
    x_re = ab_re - 1.0
    f_re = (x_re * a_re + ab_im * a_im) / den
    f_im = (ab_im * a_re - x_re * a_im) / den
    bb_re = f_re[..., None] * b_re - f_im[..., None] * b_im
    bb_im = f_re[..., None] * b_im + f_im[..., None] * b_re

    def powers(n):
        n = n.astype(F32)[:, None, None]
        pmag = jnp.exp(n * (a_re * dt)[None])
        return pmag * jnp.cos(n * (a_im * dt)[None]), pmag * jnp.sin(n * (a_im * dt)[None])

    pw_re, pw_im = powers(jnp.arange(t + 1))
    rev_re, rev_im = powers(t - 1 - jnp.arange(t))
    bt_re, bt_im = jnp.swapaxes(bb_re, 1, 2), jnp.swapaxes(bb_im, 1, 2)

    def packed(v_re, v_im):
        out = []
        for v in (v_re, v_im):
            v = v.reshape(v.shape[0], no, S5_OCT, nc, p)
            out.append(jnp.swapaxes(v, 0, 1).reshape(no, v.shape[0] * LANES, p))
        return jnp.concatenate(out, axis=-1)

    def b_packed(power_re, power_im):
        pr, pi = power_re[:, :, None, :], power_im[:, :, None, :]
        return packed(pr * bt_re[None] - pi * bt_im[None], pr * bt_im[None] + pi * bt_re[None])

    def c_packed(power_re, power_im):
        pr, pi = power_re[:, :, None, :], power_im[:, :, None, :]
        return packed(c_re[None] * pr - c_im[None] * pi, -(c_re[None] * pi + c_im[None] * pr))

    def state_row(v_re, v_im):
        return jnp.concatenate([v_re.reshape(no, 1, S5_OCT * p), v_im.reshape(no, 1, S5_OCT * p)], axis=-1)

    d_row = d.reshape(no, 1, LANES)
    return dict(
        b_end=b_packed(rev_re, rev_im),
        c_in=c_packed(pw_re[1:], pw_im[1:]),
        a_row=state_row(pw_re[t], pw_im[t]),
        d_row=jnp.tile(d_row, (1, 1, t)),
        b_one=b_packed(pw_re[:1], pw_im[:1]),
        c_one=c_packed(pw_re[:1], pw_im[:1]),
        a_one=state_row(ab_re, ab_im),
        d_one=d_row,
    )


def _expand_groups(packed_ref, rows):
    n = rows.stop - rows.start
    g_row = (lax.broadcasted_iota(I32, (n, LANES), 0) // S5_GROUP) % S5_OCT
    g_half = lax.broadcasted_iota(I32, (n, LANES), 1) // S5_STATE
    x = packed_ref[0, rows, :]
    swapped = pltpu.roll(x, S5_STATE, axis=1)
    tiles = []
    for src in (jnp.where(g_half == 0, x, swapped), jnp.where(g_half == 0, swapped, x)):
        for pair in range(S5_OCT // 2):
            tiles.append(jnp.where(g_row == 2 * pair + g_half, src, 0.0))
    return jnp.concatenate(tiles, axis=-1)


def _s5_kernel(u_ref, cone_ref, bend_ref, cin_ref, a_ref, d_ref, y_ref, h_ref,
               toep_scr, bfull_scr, cfull_scr, e_scr, s_scr):
    rows, t = e_scr.shape[0], S5_CHUNK

    @pl.when(pl.program_id(1) == 0)
    def _():
        toep_scr[...] = jnp.zeros_like(toep_scr)
        same_group = (lax.broadcasted_iota(I32, (LANES, LANES), 0) // S5_GROUP
                      == lax.broadcasted_iota(I32, (LANES, LANES), 1) // S5_GROUP)
        for s in range(t):
            sl = slice(s * LANES, (s + 1) * LANES)
            bfull_scr[sl, :] = _expand_groups(bend_ref, sl).astype(BF16)
            cfull_scr[sl, :] = _expand_groups(cin_ref, sl).astype(BF16)
            lag = t - 1 - s
            blk = lax.dot_general(bend_ref[0, sl, :], cone_ref[0], (((1,), (1,)), ((), ())),
                                  precision=lax.Precision.HIGHEST, preferred_element_type=F32)
            blk = jnp.where(same_group, blk, 0.0).astype(BF16)
            for r in range(t - lag):
                toep_scr[r * LANES:(r + 1) * LANES, (r + lag) * LANES:(r + lag + 1) * LANES] = blk

    u = jnp.concatenate([u_ref[0, pl.ds(i, rows, stride=t), :] for i in range(t)], axis=-1)
    ub = u.astype(BF16)
    e_scr[...] = _dot(ub, bfull_scr[...])
    half = a_ref.shape[-1] // 2
    a_re, a_im = a_ref[0, :, :half], a_ref[0, :, half:]

    def step(k, carry):
        s_re, s_im = carry
        s_scr[pl.ds(k, 1), :] = jnp.concatenate([s_re, s_im], axis=-1)
        e = e_scr[pl.ds(k, 1), :]
        return (a_re * s_re - a_im * s_im + e[:, :half], a_re * s_im + a_im * s_re + e[:, half:])

    zero = jnp.zeros((1, half), F32)
    s_re, s_im = lax.fori_loop(0, rows, step, (zero, zero))
    h_ref[0, 0] = jnp.concatenate([s_re, s_im], axis=-1)
    y = jax.nn.gelu(_dot(ub, toep_scr[...]) + _dot_nt(s_scr[...].astype(BF16), cfull_scr[...]) + d_ref[0] * u)
    for i in range(t):
        y_ref[0, pl.ds(i, rows, stride=t), :] = y[:, i * LANES:(i + 1) * LANES]


def _s5(u_oct, tb, bn):
    no, m, _ = u_oct.shape
    t = S5_CHUNK
    steps = m // bn
    rows = steps // t
    w = t * LANES
    ns = tb['a_row'].shape[-1]
    pk = tb['b_end'].shape[-1]
    oct_blk = lambda shape: pl.BlockSpec((1,) + shape, lambda o, b: (o,) + (0,) * len(shape))
    blk = [2 * _nbytes((rows, w), F32), 2 * _nbytes((w, pk), F32),
           (_nbytes((w, w), BF16) + 2 * _nbytes((w, ns), BF16) + 2 * _nbytes((rows, ns), F32)) // 2]
    return pl.pallas_call(
        _s5_kernel, grid=(no, bn),
        in_specs=[pl.BlockSpec((1, steps, LANES), lambda o, b: (o, b, 0)),
                  oct_blk((LANES, pk)), oct_blk((w, pk)), oct_blk((w, pk)), oct_blk((1, ns)),
                  oct_blk((1, w))],
        out_specs=(pl.BlockSpec((1, steps, LANES), lambda o, b: (o, b, 0)),
                   pl.BlockSpec((1, 1, 1, ns), lambda o, b: (b, o, 0, 0))),
        out_shape=(jax.ShapeDtypeStruct(u_oct.shape, F32), jax.ShapeDtypeStruct((bn, no, 1, ns), F32)),
        scratch_shapes=[pltpu.VMEM((w, w), BF16), pltpu.VMEM((w, ns), BF16), pltpu.VMEM((w, ns), BF16),
                        pltpu.VMEM((rows, ns), F32), pltpu.VMEM((rows, ns), F32)],
        compiler_params=_cparams(("arbitrary", "arbitrary"), *blk),
        name="s5_scan",
    )(u_oct, tb['c_one'], tb['b_end'], tb['c_in'], tb['a_row'], tb['d_row'])


def _s5_step_kernel(u_ref, h0_ref, b_ref, c_ref, a_ref, d_ref, y_ref, h_ref):
    u = u_ref[0]
    half = a_ref.shape[-1] // 2
    a_re, a_im = a_ref[0, :, :half], a_ref[0, :, half:]
    h0 = h0_ref[0]
    h0_re, h0_im = h0[:, :half], h0[:, half:]
    whole = slice(0, LANES)
    bu = _dot(u.astype(BF16), _expand_groups(b_ref, whole).astype(BF16))
    h_re = a_re * h0_re - a_im * h0_im + bu[:, :half]
    h_im = a_re * h0_im + a_im * h0_re + bu[:, half:]
    h = jnp.concatenate([h_re, h_im], axis=-1)
    h_ref[0] = h
    y_ref[0] = jax.nn.gelu(_dot_nt(h.astype(BF16), _expand_groups(c_ref, whole).astype(BF16)) + d_ref[0] * u)


def _s5_step(u_oct, h0, tb):
    no, rows, _ = u_oct.shape
    ns = h0.shape[-1]
    pk = tb['b_one'].shape[-1]
    o3 = lambda shape: pl.BlockSpec((1,) + shape, lambda o: (o, 0, 0))
    return pl.pallas_call(
        _s5_step_kernel, grid=(no,),
        in_specs=[o3((rows, LANES)), o3((rows, ns)), o3((LANES, pk)), o3((LANES, pk)), o3((1, ns)), o3((1, LANES))],
        out_specs=(o3((rows, LANES)), o3((rows, ns))),
        out_shape=(jax.ShapeDtypeStruct(u_oct.shape, F32), jax.ShapeDtypeStruct(h0.shape, F32)),
        compiler_params=_cparams(("arbitrary",), _nbytes((LANES, ns), BF16) * 2),
        name="s5_step",
    )(u_oct, h0, tb['b_one'], tb['c_one'], tb['a_one'], tb['d_one'])


def _sort_key(x):
    bits = pltpu.bitcast(x, I32)
    return jnp.where(bits >= 0, bits, bits ^ jnp.int32(0x7FFFFFFF))


def _kth_largest_key(count_ge, nsel, shape):
    res = jnp.where(count_ge(jnp.zeros(shape, I32)) >= nsel, jnp.int32(0), jnp.int32(INT_MIN))

    def bit_step(i, res):
        cand = res | jnp.left_shift(jnp.int32(1), 30 - i)
        return jnp.where(count_ge(cand) >= nsel, cand, res)

    return lax.fori_loop(0, 31, bit_step, res)


def _dsa_kernel(dq_ref, iq_ref, ikw_ref, ikb_ref, kv_ref, o_ref, key_scr, bias_scr, m_scr, l_scr, acc_scr,
                *, nsel):
    tq = dq_ref.shape[0]
    jq = pl.program_id(1)
    nkb = jq + 1
    krow = lax.broadcasted_iota(I32, (tq, tq), 0)
    qcol = lax.broadcasted_iota(I32, (tq, tq), 1)
    qpos = jq * tq + qcol
    hd = DSA_HEAD_DIM
    gsz = DSA_HEADS // DSA_KV_HEADS

    def over_keys(x, op):
        part = op(x.reshape(x.shape[0] // SUBLANES, SUBLANES, x.shape[1]), axis=0)
        return op(part, axis=0, keepdims=True)

    iq_st = jnp.concatenate([iq_ref[:, h * LANES:(h + 1) * LANES] for h in range(IDX_HEADS)], axis=0)
    w_t = jnp.transpose(ikw_ref[...])

    def score_block(kb, carry):
        ik = ikb_ref[pl.ds(pl.multiple_of(kb * tq, tq), tq), :]
        sh = jnp.maximum(_dot_nt(ik, iq_st), 0.0)
        acc = jnp.zeros((tq, tq), F32)
        for h in range(IDX_HEADS):
            acc = acc + sh[:, h * tq:(h + 1) * tq] * w_t[IDX_DIM + h:IDX_DIM + h + 1, :]
        acc = jnp.where(kb * tq + krow <= qpos, acc, -jnp.inf)
        key_scr[kb] = _sort_key(acc)
        return carry

    lax.fori_loop(0, nkb, score_block, 0)

    acc_rows = 4 * SUBLANES

    def count(pred):
        def body(kb, c):
            hit = jnp.where(pred(key_scr[kb]), 1.0, 0.0)
            return c + jnp.sum(hit.reshape(tq // acc_rows, acc_rows, tq), axis=0)
        part = lax.fori_loop(0, nkb, body, jnp.zeros((acc_rows, tq), F32))
        return jnp.sum(part, axis=0, keepdims=True)

    kth = _kth_largest_key(lambda cand: count(lambda key: key >= cand), float(nsel), (1, tq))
    at_least = count(lambda key: key >= kth)

    def select_all_ge():
        def block(kb, carry):
            sel = (key_scr[kb] >= kth) & (kb * tq + krow <= qpos)
            bias_scr[kb] = jnp.where(sel, 0.0, MASK_NEG)
            return carry
        lax.fori_loop(0, nkb, block, 0)

    def select_with_ties():
        need = float(nsel) - count(lambda key: key > kth)
        tri = jnp.where(krow >= qcol, 1.0, 0.0).astype(BF16)

        def block(kb, taken):
            key = key_scr[kb]
            tie = jnp.where(key == kth, 1.0, 0.0)
            rank = taken + _dot(tri, tie.astype(BF16))
            sel = ((key > kth) | ((tie > 0.0) & (rank <= need))) & (kb * tq + krow <= qpos)
            bias_scr[kb] = jnp.where(sel, 0.0, MASK_NEG)
            return taken + over_keys(tie, jnp.sum)

        lax.fori_loop(0, nkb, block, jnp.zeros((1, tq), F32))

    lax.cond(jnp.max(at_least) > float(nsel), select_with_ties, select_all_ge)

    scale2 = (hd ** -0.5) * math.log2(math.e)
    qgs = [jnp.concatenate([dq_ref[:, (g * gsz + i) * hd:(g * gsz + i + 1) * hd] for i in range(gsz)], axis=0)
           for g in range(DSA_KV_HEADS)]
    m_scr[...] = jnp.full_like(m_scr, -jnp.inf)
    l_scr[...] = jnp.zeros_like(l_scr)
    acc_scr[...] = jnp.zeros_like(acc_scr)

    def attend_block(kb, carry):
        ks = pl.ds(pl.multiple_of(kb * tq, tq), tq)
        bias = jnp.concatenate([bias_scr[kb]] * gsz, axis=1)
        for g in range(DSA_KV_HEADS):
            kblk = kv_ref[ks, g * hd:(g + 1) * hd]
            vblk = kv_ref[ks, (DSA_KV_HEADS + g) * hd:(DSA_KV_HEADS + g + 1) * hd]
            lg = _dot_nt(kblk, qgs[g]) * scale2 + bias
            m_old = m_scr[g]
            m_new = jnp.maximum(m_old, over_keys(lg, jnp.max))
            alpha = jnp.exp2(m_old - m_new)
            p = jnp.exp2(lg - m_new)
            l_scr[g] = alpha * l_scr[g] + over_keys(p, jnp.sum)
            v_t = jnp.transpose(vblk.astype(F32)).astype(BF16)
            acc_scr[g] = alpha * acc_scr[g] + _dot(v_t, p.astype(BF16))
            m_scr[g] = m_new
        return carry

    lax.fori_loop(0, nkb, attend_block, 0)
    for g in range(DSA_KV_HEADS):
        out_t = acc_scr[g] / l_scr[g]
        for i in range(gsz):
            o_ref[:, (g * gsz + i) * hd:(g * gsz + i + 1) * hd] = jnp.transpose(
                out_t[:, i * tq:(i + 1) * tq]).astype(o_ref.dtype)


def _dsa_prompt(dq, iq, ikw, ikb, kvb, bn):
    m = dq.shape[0]
    length = m // bn
    tq = 256
    nq = length // tq
    nsel = min(DSA_TOPK, length // 4)
    qrow = lambda b, j: (b * nq + j, 0)
    full = lambda b, j: (b, 0)
    gsz = DSA_HEADS // DSA_KV_HEADS
    blk = [_nbytes((tq, dq.shape[1]), BF16) * 3, _nbytes((length, LANES), BF16), _nbytes((length, kvb.shape[1]), BF16),
           _nbytes((nq, tq, tq), F32)]
    return pl.pallas_call(
        functools.partial(_dsa_kernel, nsel=nsel), grid=(bn, nq),
        in_specs=[pl.BlockSpec((tq, dq.shape[1]), qrow), pl.BlockSpec((tq, iq.shape[1]), qrow),
                  pl.BlockSpec((tq, LANES), qrow), pl.BlockSpec((length, LANES), full),
                  pl.BlockSpec((length, kvb.shape[1]), full)],
        out_specs=pl.BlockSpec((tq, dq.shape[1]), qrow),
        out_shape=jax.ShapeDtypeStruct(dq.shape, BF16),
        scratch_shapes=[pltpu.VMEM((nq, tq, tq), I32), pltpu.VMEM((nq, tq, tq), F32),
                        pltpu.VMEM((DSA_KV_HEADS, 1, gsz * tq), F32), pltpu.VMEM((DSA_KV_HEADS, 1, gsz * tq), F32),
                        pltpu.VMEM((DSA_KV_HEADS, DSA_HEAD_DIM, gsz * tq), F32)],
        compiler_params=_cparams(("arbitrary", "arbitrary"), *blk),
        name="dsa_prompt",
    )(dq, iq, ikw, ikb, kvb)


MAX_PAGES_PER_STEP = 32


def _idx_heads(iq_row):
    return jnp.concatenate([iq_row[:, h * LANES:h * LANES + IDX_DIM] for h in range(IDX_HEADS)], axis=0)


def _page_score_kernel(pt_ref, iq_ref, ikw_ref, *rest):
    page_refs, o_ref = rest[:-1], rest[-1]
    b = pl.program_id(0)
    iq_h = jnp.concatenate([_idx_heads(iq_ref[pl.ds(b, 1), :]),
                            jnp.zeros((SAMPLE_ROWS - IDX_HEADS, IDX_DIM), F32)], axis=0).astype(BF16)
    w_col = _col_of_row(ikw_ref[pl.ds(b, 1), IDX_DIM:IDX_DIM + SAMPLE_ROWS])
    pages = jnp.concatenate([page[...].astype(BF16) for page in page_refs], axis=1)
    sh = jnp.maximum(_dot(iq_h, pages), 0.0)
    sc = jnp.sum(sh * w_col, axis=0, keepdims=True)
    for i in range(len(page_refs)):
        o_ref[0, i:i + 1, :] = sc[:, i * PAGE_SIZE:(i + 1) * PAGE_SIZE]


def _page_scores(page_table, iq, ikw, cache_idx_t, layer):
    bn, npages = page_table.shape
    pg = min(MAX_PAGES_PER_STEP, npages)
    const = lambda b, s, pt: (0, 0)

    def page_spec(i):
        return pl.BlockSpec((None, None, IDX_DIM, PAGE_SIZE), lambda b, s, pt: (layer, pt[b, s * pg + i], 0, 0))

    return pl.pallas_call(
        _page_score_kernel,
        grid_spec=pltpu.PrefetchScalarGridSpec(
            num_scalar_prefetch=1, grid=(bn, npages // pg),
            in_specs=[pl.BlockSpec(iq.shape, const), pl.BlockSpec(ikw.shape, const)]
            + [page_spec(i) for i in range(pg)],
            out_specs=pl.BlockSpec((1, pg, PAGE_SIZE), lambda b, s, pt: (b, s, 0))),
        out_shape=jax.ShapeDtypeStruct((bn, npages, PAGE_SIZE), F32),
        compiler_params=_cparams(("arbitrary", "arbitrary")),
        name="page_scores",
    )(page_table, iq, ikw, *([cache_idx_t] * pg))


def _page_select_kernel(sc_ref, iq_ref, ikw_ref, bias_ref, bias_self_ref, *, nsel):
    b = pl.program_id(0)
    npages, psz = sc_ref.shape[1:]
    iq_h = _idx_heads(iq_ref[pl.ds(b, 1), :])
    ikw = ikw_ref[pl.ds(b, 1), :]
    w_col = _col_of_row(ikw[:, IDX_DIM:IDX_DIM + IDX_HEADS])
    s_self = jnp.sum(jnp.maximum(jnp.sum(iq_h * ikw[:, :IDX_DIM], axis=1, keepdims=True), 0.0) * w_col,
                     axis=0, keepdims=True)
    key = _sort_key(sc_ref[0])
    key_self = _sort_key(s_self)

    def total(x):
        return jnp.sum(jnp.sum(x, axis=1, keepdims=True), axis=0, keepdims=True)

    def count_ge(cand):
        return total(jnp.where(key >= cand, 1.0, 0.0)) + jnp.where(key_self >= cand, 1.0, 0.0)

    kth = _kth_largest_key(count_ge, float(nsel), (1, 1))
    need = float(nsel) - (total(jnp.where(key > kth, 1.0, 0.0)) + jnp.where(key_self > kth, 1.0, 0.0))
    tie = jnp.where(key == kth, 1.0, 0.0)
    r_in = lax.broadcasted_iota(I32, (psz, psz), 0)
    c_in = lax.broadcasted_iota(I32, (psz, psz), 1)
    in_page = _dot(tie.astype(BF16), jnp.where(r_in <= c_in, 1.0, 0.0).astype(BF16))
    per_page = jnp.broadcast_to(jnp.sum(tie, axis=1, keepdims=True), (npages, psz)).astype(BF16)
    r_pg = lax.broadcasted_iota(I32, (npages, npages), 0)
    c_pg = lax.broadcasted_iota(I32, (npages, npages), 1)
    before = _dot(jnp.where(c_pg < r_pg, 1.0, 0.0).astype(BF16), per_page)
    sel = (key > kth) | ((tie > 0.0) & (before + in_page <= need))
    r_dup = lax.broadcasted_iota(I32, (psz, bias_ref.shape[2]), 0)
    c_dup = lax.broadcasted_iota(I32, (psz, bias_ref.shape[2]), 1)
    dup = jnp.where(c_dup // DSA_KV_HEADS == r_dup, 1.0, 0.0).astype(BF16)
    spread = _dot(jnp.where(sel, 1.0, 0.0).astype(BF16), dup)
    bias_ref[0] = jnp.where(spread > 0.5, 0.0, MASK_NEG)
    sel_self = (key_self > kth) | ((key_self == kth) & (total(tie) + 1.0 <= need))
    bias_self_ref[0] = jnp.broadcast_to(jnp.where(sel_self, 0.0, MASK_NEG), (1, LANES))


def _page_select(scores, iq, ikw, nsel):
    bn, npages, psz = scores.shape
    const = lambda b: (0, 0)
    wide = DSA_KV_HEADS * psz
    return pl.pallas_call(
        functools.partial(_page_select_kernel, nsel=nsel), grid=(bn,),
        in_specs=[pl.BlockSpec((1, npages, psz), lambda b: (b, 0, 0)),
                  pl.BlockSpec(iq.shape, const), pl.BlockSpec(ikw.shape, const)],
        out_specs=(pl.BlockSpec((1, npages, wide), lambda b: (b, 0, 0)),
                   pl.BlockSpec((1, 1, LANES), lambda b: (b, 0, 0))),
        out_shape=(jax.ShapeDtypeStruct((bn, npages, wide), F32), jax.ShapeDtypeStruct((bn, 1, LANES), F32)),
        compiler_params=_cparams(("arbitrary",)),
        name="page_select",
    )(scores, iq, ikw)


def _page_attend_kernel(pt_ref, dq_ref, kvs_ref, bias_ref, bself_ref, *rest, npg):
    k_refs, v_refs = rest[:npg], rest[npg:2 * npg]
    o_ref, m_scr, l_scr, acc_scr = rest[2 * npg:]
    b = pl.program_id(0)
    s = pl.program_id(1)
    hd = DSA_HEAD_DIM
    gsz = DSA_HEADS // DSA_KV_HEADS
    scale = hd ** -0.5
    prow = k_refs[0].shape[0]
    dq_row = dq_ref[pl.ds(b, 1), :]
    q = jnp.concatenate([dq_row[:, h * hd:(h + 1) * hd] for h in range(DSA_HEADS)]
                        + [jnp.zeros((SAMPLE_ROWS - DSA_HEADS, hd), F32)], axis=0)
    kv_of_head = lax.broadcasted_iota(I32, (SAMPLE_ROWS, 1), 0) // gsz

    @pl.when(s == 0)
    def _():
        m_scr[...] = jnp.full_like(m_scr, -jnp.inf)
        l_scr[...] = jnp.zeros_like(l_scr)
        acc_scr[...] = jnp.zeros_like(acc_scr)

    def update(lg, pv_of):
        m_old = m_scr[...]
        m_new = jnp.maximum(m_old, jnp.max(lg, axis=1, keepdims=True))
        alpha = jnp.exp(m_old - m_new)
        p = jnp.exp(lg - m_new)
        l_scr[...] = alpha * l_scr[...] + jnp.sum(p, axis=1, keepdims=True)
        acc_scr[...] = alpha * acc_scr[...] + pv_of(p)
        m_scr[...] = m_new

    qb = q.astype(BF16)
    lg = jnp.concatenate([_dot_nt(qb, k_refs[i][...].astype(BF16)) for i in range(npg)], axis=1) * scale
    bias = jnp.concatenate([bias_ref[0, i:i + 1, :] for i in range(npg)], axis=1)
    col = lax.broadcasted_iota(I32, lg.shape, 1)
    lg = jnp.where(jnp.bitwise_and(col, DSA_KV_HEADS - 1) == kv_of_head, lg + bias, MASK_NEG)

    def pv_pages(p):
        pb = p.astype(BF16)
        acc = jnp.zeros((SAMPLE_ROWS, hd), F32)
        for i in range(npg):
            acc = acc + _dot(pb[:, i * prow:(i + 1) * prow], v_refs[i][...].astype(BF16))
        return acc

    update(lg, pv_pages)

    @pl.when(s == pl.num_programs(1) - 1)
    def _():
        kvs = kvs_ref[pl.ds(b, 1), :]
        k_self = jnp.where(kv_of_head == 0, kvs[:, :hd], kvs[:, hd:2 * hd])
        v_self = jnp.where(kv_of_head == 0, kvs[:, 2 * hd:3 * hd], kvs[:, 3 * hd:])
        lg_self = jnp.sum(q * k_self, axis=1, keepdims=True) * scale + bself_ref[0, :, :1]
        update(lg_self, lambda p: p * v_self)
        out = acc_scr[...] / l_scr[...]
        for h in range(DSA_HEADS):
            o_ref[0, :, h * hd:(h + 1) * hd] = out[h:h + 1]


def _page_attend(page_table, dq, kvs, bias, bias_self, cache_k, cache_v, layer):
    assert DSA_KV_HEADS == 2
    bn, npages = page_table.shape
    pg = min(MAX_PAGES_PER_STEP, npages)
    prow, hd = cache_k.shape[-2:]
    const = lambda b, s, pt: (0, 0)

    def page_spec(i):
        return pl.BlockSpec((None, None, prow, hd), lambda b, s, pt: (layer, pt[b, s * pg + i], 0, 0))

    return pl.pallas_call(
        functools.partial(_page_attend_kernel, npg=pg),
        grid_spec=pltpu.PrefetchScalarGridSpec(
            num_scalar_prefetch=1, grid=(bn, npages // pg),
            in_specs=[pl.BlockSpec(dq.shape, const), pl.BlockSpec(kvs.shape, const),
                      pl.BlockSpec((1, pg, prow), lambda b, s, pt: (b, s, 0)),
                      pl.BlockSpec((1, 1, LANES), lambda b, s, pt: (b, 0, 0))]
            + [page_spec(i) for i in range(pg)] + [page_spec(i) for i in range(pg)],
            out_specs=pl.BlockSpec((1, 1, dq.shape[1]), lambda b, s, pt: (b, 0, 0)),
            scratch_shapes=[pltpu.VMEM((SAMPLE_ROWS, 1), F32), pltpu.VMEM((SAMPLE_ROWS, 1), F32),
                            pltpu.VMEM((SAMPLE_ROWS, hd), F32)]),
        out_shape=jax.ShapeDtypeStruct((bn, 1, dq.shape[1]), F32),
        compiler_params=_cparams(("arbitrary", "arbitrary"), 2 * pg * _nbytes((prow, hd), F32)),
        name="page_attend",
    )(page_table, dq, kvs, bias, bias_self, *([cache_k] * pg), *([cache_v] * pg))


def _rope_tables(pos):
    half = RET_DK // 2
    inv = 1.0 / (ROPE_BASE ** jnp.linspace(0.0, 1.0, half, dtype=F32))
    ang = pos.astype(F32)[:, None] * inv[None, :]
    return jnp.cos(ang), jnp.sin(ang)


def _stacked_bf16(a):
    arr = a.astype(BF16)
    return lambda l: _LayerWeight(arr, l, 0, arr.shape[2])


def _in_proj_weights(w_in):
    depth, d, _ = w_in.shape
    hq = RET_HEADS * RET_DK
    hv = RET_HEADS * RET_DV
    o = 0
    cuts = {}
    for name, width in (('qk', 2 * hq), ('v', hv), ('g', hv)):
        cuts[name] = (o, width)
        o += width
    s5w = (w_in.shape[2] - 2 * hq - 2 * hv - DSA_HEADS * DSA_HEAD_DIM - 2 * DSA_KV_HEADS * DSA_HEAD_DIM
           - IDX_HEADS * IDX_DIM - IDX_DIM - IDX_HEADS) // (1 + 2 * N_BRANCHES)
    for name, width in (('su', s5w), ('dq', DSA_HEADS * DSA_HEAD_DIM), ('kv', 2 * DSA_KV_HEADS * DSA_HEAD_DIM),
                        ('iq', IDX_HEADS * IDX_DIM), ('ikw', IDX_DIM + IDX_HEADS), ('gates', N_BRANCHES * d)):
        cuts[name] = (o, width)
        o += width
    assert o == w_in.shape[2]
    cut = lambda name: w_in[:, :, cuts[name][0]:cuts[name][0] + cuts[name][1]]
    iq = jnp.pad(cut('iq').reshape(depth, d, IDX_HEADS, IDX_DIM), ((0, 0), (0, 0), (0, 0), (0, LANES - IDX_DIM)))
    repacked = {'iq': iq.reshape(depth, d, IDX_HEADS * LANES),
                'ikw': jnp.pad(cut('ikw'), ((0, 0), (0, 0), (0, LANES - IDX_DIM - IDX_HEADS))),
                'gates': cut('gates')}

    def layer(l):
        wb = {k: _LayerWeight(w_in, l, *cuts[k]) for k in ('qk', 'v', 'g', 'su', 'dq', 'kv')}
        wb.update({k: _LayerWeight(a, l, 0, a.shape[2]) for k, a in repacked.items()})
        return wb

    return layer


def _project(xb, wb, sample):
    act = F32 if sample else BF16
    p = {
        'qk': _mm(xb, wb['qk'], (F32,), name="proj_qk"),
        'v': _mm(xb, wb['v'], (act,), name="proj_v"),
        'g': _mm(xb, wb['g'], (F32,), name="proj_g"),
        'su': _mm(xb, wb['su'], None, oct_layout=True, name="proj_su"),
        'dq': _mm(xb, wb['dq'], (act,), name="proj_dq"),
        'iq': _mm(xb, wb['iq'], (act,), name="proj_iq"),
        'gates': _mm(xb, wb['gates'], (BF16,), gate=True, name="proj_gates"),
    }
    p['kv'], p['kvb'] = _mm(xb, wb['kv'], (F32, BF16), name="proj_kv")
    p['ikw'], p['ikb'] = _mm(xb, wb['ikw'], (F32, BF16), name="proj_ikw")
    return p


def _s5_state_out(h, groups):
    bn = h.shape[0]
    h = h.reshape(bn, groups // S5_OCT, 2, S5_OCT, S5_STATE)
    return h[:, :, 0].reshape(bn, groups, S5_STATE), h[:, :, 1].reshape(bn, groups, S5_STATE)


def kernel(x_prompt, x_sample, cache_k, cache_v, cache_idx_k, state_ret, state_s5_re, state_s5_im, page_table, ln1_g, ln1_b, ffn1_wg, ffn1_wu, ffn1_wd, w_in, s5_a_re, s5_a_im, s5_log_dt, s5_b_re, s5_b_im, s5_c_re, s5_c_im, s5_d, w_glu, w_ret_o, w_s5_o, w_dsa_o, w_out, ln2_g, ln2_b, ffn2_wg, ffn2_wu, ffn2_wd, ln3_g, ln3_b):
    bp, lp, d = x_prompt.shape
    bs, ls, _ = x_sample.shape
    depth = w_in.shape[0]
    assert ls == 1 and bs <= SAMPLE_ROWS
    npages = page_table.shape[1]
    past = npages * PAGE_SIZE
    groups = s5_a_re.shape[1]
    alpha = (2 * depth) ** 0.25
    kvw = DSA_KV_HEADS * DSA_HEAD_DIM
    nsel_s = min(DSA_TOPK, (past + ls) // 4)

    cos_p, sin_p = _rope_tables(jnp.arange(lp))
    cos_s, sin_s = _rope_tables(past + jnp.arange(ls))
    cache_k = cache_k.reshape(depth, -1, PAGE_SIZE * DSA_KV_HEADS, DSA_HEAD_DIM)
    cache_v = cache_v.reshape(depth, -1, PAGE_SIZE * DSA_KV_HEADS, DSA_HEAD_DIM)
    cache_idx_t = jnp.swapaxes(cache_idx_k, 2, 3)

    in_proj = _in_proj_weights(w_in.astype(BF16))
    ffn1 = [_stacked_bf16(a) for a in (ffn1_wg, ffn1_wu, ffn1_wd)]
    ffn2 = [_stacked_bf16(a) for a in (ffn2_wg, ffn2_wu, ffn2_wd)]
    mixer_out = [_stacked_bf16(a) for a in (w_glu, w_ret_o, w_s5_o, w_dsa_o, w_out)]

    xp = x_prompt.reshape(bp * lp, d)
    xs = jnp.pad(x_sample.reshape(bs, d), ((0, SAMPLE_ROWS - bs), (0, 0)))
    outs_p, outs_s = [], []
    for l in range(depth):
        wb = in_proj(l)
        tb = _s5_tables(s5_a_re[l], s5_a_im[l], s5_log_dt[l], s5_b_re[l], s5_b_im[l], s5_c_re[l], s5_c_im[l],
                        s5_d[l])
        w1 = [w(l) for w in ffn1]
        w2 = [w(l) for w in ffn2]
        wglu, wro, wso, wdo, wo = [w(l) for w in mixer_out]

        xp, xpb = _ffn_ln(xp, *w1, ln1_g[l], ln1_b[l], alpha)
        p = _project(xpb, wb, sample=False)
        o_ret, ret_p = _retention(p['qk'], p['v'], p['g'], cos_p, sin_p, bp)
        y_s5, h_p = _s5(p['su'], tb, bp)
        z = _glu(y_s5, wglu)
        att = _dsa_prompt(p['dq'], p['iq'], p['ikw'], p['ikb'], p['kvb'], bp)
        merged = _merge(o_ret, z, att, p['gates'], wro, wso, wdo)
        xp, xpb = _out_ln(xp, merged, wo, ln2_g[l], ln2_b[l], alpha)
        xp, xpb = _ffn_ln(xp, *w2, ln3_g[l], ln3_b[l], alpha)
        s5r_p, s5i_p = _s5_state_out(h_p, groups)
        outs_p.append((p['kv'][:, :kvw].reshape(bp, lp, DSA_KV_HEADS, DSA_HEAD_DIM),
                       p['kv'][:, kvw:].reshape(bp, lp, DSA_KV_HEADS, DSA_HEAD_DIM),
                       p['ikw'][:, :IDX_DIM].reshape(bp, lp, IDX_DIM), ret_p, s5r_p, s5i_p))

        xs, xsb = _ffn_ln(xs, *w1, ln1_g[l], ln1_b[l], alpha)
        q = _project(xsb, wb, sample=True)
        o_ret_s, ret_s = _retention_step(q['qk'], q['v'], q['g'], cos_s, sin_s, state_ret[l])
        o_ret_s = jnp.pad(o_ret_s[:, 0], ((0, SAMPLE_ROWS - bs), (0, 0))).astype(BF16)
        h0 = jnp.concatenate([state_s5_re[l].reshape(bs, groups // S5_OCT, S5_OCT * S5_STATE),
                              state_s5_im[l].reshape(bs, groups // S5_OCT, S5_OCT * S5_STATE)], axis=-1)
        h0 = jnp.pad(jnp.swapaxes(h0, 0, 1), ((0, 0), (0, SAMPLE_ROWS - bs), (0, 0)))
        y_s, h_s = _s5_step(q['su'], h0, tb)
        z_s = _glu(y_s, wglu)
        scores = _page_scores(page_table, q['iq'], q['ikw'], cache_idx_t, l)
        bias, bias_self = _page_select(scores, q['iq'], q['ikw'], nsel_s)
        att_s = _page_attend(page_table, q['dq'], q['kv'], bias, bias_self, cache_k, cache_v, l)
        att_s = jnp.pad(att_s[:, 0], ((0, SAMPLE_ROWS - bs), (0, 0))).astype(BF16)
        merged_s = _merge(o_ret_s, z_s, att_s, q['gates'], wro, wso, wdo)
        xs, xsb = _out_ln(xs, merged_s, wo, ln2_g[l], ln2_b[l], alpha)
        xs, xsb = _ffn_ln(xs, *w2, ln3_g[l], ln3_b[l], alpha)
        s5r_s, s5i_s = _s5_state_out(jnp.swapaxes(h_s, 0, 1)[:bs, :, None, :], groups)
        outs_s.append((q['kv'][:bs, :kvw].reshape(bs, ls, DSA_KV_HEADS, DSA_HEAD_DIM),
                       q['kv'][:bs, kvw:].reshape(bs, ls, DSA_KV_HEADS, DSA_HEAD_DIM),
                       q['ikw'][:bs, :IDX_DIM].reshape(bs, ls, IDX_DIM), ret_s, s5r_s, s5i_s))

    k_p, v_p, ik_p, ret_p, s5r_p, s5i_p = [jnp.stack(a) for a in zip(*outs_p)]
    k_s, v_s, ik_s, ret_s, s5r_s, s5i_s = [jnp.stack(a) for a in zip(*outs_s)]
    return (xp.reshape(bp, lp, d), xs[:bs].reshape(bs, ls, d), k_p, v_p, ik_p, k_s, v_s, ik_s,
            ret_p, ret_s, s5r_p, s5i_p, s5r_s, s5i_s)
```

```python
import functools
import math
from typing import NamedTuple

import numpy as np
import jax
import jax.numpy as jnp
from jax import lax
from jax.experimental import pallas as pl
from jax.experimental.pallas import tpu as pltpu

F32 = jnp.float32
BF16 = jnp.bfloat16
I32 = jnp.int32

PAGE_SIZE = 128
RET_HEADS = 4
RET_DK = 256
RET_DV = 512
RET_CHUNK = 128
ROPE_BASE = 10000.0
S5_GROUP = 16
S5_STATE = 64
DSA_HEADS = 8
DSA_KV_HEADS = 2
DSA_HEAD_DIM = 128
IDX_HEADS = 8
IDX_DIM = 64
DSA_TOPK = 256
LN_EPS = 1e-5
GN_EPS = 1e-5
N_BRANCHES = 3

LANES = 128
SUBLANES = 8
VMEM_LIMIT_CAP = 56 * 1024 * 1024
S5_CHUNK = 16
S5_OCT = LANES // S5_GROUP
SAMPLE_ROWS = 16
MASK_NEG = -1e30
INT_MIN = -2 ** 31


def _cparams(semantics, *block_bytes):
    est = 2 * sum(block_bytes) + (8 << 20)
    return pltpu.CompilerParams(dimension_semantics=semantics,
                                vmem_limit_bytes=int(min(max(est, 32 << 20), VMEM_LIMIT_CAP)))


def _nbytes(shape, dtype):
    return int(np.prod(shape)) * jnp.dtype(dtype).itemsize


class _LayerWeight(NamedTuple):
    arr: jax.Array
    layer: int
    col0: int
    ncols: int

    @property
    def shape(self):
        return (self.arr.shape[1], self.ncols)

    def spec(self, blk, idx):
        assert self.col0 % blk[1] == 0
        layer, off = self.layer, self.col0 // blk[1]

        def index_map(*g):
            r, c = idx(*g)
            return (layer, r, c + off)

        return pl.BlockSpec((None,) + tuple(blk), index_map)


def _dot(a, b):
    return jnp.dot(a, b, preferred_element_type=F32)


def _dot_nt(a, b):
    return lax.dot_general(a, b, (((1,), (1,)), ((), ())), preferred_element_type=F32)


def _layernorm(y, g, b):
    mu = jnp.mean(y, axis=-1, keepdims=True)
    yc = y - mu
    var = jnp.mean(yc * yc, axis=-1, keepdims=True)
    return yc * lax.rsqrt(var + LN_EPS) * g + b


def _col_of_row(r):
    n = r.shape[1]
    eye = lax.broadcasted_iota(I32, (n, n), 0) == lax.broadcasted_iota(I32, (n, n), 1)
    return jnp.sum(jnp.where(eye, jnp.broadcast_to(r, (n, n)), 0.0), axis=1, keepdims=True)


def _mm_kernel(x_ref, w_ref, *o_refs, gate):
    acc = _dot(x_ref[...], w_ref[...])
    if gate:
        acc = jax.nn.sigmoid(acc)
    for o_ref in o_refs:
        o_ref[...] = acc.astype(o_ref.dtype)


def _mm_oct_kernel(x_ref, w_ref, o_ref):
    acc = _dot(x_ref[...], w_ref[...])
    for i in range(o_ref.shape[0]):
        o_ref[i] = acc[:, i * LANES:(i + 1) * LANES]


def _mm(x, w, out_dtypes, oct_layout=False, gate=False, name="mm"):
    m, k = x.shape
    n = w.shape[1]
    tm = min(m, 1024)
    tn = n if n <= 1024 else (1024 if n % 1024 == 0 else 512)
    while w.col0 % tn:
        tn //= 2
    assert m % tm == 0 and n % tn == 0 and tn % LANES == 0
    w_resident = (n // tn) * m * k + k * n <= (m // tm) * k * n + m * k
    if w_resident:
        grid = (n // tn, m // tm)
        xi, wi, oi = (lambda j, i: (i, 0)), (lambda j, i: (0, j)), (lambda j, i: (i, j))
        ooct = lambda j, i: (j, i, 0)
    else:
        grid = (m // tm, n // tn)
        xi, wi, oi = (lambda i, j: (i, 0)), (lambda i, j: (0, j)), (lambda i, j: (i, j))
        ooct = lambda i, j: (j, i, 0)
    in_specs = [pl.BlockSpec((tm, k), xi), w.spec((k, tn), wi)]
    blk = [_nbytes((tm, k), BF16), _nbytes((k, tn), BF16)]
    if oct_layout:
        out_shape = jax.ShapeDtypeStruct((n // LANES, m, LANES), F32)
        out_specs = pl.BlockSpec((tn // LANES, tm, LANES), ooct)
        body = _mm_oct_kernel
        blk.append(_nbytes((tm, tn), F32))
    else:
        out_shape = tuple(jax.ShapeDtypeStruct((m, n), d) for d in out_dtypes)
        out_specs = tuple(pl.BlockSpec((tm, tn), oi) for _ in out_dtypes)
        body = functools.partial(_mm_kernel, gate=gate)
        blk += [_nbytes((tm, tn), d) for d in out_dtypes]
    out = pl.pallas_call(body, grid=grid, in_specs=in_specs, out_specs=out_specs, out_shape=out_shape,
                         compiler_params=_cparams(("arbitrary", "arbitrary"), *blk), name=name)(x, w.arr)
    return out if oct_layout else (out[0] if len(out_dtypes) == 1 else out)


def _ffn_kernel(x_ref, wg_ref, wu_ref, wd_ref, g_ref, b_ref, o_ref, ob_ref, xb_scr, acc_scr, *, alpha):
    f = pl.program_id(1)

    @pl.when(f == 0)
    def _():
        xb_scr[...] = x_ref[...].astype(BF16)
        acc_scr[...] = jnp.zeros_like(acc_scr)

    xb = xb_scr[...]
    hg = _dot(xb, wg_ref[...])
    hu = _dot(xb, wu_ref[...])
    h = hg * jax.nn.sigmoid(hg) * hu
    acc_scr[...] += _dot(h.astype(BF16), wd_ref[...])

    @pl.when(f == pl.num_programs(1) - 1)
    def _():
        y = _layernorm(alpha * x_ref[...] + 0.5 * acc_scr[...], g_ref[...], b_ref[...])
        o_ref[...] = y
        ob_ref[...] = y.astype(BF16)


def _ffn_ln(x, wg, wu, wd, g, b, alpha):
    m, d = x.shape
    dff = wg.shape[1]
    tm = min(m, 512)
    tf = 512
    assert m % tm == 0 and dff % tf == 0
    row = lambda i, f: (i, 0)
    blk = [_nbytes((tm, d), F32), 3 * _nbytes((d, tf), BF16), _nbytes((tm, d), F32), _nbytes((tm, d), BF16),
           _nbytes((tm, d), F32)]
    return pl.pallas_call(
        functools.partial(_ffn_kernel, alpha=alpha),
        grid=(m // tm, dff // tf),
        in_specs=[pl.BlockSpec((tm, d), row),
                  wg.spec((d, tf), lambda i, f: (0, f)),
                  wu.spec((d, tf), lambda i, f: (0, f)),
                  wd.spec((tf, d), lambda i, f: (f, 0)),
                  pl.BlockSpec((1, d), lambda i, f: (0, 0)),
                  pl.BlockSpec((1, d), lambda i, f: (0, 0))],
        out_specs=(pl.BlockSpec((tm, d), row), pl.BlockSpec((tm, d), row)),
        out_shape=(jax.ShapeDtypeStruct((m, d), F32), jax.ShapeDtypeStruct((m, d), BF16)),
        scratch_shapes=[pltpu.VMEM((tm, d), BF16), pltpu.VMEM((tm, d), F32)],
        compiler_params=_cparams(("arbitrary", "arbitrary"), *blk),
        name="ffn_ln",
    )(x, wg.arr, wu.arr, wd.arr, g.reshape(1, d), b.reshape(1, d))


def _glu_kernel(y_ref, w_ref, o_ref):
    y = jnp.concatenate([y_ref[i] for i in range(y_ref.shape[0])], axis=-1)
    o_ref[...] = (y * jax.nn.sigmoid(_dot(y.astype(BF16), w_ref[...]))).astype(o_ref.dtype)


def _glu(y_oct, w):
    no, m, _ = y_oct.shape
    n = w.shape[1]
    tm = min(m, 512)
    blk = [_nbytes((no, tm, LANES), F32), _nbytes(w.shape, BF16), _nbytes((tm, n), BF16)]
    return pl.pallas_call(
        _glu_kernel, grid=(m // tm,),
        in_specs=[pl.BlockSpec((no, tm, LANES), lambda i: (0, i, 0)), w.spec(w.shape, lambda i: (0, 0))],
        out_specs=pl.BlockSpec((tm, n), lambda i: (i, 0)),
        out_shape=jax.ShapeDtypeStruct((m, n), BF16),
        compiler_params=_cparams(("arbitrary",), *blk),
        name="s5_glu",
    )(y_oct, w.arr)


def _merge_kernel(o_ref, z_ref, a_ref, g0_ref, g1_ref, g2_ref, wr_ref, ws_ref, wd_ref, m_ref):
    ret = _dot(o_ref[...], wr_ref[...])
    s5 = _dot(z_ref[...], ws_ref[...])
    dsa = _dot(a_ref[...], wd_ref[...])
    merged = g0_ref[...].astype(F32) * ret + g1_ref[...].astype(F32) * s5 + g2_ref[...].astype(F32) * dsa
    m_ref[...] = merged.astype(m_ref.dtype)


def _merge(o, z, a, gates, wr, ws, wd):
    m = o.shape[0]
    d = wr.shape[1]
    tm = min(m, 512)
    tn = min(d, 1024)
    nb = d // tn
    blk = [_nbytes((tm, o.shape[1]), BF16), 2 * _nbytes((tm, z.shape[1]), BF16), 3 * _nbytes((tm, tn), BF16),
           _nbytes((o.shape[1], tn), BF16), 2 * _nbytes((z.shape[1], tn), BF16), _nbytes((tm, tn), BF16)]
    gate_spec = lambda br: pl.BlockSpec((tm, tn), lambda i, j: (i, br * nb + j))
    return pl.pallas_call(
        _merge_kernel, grid=(m // tm, nb),
        in_specs=[pl.BlockSpec((tm, o.shape[1]), lambda i, j: (i, 0)),
                  pl.BlockSpec((tm, z.shape[1]), lambda i, j: (i, 0)),
                  pl.BlockSpec((tm, a.shape[1]), lambda i, j: (i, 0)),
                  gate_spec(0), gate_spec(1), gate_spec(2),
                  wr.spec((wr.shape[0], tn), lambda i, j: (0, j)),
                  ws.spec((ws.shape[0], tn), lambda i, j: (0, j)),
                  wd.spec((wd.shape[0], tn), lambda i, j: (0, j))],
        out_specs=pl.BlockSpec((tm, tn), lambda i, j: (i, j)),
        out_shape=jax.ShapeDtypeStruct((m, d), BF16),
        compiler_params=_cparams(("arbitrary", "arbitrary"), *blk),
        name="branch_merge",
    )(o, z, a, gates, gates, gates, wr.arr, ws.arr, wd.arr)


def _out_ln_kernel(x_ref, m_ref, w_ref, g_ref, b_ref, o_ref, ob_ref, *, alpha):
    y = _layernorm(alpha * x_ref[...] + _dot(m_ref[...], w_ref[...]), g_ref[...], b_ref[...])
    o_ref[...] = y
    ob_ref[...] = y.astype(BF16)


def _out_ln(x, merged, w, g, b, alpha):
    m, d = x.shape
    tm = min(m, 512)
    row = lambda i: (i, 0)
    blk = [2 * _nbytes((tm, d), F32), 2 * _nbytes((tm, d), BF16), _nbytes((d, d), BF16)]
    return pl.pallas_call(
        functools.partial(_out_ln_kernel, alpha=alpha), grid=(m // tm,),
        in_specs=[pl.BlockSpec((tm, d), row), pl.BlockSpec((tm, d), row), w.spec((d, d), lambda i: (0, 0)),
                  pl.BlockSpec((1, d), lambda i: (0, 0)), pl.BlockSpec((1, d), lambda i: (0, 0))],
        out_specs=(pl.BlockSpec((tm, d), row), pl.BlockSpec((tm, d), row)),
        out_shape=(jax.ShapeDtypeStruct((m, d), F32), jax.ShapeDtypeStruct((m, d), BF16)),
        compiler_params=_cparams(("arbitrary",), *blk),
        name="out_ln",
    )(x, merged, w.arr, g.reshape(1, d), b.reshape(1, d))


def _rope(x, cos, sin):
    half = x.shape[-1] // 2
    x1, x2 = x[:, :half], x[:, half:]
    return jnp.concatenate([x1 * cos - x2 * sin, x1 * sin + x2 * cos], axis=-1)


def _group_norm_gate(o, gate):
    mu = jnp.mean(o, axis=-1, keepdims=True)
    oc = o - mu
    var = jnp.mean(oc * oc, axis=-1, keepdims=True)
    return gate * jax.nn.sigmoid(gate) * (oc * lax.rsqrt(var + GN_EPS))


def _ret_kernel(qk_ref, v_ref, g_ref, cos_ref, sin_ref, o_ref, st_ref):
    @pl.when(pl.program_id(1) == 0)
    def _():
        st_ref[...] = jnp.zeros_like(st_ref)

    c = qk_ref.shape[0]
    cos, sin = cos_ref[...], sin_ref[...]
    ri = lax.broadcasted_iota(I32, (c, c), 0).astype(F32)
    ci = lax.broadcasted_iota(I32, (c, c), 1).astype(F32)
    diff = ri - ci
    ti = lax.broadcasted_iota(I32, (c, 1), 0).astype(F32)
    for h in range(RET_HEADS):
        lg = math.log(1.0 - 2.0 ** (-5.0 - h))
        dmask = jnp.where(diff >= 0, jnp.exp(jnp.maximum(diff, 0.0) * lg), 0.0)
        q_dec = jnp.exp((ti + 1.0) * lg)
        k_dec = jnp.exp((c - 1.0 - ti) * lg)
        c_dec = math.exp(c * lg)
        q = _rope(qk_ref[:, h * RET_DK:(h + 1) * RET_DK], cos, sin)
        k = _rope(qk_ref[:, (RET_HEADS + h) * RET_DK:(RET_HEADS + h + 1) * RET_DK], cos, sin) * (RET_DK ** -0.5)
        v = v_ref[:, h * RET_DV:(h + 1) * RET_DV]
        s = st_ref[0, h]
        qb = q.astype(BF16)
        sc = _dot_nt(qb, k.astype(BF16)) * dmask
        o = _dot(sc.astype(BF16), v) + _dot(qb, s.astype(BF16)) * q_dec
        kd_t = jnp.transpose(k * k_dec).astype(BF16)
        st_ref[0, h] = c_dec * s + _dot(kd_t, v)
        o_ref[:, h * RET_DV:(h + 1) * RET_DV] = _group_norm_gate(
            o, g_ref[:, h * RET_DV:(h + 1) * RET_DV]).astype(o_ref.dtype)


def _retention(qk, v, gate, cos, sin, bn):
    m = qk.shape[0]
    length = m // bn
    c = RET_CHUNK
    nc = length // c
    hv = RET_HEADS * RET_DV
    row = lambda b, j: (b * nc + j, 0)
    blk = [_nbytes((c, qk.shape[1]), F32), _nbytes((c, hv), BF16), _nbytes((c, hv), F32), _nbytes((c, hv), BF16),
           _nbytes((RET_HEADS, RET_DK, RET_DV), F32)]
    return pl.pallas_call(
        _ret_kernel, grid=(bn, nc),
        in_specs=[pl.BlockSpec((c, qk.shape[1]), row), pl.BlockSpec((c, hv), row), pl.BlockSpec((c, hv), row),
                  pl.BlockSpec((c, RET_DK // 2), lambda b, j: (j, 0)),
                  pl.BlockSpec((c, RET_DK // 2), lambda b, j: (j, 0))],
        out_specs=(pl.BlockSpec((c, hv), row),
                   pl.BlockSpec((1, RET_HEADS, RET_DK, RET_DV), lambda b, j: (b, 0, 0, 0))),
        out_shape=(jax.ShapeDtypeStruct((m, hv), BF16),
                   jax.ShapeDtypeStruct((bn, RET_HEADS, RET_DK, RET_DV), F32)),
        compiler_params=_cparams(("arbitrary", "arbitrary"), *blk),
        name="retention",
    )(qk, v, gate, cos, sin)


def _ret_step_kernel(qk_ref, v_ref, g_ref, cos_ref, sin_ref, s0_ref, o_ref, st_ref):
    row = pl.ds(pl.program_id(0), 1)
    cos, sin = cos_ref[...], sin_ref[...]
    for h in range(RET_HEADS):
        decay = 1.0 - 2.0 ** (-5.0 - h)
        q = _rope(qk_ref[row, h * RET_DK:(h + 1) * RET_DK], cos, sin)
        k = _rope(qk_ref[row, (RET_HEADS + h) * RET_DK:(RET_HEADS + h + 1) * RET_DK], cos, sin) * (RET_DK ** -0.5)
        v = v_ref[row, h * RET_DV:(h + 1) * RET_DV]
        s0 = s0_ref[0, h]
        st_ref[0, h] = decay * s0 + _col_of_row(k) * v
        o = (jnp.sum(q * k, axis=1, keepdims=True) * v
             + jnp.sum(_col_of_row(q) * s0, axis=0, keepdims=True) * decay)
        o_ref[0, :, h * RET_DV:(h + 1) * RET_DV] = _group_norm_gate(o, g_ref[row, h * RET_DV:(h + 1) * RET_DV])


def _retention_step(qk, v, gate, cos, sin, s0):
    bn = s0.shape[0]
    const = lambda b: (0, 0)
    state = pl.BlockSpec((1,) + s0.shape[1:], lambda b: (b, 0, 0, 0))
    return pl.pallas_call(
        _ret_step_kernel, grid=(bn,),
        in_specs=[pl.BlockSpec(qk.shape, const), pl.BlockSpec(v.shape, const), pl.BlockSpec(gate.shape, const),
                  pl.BlockSpec(cos.shape, const), pl.BlockSpec(sin.shape, const), state],
        out_specs=(pl.BlockSpec((1, 1, v.shape[1]), lambda b: (b, 0, 0)), state),
        out_shape=(jax.ShapeDtypeStruct((bn, 1, v.shape[1]), F32), jax.ShapeDtypeStruct(s0.shape, F32)),
        compiler_params=_cparams(("arbitrary",), 2 * _nbytes(s0.shape[1:], F32)),
        name="retention_step",
    )(qk, v, gate, cos, sin, s0)


def _s5_tables(a_re, a_im, log_dt, b_re, b_im, c_re, c_im, d):
    g, p = a_re.shape
    nc = b_re.shape[-1]
    no = g // S5_OCT
    t = S5_CHUNK
    dt = jnp.exp(log_dt)[:, None]
    mag = jnp.exp(a_re * dt)
    ab_re = mag * jnp.cos(a_im * dt)
    ab_im = mag * jnp.sin(a_im * dt)
    den = a_re * a_re + a_im * a_im
    x_re = ab_re - 1.0
    f_re = (x_re * a_re + ab_im * a_im) / den
    f_im = (ab_im * a_re - x_re * a_im) / den
    bb_re = f_re[..., None] * b_re - f_im[..., None] * b_im
    bb_im = f_re[..., None] * b_im + f_im[..., None] * b_re

    def powers(n):
        n = n.astype(F32)[:, None, None]
        pmag = jnp.exp(n * (a_re * dt)[None])
        return pmag * jnp.cos(n * (a_im * dt)[None]), pmag * jnp.sin(n * (a_im * dt)[None])

    pw_re, pw_im = powers(jnp.arange(t + 1))
    rev_re, rev_im = powers(t - 1 - jnp.arange(t))
    bt_re, bt_im = jnp.swapaxes(bb_re, 1, 2), jnp.swapaxes(bb_im, 1, 2)

    def packed(v_re, v_im):
        out = []
        for v in (v_re, v_im):
            v = v.reshape(v.shape[0], no, S5_OCT, nc, p)
            out.append(jnp.swapaxes(v, 0, 1).reshape(no, v.shape[0] * LANES, p))
        return jnp.concatenate(out, axis=-1)

    def b_packed(power_re, power_im):
        pr, pi = power_re[:, :, None, :], power_im[:, :, None, :]
        return packed(pr * bt_re[None] - pi * bt_im[None], pr * bt_im[None] + pi * bt_re[None])

    def c_packed(power_re, power_im):
        pr, pi = power_re[:, :, None, :], power_im[:, :, None, :]
        return packed(c_re[None] * pr - c_im[None] * pi, -(c_re[None] * pi + c_im[None] * pr))

    def state_row(v_re, v_im):
        return jnp.concatenate([v_re.reshape(no, 1, S5_OCT * p), v_im.reshape(no, 1, S5_OCT * p)], axis=-1)

    d_row = d.reshape(no, 1, LANES)
    return dict(
        b_end=b_packed(rev_re, rev_im),
        c_in=c_packed(pw_re[1:], pw_im[1:]),
        a_row=state_row(pw_re[t], pw_im[t]),
        d_row=jnp.tile(d_row, (1, 1, t)),
        b_one=b_packed(pw_re[:1], pw_im[:1]),
        c_one=c_packed(pw_re[:1], pw_im[:1]),
        a_one=state_row(ab_re, ab_im),
        d_one=d_row,
    )


def _expand_groups(packed_ref, rows):
    n = rows.stop - rows.start
    g_row = (lax.broadcasted_iota(I32, (n, LANES), 0) // S5_GROUP) % S5_OCT
    g_half = lax.broadcasted_iota(I32, (n, LANES), 1) // S5_STATE
    x = packed_ref[0, rows, :]
    swapped = pltpu.roll(x, S5_STATE, axis=1)
    tiles = []
    for src in (jnp.where(g_half == 0, x, swapped), jnp.where(g_half == 0, swapped, x)):
        for pair in range(S5_OCT // 2):
            tiles.append(jnp.where(g_row == 2 * pair + g_half, src, 0.0))
    return jnp.concatenate(tiles, axis=-1)


def _s5_kernel(u_ref, cone_ref, bend_ref, cin_ref, a_ref, d_ref, y_ref, h_ref,
               toep_scr, bfull_scr, cfull_scr, e_scr, s_scr):
    rows, t = e_scr.shape[0], S5_CHUNK

    @pl.when(pl.program_id(1) == 0)
    def _():
        toep_scr[...] = jnp.zeros_like(toep_scr)
        same_group = (lax.broadcasted_iota(I32, (LANES, LANES), 0) // S5_GROUP
                      == lax.broadcasted_iota(I32, (LANES, LANES), 1) // S5_GROUP)
        for s in range(t):
            sl = slice(s * LANES, (s + 1) * LANES)
            bfull_scr[sl, :] = _expand_groups(bend_ref, sl).astype(BF16)
            cfull_scr[sl, :] = _expand_groups(cin_ref, sl).astype(BF16)
            lag = t - 1 - s
            blk = lax.dot_general(bend_ref[0, sl, :], cone_ref[0], (((1,), (1,)), ((), ())),
                                  precision=lax.Precision.HIGHEST, preferred_element_type=F32)
            blk = jnp.where(same_group, blk, 0.0).astype(BF16)
            for r in range(t - lag):
                toep_scr[r * LANES:(r + 1) * LANES, (r + lag) * LANES:(r + lag + 1) * LANES] = blk

    u = jnp.concatenate([u_ref[0, pl.ds(i, rows, stride=t), :] for i in range(t)], axis=-1)
    ub = u.astype(BF16)
    e_scr[...] = _dot(ub, bfull_scr[...])
    half = a_ref.shape[-1] // 2
    a_re, a_im = a_ref[0, :, :half], a_ref[0, :, half:]

    def step(k, carry):
        s_re, s_im = carry
        s_scr[pl.ds(k, 1), :] = jnp.concatenate([s_re, s_im], axis=-1)
        e = e_scr[pl.ds(k, 1), :]
        return (a_re * s_re - a_im * s_im + e[:, :half], a_re * s_im + a_im * s_re + e[:, half:])

    zero = jnp.zeros((1, half), F32)
    s_re, s_im = lax.fori_loop(0, rows, step, (zero, zero))
    h_ref[0, 0] = jnp.concatenate([s_re, s_im], axis=-1)
    y = jax.nn.gelu(_dot(ub, toep_scr[...]) + _dot_nt(s_scr[...].astype(BF16), cfull_scr[...]) + d_ref[0] * u)
    for i in range(t):
        y_ref[0, pl.ds(i, rows, stride=t), :] = y[:, i * LANES:(i + 1) * LANES]


def _s5(u_oct, tb, bn):
    no, m, _ = u_oct.shape
    t = S5_CHUNK
    steps = m // bn
    rows = steps // t
    w = t * LANES
    ns = tb['a_row'].shape[-1]
    pk = tb['b_end'].shape[-1]
    oct_blk = lambda shape: pl.BlockSpec((1,) + shape, lambda o, b: (o,) + (0,) * len(shape))
    blk = [2 * _nbytes((rows, w), F32), 2 * _nbytes((w, pk), F32),
           (_nbytes((w, w), BF16) + 2 * _nbytes((w, ns), BF16) + 2 * _nbytes((rows, ns), F32)) // 2]
    return pl.pallas_call(
        _s5_kernel, grid=(no, bn),
        in_specs=[pl.BlockSpec((1, steps, LANES), lambda o, b: (o, b, 0)),
                  oct_blk((LANES, pk)), oct_blk((w, pk)), oct_blk((w, pk)), oct_blk((1, ns)),
                  oct_blk((1, w))],
        out_specs=(pl.BlockSpec((1, steps, LANES), lambda o, b: (o, b, 0)),
                   pl.BlockSpec((1, 1, 1, ns), lambda o, b: (b, o, 0, 0))),
        out_shape=(jax.ShapeDtypeStruct(u_oct.shape, F32), jax.ShapeDtypeStruct((bn, no, 1, ns), F32)),
        scratch_shapes=[pltpu.VMEM((w, w), BF16), pltpu.VMEM((w, ns), BF16), pltpu.VMEM((w, ns), BF16),
                        pltpu.VMEM((rows, ns), F32), pltpu.VMEM((rows, ns), F32)],
        compiler_params=_cparams(("arbitrary", "arbitrary"), *blk),
        name="s5_scan",
    )(u_oct, tb['c_one'], tb['b_end'], tb['c_in'], tb['a_row'], tb['d_row'])


def _s5_step_kernel(u_ref, h0_ref, b_ref, c_ref, a_ref, d_ref, y_ref, h_ref):
    u = u_ref[0]
    half = a_ref.shape[-1] // 2
    a_re, a_im = a_ref[0, :, :half], a_ref[0, :, half:]
    h0 = h0_ref[0]
    h0_re, h0_im = h0[:, :half], h0[:, half:]
    whole = slice(0, LANES)
    bu = _dot(u.astype(BF16), _expand_groups(b_ref, whole).astype(BF16))
    h_re = a_re * h0_re - a_im * h0_im + bu[:, :half]
    h_im = a_re * h0_im + a_im * h0_re + bu[:, half:]
    h = jnp.concatenate([h_re, h_im], axis=-1)
    h_ref[0] = h
    y_ref[0] = jax.nn.gelu(_dot_nt(h.astype(BF16), _expand_groups(c_ref, whole).astype(BF16)) + d_ref[0] * u)


def _s5_step(u_oct, h0, tb):
    no, rows, _ = u_oct.shape
    ns = h0.shape[-1]
    pk = tb['b_one'].shape[-1]
    o3 = lambda shape: pl.BlockSpec((1,) + shape, lambda o: (o, 0, 0))
    return pl.pallas_call(
        _s5_step_kernel, grid=(no,),
        in_specs=[o3((rows, LANES)), o3((rows, ns)), o3((LANES, pk)), o3((LANES, pk)), o3((1, ns)), o3((1, LANES))],
        out_specs=(o3((rows, LANES)), o3((rows, ns))),
        out_shape=(jax.ShapeDtypeStruct(u_oct.shape, F32), jax.ShapeDtypeStruct(h0.shape, F32)),
        compiler_params=_cparams(("arbitrary",), _nbytes((LANES, ns), BF16) * 2),
        name="s5_step",
    )(u_oct, h0, tb['b_one'], tb['c_one'], tb['a_one'], tb['d_one'])


def _sort_key(x):
    bits = pltpu.bitcast(x, I32)
    return jnp.where(bits >= 0, bits, bits ^ jnp.int32(0x7FFFFFFF))


def _kth_largest_key(count_ge, nsel, shape):
    res = jnp.where(count_ge(jnp.zeros(shape, I32)) >= nsel, jnp.int32(0), jnp.int32(INT_MIN))

    def bit_step(i, res):
        cand = res | jnp.left_shift(jnp.int32(1), 30 - i)
        return jnp.where(count_ge(cand) >= nsel, cand, res)

    return lax.fori_loop(0, 31, bit_step, res)


def _dsa_kernel(dq_ref, iq_ref, ikw_ref, ikb_ref, kv_ref, o_ref, key_scr, bias_scr, m_scr, l_scr, acc_scr,
                *, nsel):
    tq = dq_ref.shape[0]
    jq = pl.program_id(1)
    nkb = jq + 1
    krow = lax.broadcasted_iota(I32, (tq, tq), 0)
    qcol = lax.broadcasted_iota(I32, (tq, tq), 1)
    qpos = jq * tq + qcol
    hd = DSA_HEAD_DIM
    gsz = DSA_HEADS // DSA_KV_HEADS

    def over_keys(x, op):
        part = op(x.reshape(x.shape[0] // SUBLANES, SUBLANES, x.shape[1]), axis=0)
        return op(part, axis=0, keepdims=True)

    iq_st = jnp.concatenate([iq_ref[:, h * LANES:(h + 1) * LANES] for h in range(IDX_HEADS)], axis=0)
    w_t = jnp.transpose(ikw_ref[...])

    def score_block(kb, carry):
        ik = ikb_ref[pl.ds(pl.multiple_of(kb * tq, tq), tq), :]
        sh = jnp.maximum(_dot_nt(ik, iq_st), 0.0)
        acc = jnp.zeros((tq, tq), F32)
        for h in range(IDX_HEADS):
            acc = acc + sh[:, h * tq:(h + 1) * tq] * w_t[IDX_DIM + h:IDX_DIM + h + 1, :]
        acc = jnp.where(kb * tq + krow <= qpos, acc, -jnp.inf)
        key_scr[kb] = _sort_key(acc)
        return carry

    lax.fori_loop(0, nkb, score_block, 0)

    acc_rows = 4 * SUBLANES

    def count(pred):
        def body(kb, c):
            hit = jnp.where(pred(key_scr[kb]), 1.0, 0.0)
            return c + jnp.sum(hit.reshape(tq // acc_rows, acc_rows, tq), axis=0)
        part = lax.fori_loop(0, nkb, body, jnp.zeros((acc_rows, tq), F32))
        return jnp.sum(part, axis=0, keepdims=True)

    kth = _kth_largest_key(lambda cand: count(lambda key: key >= cand), float(nsel), (1, tq))
    at_least = count(lambda key: key >= kth)

    def select_all_ge():
        def block(kb, carry):
            sel = (key_scr[kb] >= kth) & (kb * tq + krow <= qpos)
            bias_scr[kb] = jnp.where(sel, 0.0, MASK_NEG)
            return carry
        lax.fori_loop(0, nkb, block, 0)

    def select_with_ties():
        need = float(nsel) - count(lambda key: key > kth)
        tri = jnp.where(krow >= qcol, 1.0, 0.0).astype(BF16)

        def block(kb, taken):
            key = key_scr[kb]
            tie = jnp.where(key == kth, 1.0, 0.0)
            rank = taken + _dot(tri, tie.astype(BF16))
            sel = ((key > kth) | ((tie > 0.0) & (rank <= need))) & (kb * tq + krow <= qpos)
            bias_scr[kb] = jnp.where(sel, 0.0, MASK_NEG)
            return taken + over_keys(tie, jnp.sum)

        lax.fori_loop(0, nkb, block, jnp.zeros((1, tq), F32))

    lax.cond(jnp.max(at_least) > float(nsel), select_with_ties, select_all_ge)

    scale2 = (hd ** -0.5) * math.log2(math.e)
    qgs = [jnp.concatenate([dq_ref[:, (g * gsz + i) * hd:(g * gsz + i + 1) * hd] for i in range(gsz)], axis=0)
           for g in range(DSA_KV_HEADS)]
    m_scr[...] = jnp.full_like(m_scr, -jnp.inf)
    l_scr[...] = jnp.zeros_like(l_scr)
    acc_scr[...] = jnp.zeros_like(acc_scr)

    def attend_block(kb, carry):
        ks = pl.ds(pl.multiple_of(kb * tq, tq), tq)
        bias = jnp.concatenate([bias_scr[kb]] * gsz, axis=1)
        for g in range(DSA_KV_HEADS):
            kblk = kv_ref[ks, g * hd:(g + 1) * hd]
            vblk = kv_ref[ks, (DSA_KV_HEADS + g) * hd:(DSA_KV_HEADS + g + 1) * hd]
            lg = _dot_nt(kblk, qgs[g]) * scale2 + bias
            m_old = m_scr[g]
            m_new = jnp.maximum(m_old, over_keys(lg, jnp.max))
            alpha = jnp.exp2(m_old - m_new)
            p = jnp.exp2(lg - m_new)
            l_scr[g] = alpha * l_scr[g] + over_keys(p, jnp.sum)
            v_t = jnp.transpose(vblk.astype(F32)).astype(BF16)
            acc_scr[g] = alpha * acc_scr[g] + _dot(v_t, p.astype(BF16))
            m_scr[g] = m_new
        return carry

    lax.fori_loop(0, nkb, attend_block, 0)
    for g in range(DSA_KV_HEADS):
        out_t = acc_scr[g] / l_scr[g]
        for i in range(gsz):
            o_ref[:, (g * gsz + i) * hd:(g * gsz + i + 1) * hd] = jnp.transpose(
                out_t[:, i * tq:(i + 1) * tq]).astype(o_ref.dtype)


def _dsa_prompt(dq, iq, ikw, ikb, kvb, bn):
    m = dq.shape[0]
    length = m // bn
    tq = 256
    nq = length // tq
    nsel = min(DSA_TOPK, length // 4)
    qrow = lambda b, j: (b * nq + j, 0)
    full = lambda b, j: (b, 0)
    gsz = DSA_HEADS // DSA_KV_HEADS
    blk = [_nbytes((tq, dq.shape[1]), BF16) * 3, _nbytes((length, LANES), BF16), _nbytes((length, kvb.shape[1]), BF16),
           _nbytes((nq, tq, tq), F32)]
    return pl.pallas_call(
        functools.partial(_dsa_kernel, nsel=nsel), grid=(bn, nq),
        in_specs=[pl.BlockSpec((tq, dq.shape[1]), qrow), pl.BlockSpec((tq, iq.shape[1]), qrow),
                  pl.BlockSpec((tq, LANES), qrow), pl.BlockSpec((length, LANES), full),
                  pl.BlockSpec((length, kvb.shape[1]), full)],
        out_specs=pl.BlockSpec((tq, dq.shape[1]), qrow),
        out_shape=jax.ShapeDtypeStruct(dq.shape, BF16),
        scratch_shapes=[pltpu.VMEM((nq, tq, tq), I32), pltpu.VMEM((nq, tq, tq), F32),
                        pltpu.VMEM((DSA_KV_HEADS, 1, gsz * tq), F32), pltpu.VMEM((DSA_KV_HEADS, 1, gsz * tq), F32),
                        pltpu.VMEM((DSA_KV_HEADS, DSA_HEAD_DIM, gsz * tq), F32)],
        compiler_params=_cparams(("arbitrary", "arbitrary"), *blk),
        name="dsa_prompt",
    )(dq, iq, ikw, ikb, kvb)


MAX_PAGES_PER_STEP = 32


def _idx_heads(iq_row):
    return jnp.concatenate([iq_row[:, h * LANES:h * LANES + IDX_DIM] for h in range(IDX_HEADS)], axis=0)


def _page_score_kernel(pt_ref, iq_ref, ikw_ref, *rest):
    page_refs, o_ref = rest[:-1], rest[-1]
    b = pl.program_id(0)
    iq_h = jnp.concatenate([_idx_heads(iq_ref[pl.ds(b, 1), :]),
                            jnp.zeros((SAMPLE_ROWS - IDX_HEADS, IDX_DIM), F32)], axis=0).astype(BF16)
    w_col = _col_of_row(ikw_ref[pl.ds(b, 1), IDX_DIM:IDX_DIM + SAMPLE_ROWS])
    pages = jnp.concatenate([page[...].astype(BF16) for page in page_refs], axis=1)
    sh = jnp.maximum(_dot(iq_h, pages), 0.0)
    sc = jnp.sum(sh * w_col, axis=0, keepdims=True)
    for i in range(len(page_refs)):
        o_ref[0, i:i + 1, :] = sc[:, i * PAGE_SIZE:(i + 1) * PAGE_SIZE]


def _page_scores(page_table, iq, ikw, cache_idx_t, layer):
    bn, npages = page_table.shape
    pg = min(MAX_PAGES_PER_STEP, npages)
    const = lambda b, s, pt: (0, 0)

    def page_spec(i):
        return pl.BlockSpec((None, None, IDX_DIM, PAGE_SIZE), lambda b, s, pt: (layer, pt[b, s * pg + i], 0, 0))

    return pl.pallas_call(
        _page_score_kernel,
        grid_spec=pltpu.PrefetchScalarGridSpec(
            num_scalar_prefetch=1, grid=(bn, npages // pg),
            in_specs=[pl.BlockSpec(iq.shape, const), pl.BlockSpec(ikw.shape, const)]
            + [page_spec(i) for i in range(pg)],
            out_specs=pl.BlockSpec((1, pg, PAGE_SIZE), lambda b, s, pt: (b, s, 0))),
        out_shape=jax.ShapeDtypeStruct((bn, npages, PAGE_SIZE), F32),
        compiler_params=_cparams(("arbitrary", "arbitrary")),
        name="page_scores",
    )(page_table, iq, ikw, *([cache_idx_t] * pg))


def _page_select_kernel(sc_ref, iq_ref, ikw_ref, bias_ref, bias_self_ref, *, nsel):
    b = pl.program_id(0)
    npages, psz = sc_ref.shape[1:]
    iq_h = _idx_heads(iq_ref[pl.ds(b, 1), :])
    ikw = ikw_ref[pl.ds(b, 1), :]
    w_col = _col_of_row(ikw[:, IDX_DIM:IDX_DIM + IDX_HEADS])
    s_self = jnp.sum(jnp.maximum(jnp.sum(iq_h * ikw[:, :IDX_DIM], axis=1, keepdims=True), 0.0) * w_col,
                     axis=0, keepdims=True)
    key = _sort_key(sc_ref[0])
    key_self = _sort_key(s_self)

    def total(x):
        return jnp.sum(jnp.sum(x, axis=1, keepdims=True), axis=0, keepdims=True)

    def count_ge(cand):
        return total(jnp.where(key >= cand, 1.0, 0.0)) + jnp.where(key_self >= cand, 1.0, 0.0)

    kth = _kth_largest_key(count_ge, float(nsel), (1, 1))
    need = float(nsel) - (total(jnp.where(key > kth, 1.0, 0.0)) + jnp.where(key_self > kth, 1.0, 0.0))
    tie = jnp.where(key == kth, 1.0, 0.0)
    r_in = lax.broadcasted_iota(I32, (psz, psz), 0)
    c_in = lax.broadcasted_iota(I32, (psz, psz), 1)
    in_page = _dot(tie.astype(BF16), jnp.where(r_in <= c_in, 1.0, 0.0).astype(BF16))
    per_page = jnp.broadcast_to(jnp.sum(tie, axis=1, keepdims=True), (npages, psz)).astype(BF16)
    r_pg = lax.broadcasted_iota(I32, (npages, npages), 0)
    c_pg = lax.broadcasted_iota(I32, (npages, npages), 1)
    before = _dot(jnp.where(c_pg < r_pg, 1.0, 0.0).astype(BF16), per_page)
    sel = (key > kth) | ((tie > 0.0) & (before + in_page <= need))
    r_dup = lax.broadcasted_iota(I32, (psz, bias_ref.shape[2]), 0)
    c_dup = lax.broadcasted_iota(I32, (psz, bias_ref.shape[2]), 1)
    dup = jnp.where(c_dup // DSA_KV_HEADS == r_dup, 1.0, 0.0).astype(BF16)
    spread = _dot(jnp.where(sel, 1.0, 0.0).astype(BF16), dup)
    bias_ref[0] = jnp.where(spread > 0.5, 0.0, MASK_NEG)
    sel_self = (key_self > kth) | ((key_self == kth) & (total(tie) + 1.0 <= need))
    bias_self_ref[0] = jnp.broadcast_to(jnp.where(sel_self, 0.0, MASK_NEG), (1, LANES))


def _page_select(scores, iq, ikw, nsel):
    bn, npages, psz = scores.shape
    const = lambda b: (0, 0)
    wide = DSA_KV_HEADS * psz
    return pl.pallas_call(
        functools.partial(_page_select_kernel, nsel=nsel), grid=(bn,),
        in_specs=[pl.BlockSpec((1, npages, psz), lambda b: (b, 0, 0)),
                  pl.BlockSpec(iq.shape, const), pl.BlockSpec(ikw.shape, const)],
        out_specs=(pl.BlockSpec((1, npages, wide), lambda b: (b, 0, 0)),
                   pl.BlockSpec((1, 1, LANES), lambda b: (b, 0, 0))),
        out_shape=(jax.ShapeDtypeStruct((bn, npages, wide), F32), jax.ShapeDtypeStruct((bn, 1, LANES), F32)),
        compiler_params=_cparams(("arbitrary",)),
        name="page_select",
    )(scores, iq, ikw)


def _page_attend_kernel(pt_ref, dq_ref, kvs_ref, bias_ref, bself_ref, *rest, npg):
    k_refs, v_refs = rest[:npg], rest[npg:2 * npg]
    o_ref, m_scr, l_scr, acc_scr = rest[2 * npg:]
    b = pl.program_id(0)
    s = pl.program_id(1)
    hd = DSA_HEAD_DIM
    gsz = DSA_HEADS // DSA_KV_HEADS
    scale = hd ** -0.5
    prow = k_refs[0].shape[0]
    dq_row = dq_ref[pl.ds(b, 1), :]
    q = jnp.concatenate([dq_row[:, h * hd:(h + 1) * hd] for h in range(DSA_HEADS)]
                        + [jnp.zeros((SAMPLE_ROWS - DSA_HEADS, hd), F32)], axis=0)
    kv_of_head = lax.broadcasted_iota(I32, (SAMPLE_ROWS, 1), 0) // gsz

    @pl.when(s == 0)
    def _():
        m_scr[...] = jnp.full_like(m_scr, -jnp.inf)
        l_scr[...] = jnp.zeros_like(l_scr)
        acc_scr[...] = jnp.zeros_like(acc_scr)

    def update(lg, pv_of):
        m_old = m_scr[...]
        m_new = jnp.maximum(m_old, jnp.max(lg, axis=1, keepdims=True))
        alpha = jnp.exp(m_old - m_new)
        p = jnp.exp(lg - m_new)
        l_scr[...] = alpha * l_scr[...] + jnp.sum(p, axis=1, keepdims=True)
        acc_scr[...] = alpha * acc_scr[...] + pv_of(p)
        m_scr[...] = m_new

    qb = q.astype(BF16)
    lg = jnp.concatenate([_dot_nt(qb, k_refs[i][...].astype(BF16)) for i in range(npg)], axis=1) * scale
    bias = jnp.concatenate([bias_ref[0, i:i + 1, :] for i in range(npg)], axis=1)
    col = lax.broadcasted_iota(I32, lg.shape, 1)
    lg = jnp.where(jnp.bitwise_and(col, DSA_KV_HEADS - 1) == kv_of_head, lg + bias, MASK_NEG)

    def pv_pages(p):
        pb = p.astype(BF16)
        acc = jnp.zeros((SAMPLE_ROWS, hd), F32)
        for i in range(npg):
            acc = acc + _dot(pb[:, i * prow:(i + 1) * prow], v_refs[i][...].astype(BF16))
        return acc

    update(lg, pv_pages)

    @pl.when(s == pl.num_programs(1) - 1)
    def _():
        kvs = kvs_ref[pl.ds(b, 1), :]
        k_self = jnp.where(kv_of_head == 0, kvs[:, :hd], kvs[:, hd:2 * hd])
        v_self = jnp.where(kv_of_head == 0, kvs[:, 2 * hd:3 * hd], kvs[:, 3 * hd:])
        lg_self = jnp.sum(q * k_self, axis=1, keepdims=True) * scale + bself_ref[0, :, :1]
        update(lg_self, lambda p: p * v_self)
        out = acc_scr[...] / l_scr[...]
        for h in range(DSA_HEADS):
            o_ref[0, :, h * hd:(h + 1) * hd] = out[h:h + 1]


def _page_attend(page_table, dq, kvs, bias, bias_self, cache_k, cache_v, layer):
    assert DSA_KV_HEADS == 2
    bn, npages = page_table.shape
    pg = min(MAX_PAGES_PER_STEP, npages)
    prow, hd = cache_k.shape[-2:]
    const = lambda b, s, pt: (0, 0)

    def page_spec(i):
        return pl.BlockSpec((None, None, prow, hd), lambda b, s, pt: (layer, pt[b, s * pg + i], 0, 0))

    return pl.pallas_call(
        functools.partial(_page_attend_kernel, npg=pg),
        grid_spec=pltpu.PrefetchScalarGridSpec(
            num_scalar_prefetch=1, grid=(bn, npages // pg),
            in_specs=[pl.BlockSpec(dq.shape, const), pl.BlockSpec(kvs.shape, const),
                      pl.BlockSpec((1, pg, prow), lambda b, s, pt: (b, s, 0)),
                      pl.BlockSpec((1, 1, LANES), lambda b, s, pt: (b, 0, 0))]
            + [page_spec(i) for i in range(pg)] + [page_spec(i) for i in range(pg)],
            out_specs=pl.BlockSpec((1, 1, dq.shape[1]), lambda b, s, pt: (b, 0, 0)),
            scratch_shapes=[pltpu.VMEM((SAMPLE_ROWS, 1), F32), pltpu.VMEM((SAMPLE_ROWS, 1), F32),
                            pltpu.VMEM((SAMPLE_ROWS, hd), F32)]),
        out_shape=jax.ShapeDtypeStruct((bn, 1, dq.shape[1]), F32),
        compiler_params=_cparams(("arbitrary", "arbitrary"), 2 * pg * _nbytes((prow, hd), F32)),
        name="page_attend",
    )(page_table, dq, kvs, bias, bias_self, *([cache_k] * pg), *([cache_v] * pg))


def _rope_tables(pos):
    half = RET_DK // 2
    inv = 1.0 / (ROPE_BASE ** jnp.linspace(0.0, 1.0, half, dtype=F32))
    ang = pos.astype(F32)[:, None] * inv[None, :]
    return jnp.cos(ang), jnp.sin(ang)


def _stacked_bf16(a):
    arr = a.astype(BF16)
    return lambda l: _LayerWeight(arr, l, 0, arr.shape[2])


def _in_proj_weights(w_in):
    depth, d, _ = w_in.shape
    hq = RET_HEADS * RET_DK
    hv = RET_HEADS * RET_DV
    o = 0
    cuts = {}
    for name, width in (('qk', 2 * hq), ('v', hv), ('g', hv)):
        cuts[name] = (o, width)
        o += width
    s5w = (w_in.shape[2] - 2 * hq - 2 * hv - DSA_HEADS * DSA_HEAD_DIM - 2 * DSA_KV_HEADS * DSA_HEAD_DIM
           - IDX_HEADS * IDX_DIM - IDX_DIM - IDX_HEADS) // (1 + 2 * N_BRANCHES)
    for name, width in (('su', s5w), ('dq', DSA_HEADS * DSA_HEAD_DIM), ('kv', 2 * DSA_KV_HEADS * DSA_HEAD_DIM),
                        ('iq', IDX_HEADS * IDX_DIM), ('ikw', IDX_DIM + IDX_HEADS), ('gates', N_BRANCHES * d)):
        cuts[name] = (o, width)
        o += width
    assert o == w_in.shape[2]
    cut = lambda name: w_in[:, :, cuts[name][0]:cuts[name][0] + cuts[name][1]]
    iq = jnp.pad(cut('iq').reshape(depth, d, IDX_HEADS, IDX_DIM), ((0, 0), (0, 0), (0, 0), (0, LANES - IDX_DIM)))
    repacked = {'iq': iq.reshape(depth, d, IDX_HEADS * LANES),
                'ikw': jnp.pad(cut('ikw'), ((0, 0), (0, 0), (0, LANES - IDX_DIM - IDX_HEADS))),
                'gates': cut('gates')}

    def layer(l):
        wb = {k: _LayerWeight(w_in, l, *cuts[k]) for k in ('qk', 'v', 'g', 'su', 'dq', 'kv')}
        wb.update({k: _LayerWeight(a, l, 0, a.shape[2]) for k, a in repacked.items()})
        return wb

    return layer


def _project(xb, wb, sample):
    act = F32 if sample else BF16
    p = {
        'qk': _mm(xb, wb['qk'], (F32,), name="proj_qk"),
        'v': _mm(xb, wb['v'], (act,), name="proj_v"),
        'g': _mm(xb, wb['g'], (F32,), name="proj_g"),
        'su': _mm(xb, wb['su'], None, oct_layout=True, name="proj_su"),
        'dq': _mm(xb, wb['dq'], (act,), name="proj_dq"),
        'iq': _mm(xb, wb['iq'], (act,), name="proj_iq"),
        'gates': _mm(xb, wb['gates'], (BF16,), gate=True, name="proj_gates"),
    }
    p['kv'], p['kvb'] = _mm(xb, wb['kv'], (F32, BF16), name="proj_kv")
    p['ikw'], p['ikb'] = _mm(xb, wb['ikw'], (F32, BF16), name="proj_ikw")
    return p


def _s5_state_out(h, groups):
    bn = h.shape[0]
    h = h.reshape(bn, groups // S5_OCT, 2, S5_OCT, S5_STATE)
    return h[:, :, 0].reshape(bn, groups, S5_STATE), h[:, :, 1].reshape(bn, groups, S5_STATE)


def kernel(x_prompt, x_sample, cache_k, cache_v, cache_idx_k, state_ret, state_s5_re, state_s5_im, page_table, ln1_g, ln1_b, ffn1_wg, ffn1_wu, ffn1_wd, w_in, s5_a_re, s5_a_im, s5_log_dt, s5_b_re, s5_b_im, s5_c_re, s5_c_im, s5_d, w_glu, w_ret_o, w_s5_o, w_dsa_o, w_out, ln2_g, ln2_b, ffn2_wg, ffn2_wu, ffn2_wd, ln3_g, ln3_b):
    bp, lp, d = x_prompt.shape
    bs, ls, _ = x_sample.shape
    depth = w_in.shape[0]
    assert ls == 1 and bs <= SAMPLE_ROWS
    npages = page_table.shape[1]
    past = npages * PAGE_SIZE
    groups = s5_a_re.shape[1]
    alpha = (2 * depth) ** 0.25
    kvw = DSA_KV_HEADS * DSA_HEAD_DIM
    nsel_s = min(DSA_TOPK, (past + ls) // 4)

    cos_p, sin_p = _rope_tables(jnp.arange(lp))
    cos_s, sin_s = _rope_tables(past + jnp.arange(ls))
    cache_k = cache_k.reshape(depth, -1, PAGE_SIZE * DSA_KV_HEADS, DSA_HEAD_DIM)
    cache_v = cache_v.reshape(depth, -1, PAGE_SIZE * DSA_KV_HEADS, DSA_HEAD_DIM)
    cache_idx_t = jnp.swapaxes(cache_idx_k, 2, 3)

    in_proj = _in_proj_weights(w_in.astype(BF16))
    ffn1 = [_stacked_bf16(a) for a in (ffn1_wg, ffn1_wu, ffn1_wd)]
    ffn2 = [_stacked_bf16(a) for a in (ffn2_wg, ffn2_wu, ffn2_wd)]
    mixer_out = [_stacked_bf16(a) for a in (w_glu, w_ret_o, w_s5_o, w_dsa_o, w_out)]

    xp = x_prompt.reshape(bp * lp, d)
    xs = jnp.pad(x_sample.reshape(bs, d), ((0, SAMPLE_ROWS - bs), (0, 0)))
    outs_p, outs_s = [], []
    for l in range(depth):
        wb = in_proj(l)
        tb = _s5_tables(s5_a_re[l], s5_a_im[l], s5_log_dt[l], s5_b_re[l], s5_b_im[l], s5_c_re[l], s5_c_im[l],
                        s5_d[l])
        w1 = [w(l) for w in ffn1]
        w2 = [w(l) for w in ffn2]
        wglu, wro, wso, wdo, wo = [w(l) for w in mixer_out]

        xp, xpb = _ffn_ln(xp, *w1, ln1_g[l], ln1_b[l], alpha)
        p = _project(xpb, wb, sample=False)
        o_ret, ret_p = _retention(p['qk'], p['v'], p['g'], cos_p, sin_p, bp)
        y_s5, h_p = _s5(p['su'], tb, bp)
        z = _glu(y_s5, wglu)
        att = _dsa_prompt(p['dq'], p['iq'], p['ikw'], p['ikb'], p['kvb'], bp)
        merged = _merge(o_ret, z, att, p['gates'], wro, wso, wdo)
        xp, xpb = _out_ln(xp, merged, wo, ln2_g[l], ln2_b[l], alpha)
        xp, xpb = _ffn_ln(xp, *w2, ln3_g[l], ln3_b[l], alpha)
        s5r_p, s5i_p = _s5_state_out(h_p, groups)
        outs_p.append((p['kv'][:, :kvw].reshape(bp, lp, DSA_KV_HEADS, DSA_HEAD_DIM),
                       p['kv'][:, kvw:].reshape(bp, lp, DSA_KV_HEADS, DSA_HEAD_DIM),
                       p['ikw'][:, :IDX_DIM].reshape(bp, lp, IDX_DIM), ret_p, s5r_p, s5i_p))

        xs, xsb = _ffn_ln(xs, *w1, ln1_g[l], ln1_b[l], alpha)
        q = _project(xsb, wb, sample=True)
        o_ret_s, ret_s = _retention_step(q['qk'], q['v'], q['g'], cos_s, sin_s, state_ret[l])
        o_ret_s = jnp.pad(o_ret_s[:, 0], ((0, SAMPLE_ROWS - bs), (0, 0))).astype(BF16)
        h0 = jnp.concatenate([state_s5_re[l].reshape(bs, groups // S5_OCT, S5_OCT * S5_STATE),
                              state_s5_im[l].reshape(bs, groups // S5_OCT, S5_OCT * S5_STATE)], axis=-1)
        h0 = jnp.pad(jnp.swapaxes(h0, 0, 1), ((0, 0), (0, SAMPLE_ROWS - bs), (0, 0)))
        y_s, h_s = _s5_step(q['su'], h0, tb)
        z_s = _glu(y_s, wglu)
        scores = _page_scores(page_table, q['iq'], q['ikw'], cache_idx_t, l)
        bias, bias_self = _page_select(scores, q['iq'], q['ikw'], nsel_s)
        att_s = _page_attend(page_table, q['dq'], q['kv'], bias, bias_self, cache_k, cache_v, l)
        att_s = jnp.pad(att_s[:, 0], ((0, SAMPLE_ROWS - bs), (0, 0))).astype(BF16)
        merged_s = _merge(o_ret_s, z_s, att_s, q['gates'], wro, wso, wdo)
        xs, xsb = _out_ln(xs, merged_s, wo, ln2_g[l], ln2_b[l], alpha)
        xs, xsb = _ffn_ln(xs, *w2, ln3_g[l], ln3_b[l], alpha)
        s5r_s, s5i_s = _s5_state_out(jnp.swapaxes(h_s, 0, 1)[:bs, :, None, :], groups)
        outs_s.append((q['kv'][:bs, :kvw].reshape(bs, ls, DSA_KV_HEADS, DSA_HEAD_DIM),
                       q['kv'][:bs, kvw:].reshape(bs, ls, DSA_KV_HEADS, DSA_HEAD_DIM),
                       q['ikw'][:bs, :IDX_DIM].reshape(bs, ls, IDX_DIM), ret_s, s5r_s, s5i_s))

    k_p, v_p, ik_p, ret_p, s5r_p, s5i_p = [jnp.stack(a) for a in zip(*outs_p)]
    k_s, v_s, ik_s, ret_s, s5r_s, s5i_s = [jnp.stack(a) for a in zip(*outs_s)]
    return (xp.reshape(bp, lp, d), xs[:bs].reshape(bs, ls, d), k_p, v_p, ik_p, k_s, v_s, ik_s,
            ret_p, ret_s, s5r_p, s5i_p, s5r_s, s5i_s)
```

```python
import functools
import math
from typing import NamedTuple

import numpy as np
import jax
import jax.numpy as jnp
from jax import lax
from jax.experimental import pallas as pl
from jax.experimental.pallas import tpu as pltpu

F32 = jnp.float32
BF16 = jnp.bfloat16
I32 = jnp.int32

PAGE_SIZE = 128
RET_HEADS = 4
RET_DK = 256
RET_DV = 512
RET_CHUNK = 256
ROPE_BASE = 10000.0
S5_GROUP = 16
S5_STATE = 64
DSA_HEADS = 8
DSA_KV_HEADS = 2
DSA_HEAD_DIM = 128
IDX_HEADS = 8
IDX_DIM = 64
DSA_TOPK = 256
LN_EPS = 1e-5
GN_EPS = 1e-5
N_BRANCHES = 3

LANES = 128
SUBLANES = 8
VMEM_LIMIT_CAP = 56 * 1024 * 1024
S5_CHUNK = 16
S5_OCT = LANES // S5_GROUP
SAMPLE_ROWS = 16
MASK_NEG = -1e30
INT_MIN = -2 ** 31


def _cparams(semantics, *block_bytes):
    est = 2 * sum(block_bytes) + (8 << 20)
    return pltpu.CompilerParams(dimension_semantics=semantics,
                                vmem_limit_bytes=int(min(max(est, 32 << 20), VMEM_LIMIT_CAP)))


def _nbytes(shape, dtype):
    return int(np.prod(shape)) * jnp.dtype(dtype).itemsize


class _LayerWeight(NamedTuple):
    arr: jax.Array
    layer: int
    col0: int
    ncols: int

    @property
    def shape(self):
        return (self.arr.shape[1], self.ncols)

    def spec(self, blk, idx):
        assert self.col0 % blk[1] == 0
        layer, off = self.layer, self.col0 // blk[1]

        def index_map(*g):
            r, c = idx(*g)
            return (layer, r, c + off)

        return pl.BlockSpec((None,) + tuple(blk), index_map)


def _dot(a, b):
    return jnp.dot(a, b, preferred_element_type=F32)


def _dot_nt(a, b):
    return lax.dot_general(a, b, (((1,), (1,)), ((), ())), preferred_element_type=F32)


def _layernorm(y, g, b):
    mu = jnp.mean(y, axis=-1, keepdims=True)
    yc = y - mu
    var = jnp.mean(yc * yc, axis=-1, keepdims=True)
    return yc * lax.rsqrt(var + LN_EPS) * g + b


def _col_of_row(r):
    n = r.shape[1]
    eye = lax.broadcasted_iota(I32, (n, n), 0) == lax.broadcasted_iota(I32, (n, n), 1)
    return jnp.sum(jnp.where(eye, jnp.broadcast_to(r, (n, n)), 0.0), axis=1, keepdims=True)


def _mm_kernel(x_ref, w_ref, *o_refs, gate):
    acc = _dot(x_ref[...], w_ref[...])
    if gate:
        acc = jax.nn.sigmoid(acc)
    for o_ref in o_refs:
        o_ref[...] = acc.astype(o_ref.dtype)


def _mm_oct_kernel(x_ref, w_ref, o_ref):
    acc = _dot(x_ref[...], w_ref[...])
    for i in range(o_ref.shape[0]):
        o_ref[i] = acc[:, i * LANES:(i + 1) * LANES]


def _mm(x, w, out_dtypes, oct_layout=False, gate=False, name="mm"):
    m, k = x.shape
    n = w.shape[1]
    tm = min(m, 1024)
    tn = n if n <= 1024 else (1024 if n % 1024 == 0 else 512)
    while w.col0 % tn:
        tn //= 2
    assert m % tm == 0 and n % tn == 0 and tn % LANES == 0
    w_resident = (n // tn) * m * k + k * n <= (m // tm) * k * n + m * k
    if w_resident:
        grid = (n // tn, m // tm)
        xi, wi, oi = (lambda j, i: (i, 0)), (lambda j, i: (0, j)), (lambda j, i: (i, j))
        ooct = lambda j, i: (j, i, 0)
    else:
        grid = (m // tm, n // tn)
        xi, wi, oi = (lambda i, j: (i, 0)), (lambda i, j: (0, j)), (lambda i, j: (i, j))
        ooct = lambda i, j: (j, i, 0)
    in_specs = [pl.BlockSpec((tm, k), xi), w.spec((k, tn), wi)]
    blk = [_nbytes((tm, k), BF16), _nbytes((k, tn), BF16)]
    if oct_layout:
        out_shape = jax.ShapeDtypeStruct((n // LANES, m, LANES), F32)
        out_specs = pl.BlockSpec((tn // LANES, tm, LANES), ooct)
        body = _mm_oct_kernel
        blk.append(_nbytes((tm, tn), F32))
    else:
        out_shape = tuple(jax.ShapeDtypeStruct((m, n), d) for d in out_dtypes)
        out_specs = tuple(pl.BlockSpec((tm, tn), oi) for _ in out_dtypes)
        body = functools.partial(_mm_kernel, gate=gate)
        blk += [_nbytes((tm, tn), d) for d in out_dtypes]
    out = pl.pallas_call(body, grid=grid, in_specs=in_specs, out_specs=out_specs, out_shape=out_shape,
                         compiler_params=_cparams(("arbitrary", "arbitrary"), *blk), name=name)(x, w.arr)
    return out if oct_layout else (out[0] if len(out_dtypes) == 1 else out)


def _ffn_kernel(x_ref, wg_ref, wu_ref, wd_ref, g_ref, b_ref, o_ref, ob_ref, xb_scr, acc_scr, *, alpha):
    f = pl.program_id(1)

    @pl.when(f == 0)
    def _():
        xb_scr[...] = x_ref[...].astype(BF16)
        acc_scr[...] = jnp.zeros_like(acc_scr)

    xb = xb_scr[...]
    hg = _dot(xb, wg_ref[...])
    hu = _dot(xb, wu_ref[...])
    h = hg * jax.nn.sigmoid(hg) * hu
    acc_scr[...] += _dot(h.astype(BF16), wd_ref[...])

    @pl.when(f == pl.num_programs(1) - 1)
    def _():
        y = _layernorm(alpha * x_ref[...] + 0.5 * acc_scr[...], g_ref[...], b_ref[...])
        o_ref[...] = y
        ob_ref[...] = y.astype(BF16)


def _ffn_ln(x, wg, wu, wd, g, b, alpha):
    m, d = x.shape
    dff = wg.shape[1]
    tm = min(m, 512)
    tf = 512
    assert m % tm == 0 and dff % tf == 0
    row = lambda i, f: (i, 0)
    blk = [_nbytes((tm, d), F32), 3 * _nbytes((d, tf), BF16), _nbytes((tm, d), F32), _nbytes((tm, d), BF16),
           _nbytes((tm, d), F32)]
    return pl.pallas_call(
        functools.partial(_ffn_kernel, alpha=alpha),
        grid=(m // tm, dff // tf),
        in_specs=[pl.BlockSpec((tm, d), row),
                  wg.spec((d, tf), lambda i, f: (0, f)),
                  wu.spec((d, tf), lambda i, f: (0, f)),
                  wd.spec((tf, d), lambda i, f: (f, 0)),
                  pl.BlockSpec((1, d), lambda i, f: (0, 0)),
                  pl.BlockSpec((1, d), lambda i, f: (0, 0))],
        out_specs=(pl.BlockSpec((tm, d), row), pl.BlockSpec((tm, d), row)),
        out_shape=(jax.ShapeDtypeStruct((m, d), F32), jax.ShapeDtypeStruct((m, d), BF16)),
        scratch_shapes=[pltpu.VMEM((tm, d), BF16), pltpu.VMEM((tm, d), F32)],
        compiler_params=_cparams(("arbitrary", "arbitrary"), *blk),
        name="ffn_ln",
    )(x, wg.arr, wu.arr, wd.arr, g.reshape(1, d), b.reshape(1, d))


def _glu_kernel(y_ref, w_ref, o_ref):
    y = jnp.concatenate([y_ref[i] for i in range(y_ref.shape[0])], axis=-1)
    o_ref[...] = (y * jax.nn.sigmoid(_dot(y.astype(BF16), w_ref[...]))).astype(o_ref.dtype)


def _glu(y_oct, w):
    no, m, _ = y_oct.shape
    n = w.shape[1]
    tm = min(m, 512)
    blk = [_nbytes((no, tm, LANES), F32), _nbytes(w.shape, BF16), _nbytes((tm, n), BF16)]
    return pl.pallas_call(
        _glu_kernel, grid=(m // tm,),
        in_specs=[pl.BlockSpec((no, tm, LANES), lambda i: (0, i, 0)), w.spec(w.shape, lambda i: (0, 0))],
        out_specs=pl.BlockSpec((tm, n), lambda i: (i, 0)),
        out_shape=jax.ShapeDtypeStruct((m, n), BF16),
        compiler_params=_cparams(("arbitrary",), *blk),
        name="s5_glu",
    )(y_oct, w.arr)


def _merge_kernel(o_ref, z_ref, a_ref, g0_ref, g1_ref, g2_ref, wr_ref, ws_ref, wd_ref, m_ref):
    ret = _dot(o_ref[...], wr_ref[...])
    s5 = _dot(z_ref[...], ws_ref[...])
    dsa = _dot(a_ref[...], wd_ref[...])
    merged = g0_ref[...].astype(F32) * ret + g1_ref[...].astype(F32) * s5 + g2_ref[...].astype(F32) * dsa
    m_ref[...] = merged.astype(m_ref.dtype)


def _merge(o, z, a, gates, wr, ws, wd):
    m = o.shape[0]
    d = wr.shape[1]
    tm = min(m, 512)
    tn = min(d, 1024)
    nb = d // tn
    blk = [_nbytes((tm, o.shape[1]), BF16), 2 * _nbytes((tm, z.shape[1]), BF16), 3 * _nbytes((tm, tn), BF16),
           _nbytes((o.shape[1], tn), BF16), 2 * _nbytes((z.shape[1], tn), BF16), _nbytes((tm, tn), BF16)]
    gate_spec = lambda br: pl.BlockSpec((tm, tn), lambda i, j: (i, br * nb + j))
    return pl.pallas_call(
        _merge_kernel, grid=(m // tm, nb),
        in_specs=[pl.BlockSpec((tm, o.shape[1]), lambda i, j: (i, 0)),
                  pl.BlockSpec((tm, z.shape[1]), lambda i, j: (i, 0)),
                  pl.BlockSpec((tm, a.shape[1]), lambda i, j: (i, 0)),
                  gate_spec(0), gate_spec(1), gate_spec(2),
                  wr.spec((wr.shape[0], tn), lambda i, j: (0, j)),
                  ws.spec((ws.shape[0], tn), lambda i, j: (0, j)),
                  wd.spec((wd.shape[0], tn), lambda i, j: (0, j))],
        out_specs=pl.BlockSpec((tm, tn), lambda i, j: (i, j)),
        out_shape=jax.ShapeDtypeStruct((m, d), BF16),
        compiler_params=_cparams(("arbitrary", "arbitrary"), *blk),
        name="branch_merge",
    )(o, z, a, gates, gates, gates, wr.arr, ws.arr, wd.arr)


def _out_ln_kernel(x_ref, m_ref, w_ref, g_ref, b_ref, o_ref, ob_ref, *, alpha):
    y = _layernorm(alpha * x_ref[...] + _dot(m_ref[...], w_ref[...]), g_ref[...], b_ref[...])
    o_ref[...] = y
    ob_ref[...] = y.astype(BF16)


def _out_ln(x, merged, w, g, b, alpha):
    m, d = x.shape
    tm = min(m, 512)
    row = lambda i: (i, 0)
    blk = [2 * _nbytes((tm, d), F32), 2 * _nbytes((tm, d), BF16), _nbytes((d, d), BF16)]
    return pl.pallas_call(
        functools.partial(_out_ln_kernel, alpha=alpha), grid=(m // tm,),
        in_specs=[pl.BlockSpec((tm, d), row), pl.BlockSpec((tm, d), row), w.spec((d, d), lambda i: (0, 0)),
                  pl.BlockSpec((1, d), lambda i: (0, 0)), pl.BlockSpec((1, d), lambda i: (0, 0))],
        out_specs=(pl.BlockSpec((tm, d), row), pl.BlockSpec((tm, d), row)),
        out_shape=(jax.ShapeDtypeStruct((m, d), F32), jax.ShapeDtypeStruct((m, d), BF16)),
        compiler_params=_cparams(("arbitrary",), *blk),
        name="out_ln",
    )(x, merged, w.arr, g.reshape(1, d), b.reshape(1, d))


def _rope(x, cos, sin):
    half = x.shape[-1] // 2
    x1, x2 = x[:, :half], x[:, half:]
    return jnp.concatenate([x1 * cos - x2 * sin, x1 * sin + x2 * cos], axis=-1)


def _group_norm_gate(o, gate):
    mu = jnp.mean(o, axis=-1, keepdims=True)
    oc = o - mu
    var = jnp.mean(oc * oc, axis=-1, keepdims=True)
    return gate * jax.nn.sigmoid(gate) * (oc * lax.rsqrt(var + GN_EPS))


def _ret_kernel(qk_ref, v_ref, g_ref, cos_ref, sin_ref, o_ref, st_ref):
    @pl.when(pl.program_id(1) == 0)
    def _():
        st_ref[...] = jnp.zeros_like(st_ref)

    c = qk_ref.shape[0]
    cos, sin = cos_ref[...], sin_ref[...]
    ri = lax.broadcasted_iota(I32, (c, c), 0).astype(F32)
    ci = lax.broadcasted_iota(I32, (c, c), 1).astype(F32)
    diff = ri - ci
    ti = lax.broadcasted_iota(I32, (c, 1), 0).astype(F32)
    for h in range(RET_HEADS):
        lg = math.log(1.0 - 2.0 ** (-5.0 - h))
        dmask = jnp.where(diff >= 0, jnp.exp(jnp.maximum(diff, 0.0) * lg), 0.0)
        q_dec = jnp.exp((ti + 1.0) * lg)
        k_dec = jnp.exp((c - 1.0 - ti) * lg)
        c_dec = math.exp(c * lg)
        q = _rope(qk_ref[:, h * RET_DK:(h + 1) * RET_DK], cos, sin)
        k = _rope(qk_ref[:, (RET_HEADS + h) * RET_DK:(RET_HEADS + h + 1) * RET_DK], cos, sin) * (RET_DK ** -0.5)
        v = v_ref[:, h * RET_DV:(h + 1) * RET_DV]
        s = st_ref[0, h]
        qb = q.astype(BF16)
        sc = _dot_nt(qb, k.astype(BF16)) * dmask
        o = _dot(sc.astype(BF16), v) + _dot(qb, s.astype(BF16)) * q_dec
        kd_t = jnp.transpose(k * k_dec).astype(BF16)
        st_ref[0, h] = c_dec * s + _dot(kd_t, v)
        o_ref[:, h * RET_DV:(h + 1) * RET_DV] = _group_norm_gate(
            o, g_ref[:, h * RET_DV:(h + 1) * RET_DV]).astype(o_ref.dtype)


def _retention(qk, v, gate, cos, sin, bn):
    m = qk.shape[0]
    length = m // bn
    c = RET_CHUNK
    nc = length // c
    hv = RET_HEADS * RET_DV
    row = lambda b, j: (b * nc + j, 0)
    blk = [_nbytes((c, qk.shape[1]), F32), _nbytes((c, hv), BF16), _nbytes((c, hv), F32), _nbytes((c, hv), BF16),
           _nbytes((RET_HEADS, RET_DK, RET_DV), F32)]
    return pl.pallas_call(
        _ret_kernel, grid=(bn, nc),
        in_specs=[pl.BlockSpec((c, qk.shape[1]), row), pl.BlockSpec((c, hv), row), pl.BlockSpec((c, hv), row),
                  pl.BlockSpec((c, RET_DK // 2), lambda b, j: (j, 0)),
                  pl.BlockSpec((c, RET_DK // 2), lambda b, j: (j, 0))],
        out_specs=(pl.BlockSpec((c, hv), row),
                   pl.BlockSpec((1, RET_HEADS, RET_DK, RET_DV), lambda b, j: (b, 0, 0, 0))),
        out_shape=(jax.ShapeDtypeStruct((m, hv), BF16),
                   jax.ShapeDtypeStruct((bn, RET_HEADS, RET_DK, RET_DV), F32)),
        compiler_params=_cparams(("arbitrary", "arbitrary"), *blk),
        name="retention",
    )(qk, v, gate, cos, sin)


def _ret_step_kernel(qk_ref, v_ref, g_ref, cos_ref, sin_ref, s0_ref, o_ref, st_ref):
    row = pl.ds(pl.program_id(0), 1)
    cos, sin = cos_ref[...], sin_ref[...]
    for h in range(RET_HEADS):
        decay = 1.0 - 2.0 ** (-5.0 - h)
        q = _rope(qk_ref[row, h * RET_DK:(h + 1) * RET_DK], cos, sin)
        k = _rope(qk_ref[row, (RET_HEADS + h) * RET_DK:(RET_HEADS + h + 1) * RET_DK], cos, sin) * (RET_DK ** -0.5)
        v = v_ref[row, h * RET_DV:(h + 1) * RET_DV]
        s0 = s0_ref[0, h]
        st_ref[0, h] = decay * s0 + _col_of_row(k) * v
        o = (jnp.sum(q * k, axis=1, keepdims=True) * v
             + jnp.sum(_col_of_row(q) * s0, axis=0, keepdims=True) * decay)
        o_ref[0, :, h * RET_DV:(h + 1) * RET_DV] = _group_norm_gate(o, g_ref[row, h * RET_DV:(h + 1) * RET_DV])


def _retention_step(qk, v, gate, cos, sin, s0):
    bn = s0.shape[0]
    const = lambda b: (0, 0)
    state = pl.BlockSpec((1,) + s0.shape[1:], lambda b: (b, 0, 0, 0))
    return pl.pallas_call(
        _ret_step_kernel, grid=(bn,),
        in_specs=[pl.BlockSpec(qk.shape, const), pl.BlockSpec(v.shape, const), pl.BlockSpec(gate.shape, const),
                  pl.BlockSpec(cos.shape, const), pl.BlockSpec(sin.shape, const), state],
        out_specs=(pl.BlockSpec((1, 1, v.shape[1]), lambda b: (b, 0, 0)), state),
        out_shape=(jax.ShapeDtypeStruct((bn, 1, v.shape[1]), F32), jax.ShapeDtypeStruct(s0.shape, F32)),
        compiler_params=_cparams(("arbitrary",), 2 * _nbytes(s0.shape[1:], F32)),
        name="retention_step",
    )(qk, v, gate, cos, sin, s0)


def _s5_tables(a_re, a_im, log_dt, b_re, b_im, c_re, c_im, d):
    g, p = a_re.shape
    nc = b_re.shape[-1]
    no = g // S5_OCT
    t = S5_CHUNK
    dt = jnp.exp(log_dt)[:, None]
    mag = jnp.exp(a_re * dt)
    ab_re = mag * jnp.cos(a_im * dt)
    ab_im = mag * jnp.sin(a_im * dt)
    den = a_re * a_re + a_im * a_im
    x_re = ab_re - 1.0
    f_re = (x_re * a_re + ab_im * a_im) / den
    f_im = (ab_im * a_re - x_re * a_im) / den
    bb_re = f_re[..., None] * b_re - f_im[..., None] * b_im
    bb_im = f_re[..., None] * b_im + f_im[..., None] * b_re

    def powers(n):
        n = n.astype(F32)[:, None, None]
        pmag = jnp.exp(n * (a_re * dt)[None])
        return pmag * jnp.cos(n * (a_im * dt)[None]), pmag * jnp.sin(n * (a_im * dt)[None])

    pw_re, pw_im = powers(jnp.arange(t + 1))
    rev_re, rev_im = powers(t - 1 - jnp.arange(t))
    bt_re, bt_im = jnp.swapaxes(bb_re, 1, 2), jnp.swapaxes(bb_im, 1, 2)

    def packed(v_re, v_im):
        out = []
        for v in (v_re, v_im):
            v = v.reshape(v.shape[0], no, S5_OCT, nc, p)
            out.append(jnp.swapaxes(v, 0, 1).reshape(no, v.shape[0] * LANES, p))
        return jnp.concatenate(out, axis=-1)

    def b_packed(power_re, power_im):
        pr, pi = power_re[:, :, None, :], power_im[:, :, None, :]
        return packed(pr * bt_re[None] - pi * bt_im[None], pr * bt_im[None] + pi * bt_re[None])

    def c_packed(power_re, power_im):
        pr, pi = power_re[:, :, None, :], power_im[:, :, None, :]
        return packed(c_re[None] * pr - c_im[None] * pi, -(c_re[None] * pi + c_im[None] * pr))

    def state_row(v_re, v_im):
        return jnp.concatenate([v_re.reshape(no, 1, S5_OCT * p), v_im.reshape(no, 1, S5_OCT * p)], axis=-1)

    d_row = d.reshape(no, 1, LANES)
    return dict(
        b_end=b_packed(rev_re, rev_im),
        c_in=c_packed(pw_re[1:], pw_im[1:]),
        a_row=state_row(pw_re[t], pw_im[t]),
        d_row=jnp.tile(d_row, (1, 1, t)),
        b_one=b_packed(pw_re[:1], pw_im[:1]),
        c_one=c_packed(pw_re[:1], pw_im[:1]),
        a_one=state_row(ab_re, ab_im),
        d_one=d_row,
    )


def _expand_groups(packed_ref, rows):
    n = rows.stop - rows.start
    g_row = (lax.broadcasted_iota(I32, (n, LANES), 0) // S5_GROUP) % S5_OCT
    g_half = lax.broadcasted_iota(I32, (n, LANES), 1) // S5_STATE
    x = packed_ref[0, rows, :]
    swapped = pltpu.roll(x, S5_STATE, axis=1)
    tiles = []
    for src in (jnp.where(g_half == 0, x, swapped), jnp.where(g_half == 0, swapped, x)):
        for pair in range(S5_OCT // 2):
            tiles.append(jnp.where(g_row == 2 * pair + g_half, src, 0.0))
    return jnp.concatenate(tiles, axis=-1)


def _s5_kernel(u_ref, cone_ref, bend_ref, cin_ref, a_ref, d_ref, y_ref, h_ref,
               toep_scr, bfull_scr, cfull_scr, e_scr, s_scr):
    rows, t = e_scr.shape[0], S5_CHUNK

    @pl.when(pl.program_id(1) == 0)
    def _():
        toep_scr[...] = jnp.zeros_like(toep_scr)
        same_group = (lax.broadcasted_iota(I32, (LANES, LANES), 0) // S5_GROUP
                      == lax.broadcasted_iota(I32, (LANES, LANES), 1) // S5_GROUP)
        for s in range(t):
            sl = slice(s * LANES, (s + 1) * LANES)
            bfull_scr[sl, :] = _expand_groups(bend_ref, sl).astype(BF16)
            cfull_scr[sl, :] = _expand_groups(cin_ref, sl).astype(BF16)
            lag = t - 1 - s
            blk = lax.dot_general(bend_ref[0, sl, :], cone_ref[0], (((1,), (1,)), ((), ())),
                                  precision=lax.Precision.HIGHEST, preferred_element_type=F32)
            blk = jnp.where(same_group, blk, 0.0).astype(BF16)
            for r in range(t - lag):
                toep_scr[r * LANES:(r + 1) * LANES, (r + lag) * LANES:(r + lag + 1) * LANES] = blk

    u = jnp.concatenate([u_ref[0, pl.ds(i, rows, stride=t), :] for i in range(t)], axis=-1)
    ub = u.astype(BF16)
    e_scr[...] = _dot(ub, bfull_scr[...])
    half = a_ref.shape[-1] // 2
    a_re, a_im = a_ref[0, :, :half], a_ref[0, :, half:]

    def step(k, carry):
        s_re, s_im = carry
        s_scr[pl.ds(k, 1), :] = jnp.concatenate([s_re, s_im], axis=-1)
        e = e_scr[pl.ds(k, 1), :]
        return (a_re * s_re - a_im * s_im + e[:, :half], a_re * s_im + a_im * s_re + e[:, half:])

    zero = jnp.zeros((1, half), F32)
    s_re, s_im = lax.fori_loop(0, rows, step, (zero, zero))
    h_ref[0, 0] = jnp.concatenate([s_re, s_im], axis=-1)
    y = jax.nn.gelu(_dot(ub, toep_scr[...]) + _dot_nt(s_scr[...].astype(BF16), cfull_scr[...]) + d_ref[0] * u)
    for i in range(t):
        y_ref[0, pl.ds(i, rows, stride=t), :] = y[:, i * LANES:(i + 1) * LANES]


def _s5(u_oct, tb, bn):
    no, m, _ = u_oct.shape
    t = S5_CHUNK
    steps = m // bn
    rows = steps // t
    w = t * LANES
    ns = tb['a_row'].shape[-1]
    pk = tb['b_end'].shape[-1]
    oct_blk = lambda shape: pl.BlockSpec((1,) + shape, lambda o, b: (o,) + (0,) * len(shape))
    blk = [2 * _nbytes((rows, w), F32), 2 * _nbytes((w, pk), F32),
           (_nbytes((w, w), BF16) + 2 * _nbytes((w, ns), BF16) + 2 * _nbytes((rows, ns), F32)) // 2]
    return pl.pallas_call(
        _s5_kernel, grid=(no, bn),
        in_specs=[pl.BlockSpec((1, steps, LANES), lambda o, b: (o, b, 0)),
                  oct_blk((LANES, pk)), oct_blk((w, pk)), oct_blk((w, pk)), oct_blk((1, ns)),
                  oct_blk((1, w))],
        out_specs=(pl.BlockSpec((1, steps, LANES), lambda o, b: (o, b, 0)),
                   pl.BlockSpec((1, 1, 1, ns), lambda o, b: (b, o, 0, 0))),
        out_shape=(jax.ShapeDtypeStruct(u_oct.shape, F32), jax.ShapeDtypeStruct((bn, no, 1, ns), F32)),
        scratch_shapes=[pltpu.VMEM((w, w), BF16), pltpu.VMEM((w, ns), BF16), pltpu.VMEM((w, ns), BF16),
                        pltpu.VMEM((rows, ns), F32), pltpu.VMEM((rows, ns), F32)],
        compiler_params=_cparams(("arbitrary", "arbitrary"), *blk),
        name="s5_scan",
    )(u_oct, tb['c_one'], tb['b_end'], tb['c_in'], tb['a_row'], tb['d_row'])


def _s5_step_kernel(u_ref, h0_ref, b_ref, c_ref, a_ref, d_ref, y_ref, h_ref):
    u = u_ref[0]
    half = a_ref.shape[-1] // 2
    a_re, a_im = a_ref[0, :, :half], a_ref[0, :, half:]
    h0 = h0_ref[0]
    h0_re, h0_im = h0[:, :half], h0[:, half:]
    whole = slice(0, LANES)
    bu = _dot(u.astype(BF16), _expand_groups(b_ref, whole).astype(BF16))
    h_re = a_re * h0_re - a_im * h0_im + bu[:, :half]
    h_im = a_re * h0_im + a_im * h0_re + bu[:, half:]
    h = jnp.concatenate([h_re, h_im], axis=-1)
    h_ref[0] = h
    y_ref[0] = jax.nn.gelu(_dot_nt(h.astype(BF16), _expand_groups(c_ref, whole).astype(BF16)) + d_ref[0] * u)


def _s5_step(u_oct, h0, tb):
    no, rows, _ = u_oct.shape
    ns = h0.shape[-1]
    pk = tb['b_one'].shape[-1]
    o3 = lambda shape: pl.BlockSpec((1,) + shape, lambda o: (o, 0, 0))
    return pl.pallas_call(
        _s5_step_kernel, grid=(no,),
        in_specs=[o3((rows, LANES)), o3((rows, ns)), o3((LANES, pk)), o3((LANES, pk)), o3((1, ns)), o3((1, LANES))],
        out_specs=(o3((rows, LANES)), o3((rows, ns))),
        out_shape=(jax.ShapeDtypeStruct(u_oct.shape, F32), jax.ShapeDtypeStruct(h0.shape, F32)),
        compiler_params=_cparams(("arbitrary",), _nbytes((LANES, ns), BF16) * 2),
        name="s5_step",
    )(u_oct, h0, tb['b_one'], tb['c_one'], tb['a_one'], tb['d_one'])


def _sort_key(x):
    bits = pltpu.bitcast(x, I32)
    return jnp.where(bits >= 0, bits, bits ^ jnp.int32(0x7FFFFFFF))


def _kth_largest_key(count_ge, nsel, shape):
    res = jnp.where(count_ge(jnp.zeros(shape, I32)) >= nsel, jnp.int32(0), jnp.int32(INT_MIN))

    def bit_step(i, res):
        cand = res | jnp.left_shift(jnp.int32(1), 30 - i)
        return jnp.where(count_ge(cand) >= nsel, cand, res)

    return lax.fori_loop(0, 31, bit_step, res)


def _dsa_kernel(dq_ref, iq_ref, ikw_ref, ikb_ref, kv_ref, o_ref, key_scr, bias_scr, m_scr, l_scr, acc_scr,
                *, nsel):
    tq = dq_ref.shape[0]
    jq = pl.program_id(1)
    nkb = jq + 1
    krow = lax.broadcasted_iota(I32, (tq, tq), 0)
    qcol = lax.broadcasted_iota(I32, (tq, tq), 1)
    qpos = jq * tq + qcol
    hd = DSA_HEAD_DIM
    gsz = DSA_HEADS // DSA_KV_HEADS

    def over_keys(x, op):
        part = op(x.reshape(x.shape[0] // SUBLANES, SUBLANES, x.shape[1]), axis=0)
        return op(part, axis=0, keepdims=True)

    iq_st = jnp.concatenate([iq_ref[:, h * LANES:(h + 1) * LANES] for h in range(IDX_HEADS)], axis=0)
    w_t = jnp.transpose(ikw_ref[...])

    def score_block(kb, carry):
        ik = ikb_ref[pl.ds(pl.multiple_of(kb * tq, tq), tq), :]
        sh = jnp.maximum(_dot_nt(ik, iq_st), 0.0)
        acc = jnp.zeros((tq, tq), F32)
        for h in range(IDX_HEADS):
            acc = acc + sh[:, h * tq:(h + 1) * tq] * w_t[IDX_DIM + h:IDX_DIM + h + 1, :]
        acc = jnp.where(kb * tq + krow <= qpos, acc, -jnp.inf)
        key_scr[kb] = _sort_key(acc)
        return carry

    lax.fori_loop(0, nkb, score_block, 0)

    acc_rows = 4 * SUBLANES

    def count(pred):
        def body(kb, c):
            hit = jnp.where(pred(key_scr[kb]), 1.0, 0.0)
            return c + jnp.sum(hit.reshape(tq // acc_rows, acc_rows, tq), axis=0)
        part = lax.fori_loop(0, nkb, body, jnp.zeros((acc_rows, tq), F32))
        return jnp.sum(part, axis=0, keepdims=True)

    kth = _kth_largest_key(lambda cand: count(lambda key: key >= cand), float(nsel), (1, tq))
    at_least = count(lambda key: key >= kth)

    def select_all_ge():
        def block(kb, carry):
            sel = (key_scr[kb] >= kth) & (kb * tq + krow <= qpos)
            bias_scr[kb] = jnp.where(sel, 0.0, MASK_NEG)
            return carry
        lax.fori_loop(0, nkb, block, 0)

    def select_with_ties():
        need = float(nsel) - count(lambda key: key > kth)
        tri = jnp.where(krow >= qcol, 1.0, 0.0).astype(BF16)

        def block(kb, taken):
            key = key_scr[kb]
            tie = jnp.where(key == kth, 1.0, 0.0)
            rank = taken + _dot(tri, tie.astype(BF16))
            sel = ((key > kth) | ((tie > 0.0) & (rank <= need))) & (kb * tq + krow <= qpos)
            bias_scr[kb] = jnp.where(sel, 0.0, MASK_NEG)
            return taken + over_keys(tie, jnp.sum)

        lax.fori_loop(0, nkb, block, jnp.zeros((1, tq), F32))

    lax.cond(jnp.max(at_least) > float(nsel), select_with_ties, select_all_ge)

    scale2 = (hd ** -0.5) * math.log2(math.e)
    qgs = [jnp.concatenate([dq_ref[:, (g * gsz + i) * hd:(g * gsz + i + 1) * hd] for i in range(gsz)], axis=0)
           for g in range(DSA_KV_HEADS)]
    m_scr[...] = jnp.full_like(m_scr, -jnp.inf)
    l_scr[...] = jnp.zeros_like(l_scr)
    acc_scr[...] = jnp.zeros_like(acc_scr)

    def attend_block(kb, carry):
        ks = pl.ds(pl.multiple_of(kb * tq, tq), tq)
        bias = jnp.concatenate([bias_scr[kb]] * gsz, axis=1)
        for g in range(DSA_KV_HEADS):
            kblk = kv_ref[ks, g * hd:(g + 1) * hd]
            vblk = kv_ref[ks, (DSA_KV_HEADS + g) * hd:(DSA_KV_HEADS + g + 1) * hd]
            lg = _dot_nt(kblk, qgs[g]) * scale2 + bias
            m_old = m_scr[g]
            m_new = jnp.maximum(m_old, over_keys(lg, jnp.max))
            alpha = jnp.exp2(m_old - m_new)
            p = jnp.exp2(lg - m_new)
            l_scr[g] = alpha * l_scr[g] + over_keys(p, jnp.sum)
            v_t = jnp.transpose(vblk.astype(F32)).astype(BF16)
            acc_scr[g] = alpha * acc_scr[g] + _dot(v_t, p.astype(BF16))
            m_scr[g] = m_new
        return carry

    lax.fori_loop(0, nkb, attend_block, 0)
    for g in range(DSA_KV_HEADS):
        out_t = acc_scr[g] / l_scr[g]
        for i in range(gsz):
            o_ref[:, (g * gsz + i) * hd:(g * gsz + i + 1) * hd] = jnp.transpose(
                out_t[:, i * tq:(i + 1) * tq]).astype(o_ref.dtype)


def _dsa_prompt(dq, iq, ikw, ikb, kvb, bn):
    m = dq.shape[0]
    length = m // bn
    tq = 256
    nq = length // tq
    nsel = min(DSA_TOPK, length // 4)
    qrow = lambda b, j: (b * nq + j, 0)
    full = lambda b, j: (b, 0)
    gsz = DSA_HEADS // DSA_KV_HEADS
    blk = [_nbytes((tq, dq.shape[1]), BF16) * 3, _nbytes((length, LANES), BF16), _nbytes((length, kvb.shape[1]), BF16),
           _nbytes((nq, tq, tq), F32)]
    return pl.pallas_call(
        functools.partial(_dsa_kernel, nsel=nsel), grid=(bn, nq),
        in_specs=[pl.BlockSpec((tq, dq.shape[1]), qrow), pl.BlockSpec((tq, iq.shape[1]), qrow),
                  pl.BlockSpec((tq, LANES), qrow), pl.BlockSpec((length, LANES), full),
                  pl.BlockSpec((length, kvb.shape[1]), full)],
        out_specs=pl.BlockSpec((tq, dq.shape[1]), qrow),
        out_shape=jax.ShapeDtypeStruct(dq.shape, BF16),
        scratch_shapes=[pltpu.VMEM((nq, tq, tq), I32), pltpu.VMEM((nq, tq, tq), F32),
                        pltpu.VMEM((DSA_KV_HEADS, 1, gsz * tq), F32), pltpu.VMEM((DSA_KV_HEADS, 1, gsz * tq), F32),
                        pltpu.VMEM((DSA_KV_HEADS, DSA_HEAD_DIM, gsz * tq), F32)],
        compiler_params=_cparams(("arbitrary", "arbitrary"), *blk),
        name="dsa_prompt",
    )(dq, iq, ikw, ikb, kvb)


MAX_PAGES_PER_STEP = 32


def _idx_heads(iq_row):
    return jnp.concatenate([iq_row[:, h * LANES:h * LANES + IDX_DIM] for h in range(IDX_HEADS)], axis=0)


def _page_score_kernel(pt_ref, iq_ref, ikw_ref, *rest):
    page_refs, o_ref = rest[:-1], rest[-1]
    b = pl.program_id(0)
    iq_h = jnp.concatenate([_idx_heads(iq_ref[pl.ds(b, 1), :]),
                            jnp.zeros((SAMPLE_ROWS - IDX_HEADS, IDX_DIM), F32)], axis=0).astype(BF16)
    w_col = _col_of_row(ikw_ref[pl.ds(b, 1), IDX_DIM:IDX_DIM + SAMPLE_ROWS])
    pages = jnp.concatenate([page[...].astype(BF16) for page in page_refs], axis=1)
    sh = jnp.maximum(_dot(iq_h, pages), 0.0)
    sc = jnp.sum(sh * w_col, axis=0, keepdims=True)
    for i in range(len(page_refs)):
        o_ref[0, i:i + 1, :] = sc[:, i * PAGE_SIZE:(i + 1) * PAGE_SIZE]


def _page_scores(page_table, iq, ikw, cache_idx_t, layer):
    bn, npages = page_table.shape
    pg = min(MAX_PAGES_PER_STEP, npages)
    const = lambda b, s, pt: (0, 0)

    def page_spec(i):
        return pl.BlockSpec((None, None, IDX_DIM, PAGE_SIZE), lambda b, s, pt: (layer, pt[b, s * pg + i], 0, 0))

    return pl.pallas_call(
        _page_score_kernel,
        grid_spec=pltpu.PrefetchScalarGridSpec(
            num_scalar_prefetch=1, grid=(bn, npages // pg),
            in_specs=[pl.BlockSpec(iq.shape, const), pl.BlockSpec(ikw.shape, const)]
            + [page_spec(i) for i in range(pg)],
            out_specs=pl.BlockSpec((1, pg, PAGE_SIZE), lambda b, s, pt: (b, s, 0))),
        out_shape=jax.ShapeDtypeStruct((bn, npages, PAGE_SIZE), F32),
        compiler_params=_cparams(("arbitrary", "arbitrary")),
        name="page_scores",
    )(page_table, iq, ikw, *([cache_idx_t] * pg))


def _page_select_kernel(sc_ref, iq_ref, ikw_ref, bias_ref, bias_self_ref, *, nsel):
    b = pl.program_id(0)
    npages, psz = sc_ref.shape[1:]
    iq_h = _idx_heads(iq_ref[pl.ds(b, 1), :])
    ikw = ikw_ref[pl.ds(b, 1), :]
    w_col = _col_of_row(ikw[:, IDX_DIM:IDX_DIM + IDX_HEADS])
    s_self = jnp.sum(jnp.maximum(jnp.sum(iq_h * ikw[:, :IDX_DIM], axis=1, keepdims=True), 0.0) * w_col,
                     axis=0, keepdims=True)
    key = _sort_key(sc_ref[0])
    key_self = _sort_key(s_self)

    def total(x):
        return jnp.sum(jnp.sum(x, axis=1, keepdims=True), axis=0, keepdims=True)

    def count_ge(cand):
        return total(jnp.where(key >= cand, 1.0, 0.0)) + jnp.where(key_self >= cand, 1.0, 0.0)

    kth = _kth_largest_key(count_ge, float(nsel), (1, 1))
    need = float(nsel) - (total(jnp.where(key > kth, 1.0, 0.0)) + jnp.where(key_self > kth, 1.0, 0.0))
    tie = jnp.where(key == kth, 1.0, 0.0)
    r_in = lax.broadcasted_iota(I32, (psz, psz), 0)
    c_in = lax.broadcasted_iota(I32, (psz, psz), 1)
    in_page = _dot(tie.astype(BF16), jnp.where(r_in <= c_in, 1.0, 0.0).astype(BF16))
    per_page = jnp.broadcast_to(jnp.sum(tie, axis=1, keepdims=True), (npages, psz)).astype(BF16)
    r_pg = lax.broadcasted_iota(I32, (npages, npages), 0)
    c_pg = lax.broadcasted_iota(I32, (npages, npages), 1)
    before = _dot(jnp.where(c_pg < r_pg, 1.0, 0.0).astype(BF16), per_page)
    sel = (key > kth) | ((tie > 0.0) & (before + in_page <= need))
    r_dup = lax.broadcasted_iota(I32, (psz, bias_ref.shape[2]), 0)
    c_dup = lax.broadcasted_iota(I32, (psz, bias_ref.shape[2]), 1)
    dup = jnp.where(c_dup // DSA_KV_HEADS == r_dup, 1.0, 0.0).astype(BF16)
    spread = _dot(jnp.where(sel, 1.0, 0.0).astype(BF16), dup)
    bias_ref[0] = jnp.where(spread > 0.5, 0.0, MASK_NEG)
    sel_self = (key_self > kth) | ((key_self == kth) & (total(tie) + 1.0 <= need))
    bias_self_ref[0] = jnp.broadcast_to(jnp.where(sel_self, 0.0, MASK_NEG), (1, LANES))


def _page_select(scores, iq, ikw, nsel):
    bn, npages, psz = scores.shape
    const = lambda b: (0, 0)
    wide = DSA_KV_HEADS * psz
    return pl.pallas_call(
        functools.partial(_page_select_kernel, nsel=nsel), grid=(bn,),
        in_specs=[pl.BlockSpec((1, npages, psz), lambda b: (b, 0, 0)),
                  pl.BlockSpec(iq.shape, const), pl.BlockSpec(ikw.shape, const)],
        out_specs=(pl.BlockSpec((1, npages, wide), lambda b: (b, 0, 0)),
                   pl.BlockSpec((1, 1, LANES), lambda b: (b, 0, 0))),
        out_shape=(jax.ShapeDtypeStruct((bn, npages, wide), F32), jax.ShapeDtypeStruct((bn, 1, LANES), F32)),
        compiler_params=_cparams(("arbitrary",)),
        name="page_select",
    )(scores, iq, ikw)


def _page_attend_kernel(pt_ref, dq_ref, kvs_ref, bias_ref, bself_ref, *rest, npg):
    k_refs, v_refs = rest[:npg], rest[npg:2 * npg]
    o_ref, m_scr, l_scr, acc_scr = rest[2 * npg:]
    b = pl.program_id(0)
    s = pl.program_id(1)
    hd = DSA_HEAD_DIM
    gsz = DSA_HEADS // DSA_KV_HEADS
    scale = hd ** -0.5
    prow = k_refs[0].shape[0]
    dq_row = dq_ref[pl.ds(b, 1), :]
    q = jnp.concatenate([dq_row[:, h * hd:(h + 1) * hd] for h in range(DSA_HEADS)]
                        + [jnp.zeros((SAMPLE_ROWS - DSA_HEADS, hd), F32)], axis=0)
    kv_of_head = lax.broadcasted_iota(I32, (SAMPLE_ROWS, 1), 0) // gsz

    @pl.when(s == 0)
    def _():
        m_scr[...] = jnp.full_like(m_scr, -jnp.inf)
        l_scr[...] = jnp.zeros_like(l_scr)
        acc_scr[...] = jnp.zeros_like(acc_scr)

    def update(lg, pv_of):
        m_old = m_scr[...]
        m_new = jnp.maximum(m_old, jnp.max(lg, axis=1, keepdims=True))
        alpha = jnp.exp(m_old - m_new)
        p = jnp.exp(lg - m_new)
        l_scr[...] = alpha * l_scr[...] + jnp.sum(p, axis=1, keepdims=True)
        acc_scr[...] = alpha * acc_scr[...] + pv_of(p)
        m_scr[...] = m_new

    qb = q.astype(BF16)
    lg = jnp.concatenate([_dot_nt(qb, k_refs[i][...].astype(BF16)) for i in range(npg)], axis=1) * scale
    bias = jnp.concatenate([bias_ref[0, i:i + 1, :] for i in range(npg)], axis=1)
    col = lax.broadcasted_iota(I32, lg.shape, 1)
    lg = jnp.where(jnp.bitwise_and(col, DSA_KV_HEADS - 1) == kv_of_head, lg + bias, MASK_NEG)

    def pv_pages(p):
        pb = p.astype(BF16)
        acc = jnp.zeros((SAMPLE_ROWS, hd), F32)
        for i in range(npg):
            acc = acc + _dot(pb[:, i * prow:(i + 1) * prow], v_refs[i][...].astype(BF16))
        return acc

    update(lg, pv_pages)

    @pl.when(s == pl.num_programs(1) - 1)
    def _():
        kvs = kvs_ref[pl.ds(b, 1), :]
        k_self = jnp.where(kv_of_head == 0, kvs[:, :hd], kvs[:, hd:2 * hd])
        v_self = jnp.where(kv_of_head == 0, kvs[:, 2 * hd:3 * hd], kvs[:, 3 * hd:])
        lg_self = jnp.sum(q * k_self, axis=1, keepdims=True) * scale + bself_ref[0, :, :1]
        update(lg_self, lambda p: p * v_self)
        out = acc_scr[...] / l_scr[...]
        for h in range(DSA_HEADS):
            o_ref[0, :, h * hd:(h + 1) * hd] = out[h:h + 1]


def _page_attend(page_table, dq, kvs, bias, bias_self, cache_k, cache_v, layer):
    assert DSA_KV_HEADS == 2
    bn, npages = page_table.shape
    pg = min(MAX_PAGES_PER_STEP, npages)
    prow, hd = cache_k.shape[-2:]
    const = lambda b, s, pt: (0, 0)

    def page_spec(i):
        return pl.BlockSpec((None, None, prow, hd), lambda b, s, pt: (layer, pt[b, s * pg + i], 0, 0))

    return pl.pallas_call(
        functools.partial(_page_attend_kernel, npg=pg),
        grid_spec=pltpu.PrefetchScalarGridSpec(
            num_scalar_prefetch=1, grid=(bn, npages // pg),
            in_specs=[pl.BlockSpec(dq.shape, const), pl.BlockSpec(kvs.shape, const),
                      pl.BlockSpec((1, pg, prow), lambda b, s, pt: (b, s, 0)),
                      pl.BlockSpec((1, 1, LANES), lambda b, s, pt: (b, 0, 0))]
            + [page_spec(i) for i in range(pg)] + [page_spec(i) for i in range(pg)],
            out_specs=pl.BlockSpec((1, 1, dq.shape[1]), lambda b, s, pt: (b, 0, 0)),
            scratch_shapes=[pltpu.VMEM((SAMPLE_ROWS, 1), F32), pltpu.VMEM((SAMPLE_ROWS, 1), F32),
                            pltpu.VMEM((SAMPLE_ROWS, hd), F32)]),
        out_shape=jax.ShapeDtypeStruct((bn, 1, dq.shape[1]), F32),
        compiler_params=_cparams(("arbitrary", "arbitrary"), 2 * pg * _nbytes((prow, hd), F32)),
        name="page_attend",
    )(page_table, dq, kvs, bias, bias_self, *([cache_k] * pg), *([cache_v] * pg))


def _rope_tables(pos):
    half = RET_DK // 2
    inv = 1.0 / (ROPE_BASE ** jnp.linspace(0.0, 1.0, half, dtype=F32))
    ang = pos.astype(F32)[:, None] * inv[None, :]
    return jnp.cos(ang), jnp.sin(ang)


def _stacked_bf16(a):
    arr = a.astype(BF16)
    return lambda l: _LayerWeight(arr, l, 0, arr.shape[2])


def _in_proj_weights(w_in):
    depth, d, _ = w_in.shape
    hq = RET_HEADS * RET_DK
    hv = RET_HEADS * RET_DV
    o = 0
    cuts = {}
    for name, width in (('qk', 2 * hq), ('v', hv), ('g', hv)):
        cuts[name] = (o, width)
        o += width
    s5w = (w_in.shape[2] - 2 * hq - 2 * hv - DSA_HEADS * DSA_HEAD_DIM - 2 * DSA_KV_HEADS * DSA_HEAD_DIM
           - IDX_HEADS * IDX_DIM - IDX_DIM - IDX_HEADS) // (1 + 2 * N_BRANCHES)
    for name, width in (('su', s5w), ('dq', DSA_HEADS * DSA_HEAD_DIM), ('kv', 2 * DSA_KV_HEADS * DSA_HEAD_DIM),
                        ('iq', IDX_HEADS * IDX_DIM), ('ikw', IDX_DIM + IDX_HEADS), ('gates', N_BRANCHES * d)):
        cuts[name] = (o, width)
        o += width
    assert o == w_in.shape[2]
    cut = lambda name: w_in[:, :, cuts[name][0]:cuts[name][0] + cuts[name][1]]
    iq = jnp.pad(cut('iq').reshape(depth, d, IDX_HEADS, IDX_DIM), ((0, 0), (0, 0), (0, 0), (0, LANES - IDX_DIM)))
    repacked = {'iq': iq.reshape(depth, d, IDX_HEADS * LANES),
                'ikw': jnp.pad(cut('ikw'), ((0, 0), (0, 0), (0, LANES - IDX_DIM - IDX_HEADS))),
                'gates': cut('gates')}

    def layer(l):
        wb = {k: _LayerWeight(w_in, l, *cuts[k]) for k in ('qk', 'v', 'g', 'su', 'dq', 'kv')}
        wb.update({k: _LayerWeight(a, l, 0, a.shape[2]) for k, a in repacked.items()})
        return wb

    return layer


def _project(xb, wb, sample):
    act = F32 if sample else BF16
    p = {
        'qk': _mm(xb, wb['qk'], (F32,), name="proj_qk"),
        'v': _mm(xb, wb['v'], (act,), name="proj_v"),
        'g': _mm(xb, wb['g'], (F32,), name="proj_g"),
        'su': _mm(xb, wb['su'], None, oct_layout=True, name="proj_su"),
        'dq': _mm(xb, wb['dq'], (act,), name="proj_dq"),
        'iq': _mm(xb, wb['iq'], (act,), name="proj_iq"),
        'gates': _mm(xb, wb['gates'], (BF16,), gate=True, name="proj_gates"),
    }
    p['kv'], p['kvb'] = _mm(xb, wb['kv'], (F32, BF16), name="proj_kv")
    p['ikw'], p['ikb'] = _mm(xb, wb['ikw'], (F32, BF16), name="proj_ikw")
    return p


def _s5_state_out(h, groups):
    bn = h.shape[0]
    h = h.reshape(bn, groups // S5_OCT, 2, S5_OCT, S5_STATE)
    return h[:, :, 0].reshape(bn, groups, S5_STATE), h[:, :, 1].reshape(bn, groups, S5_STATE)


def kernel(x_prompt, x_sample, cache_k, cache_v, cache_idx_k, state_ret, state_s5_re, state_s5_im, page_table, ln1_g, ln1_b, ffn1_wg, ffn1_wu, ffn1_wd, w_in, s5_a_re, s5_a_im, s5_log_dt, s5_b_re, s5_b_im, s5_c_re, s5_c_im, s5_d, w_glu, w_ret_o, w_s5_o, w_dsa_o, w_out, ln2_g, ln2_b, ffn2_wg, ffn2_wu, ffn2_wd, ln3_g, ln3_b):
    bp, lp, d = x_prompt.shape
    bs, ls, _ = x_sample.shape
    depth = w_in.shape[0]
    assert ls == 1 and bs <= SAMPLE_ROWS
    npages = page_table.shape[1]
    past = npages * PAGE_SIZE
    groups = s5_a_re.shape[1]
    alpha = (2 * depth) ** 0.25
    kvw = DSA_KV_HEADS * DSA_HEAD_DIM
    nsel_s = min(DSA_TOPK, (past + ls) // 4)

    cos_p, sin_p = _rope_tables(jnp.arange(lp))
    cos_s, sin_s = _rope_tables(past + jnp.arange(ls))
    cache_k = cache_k.reshape(depth, -1, PAGE_SIZE * DSA_KV_HEADS, DSA_HEAD_DIM)
    cache_v = cache_v.reshape(depth, -1, PAGE_SIZE * DSA_KV_HEADS, DSA_HEAD_DIM)
    cache_idx_t = jnp.swapaxes(cache_idx_k, 2, 3)

    in_proj = _in_proj_weights(w_in.astype(BF16))
    ffn1 = [_stacked_bf16(a) for a in (ffn1_wg, ffn1_wu, ffn1_wd)]
    ffn2 = [_stacked_bf16(a) for a in (ffn2_wg, ffn2_wu, ffn2_wd)]
    mixer_out = [_stacked_bf16(a) for a in (w_glu, w_ret_o, w_s5_o, w_dsa_o, w_out)]

    xp = x_prompt.reshape(bp * lp, d)
    xs = jnp.pad(x_sample.reshape(bs, d), ((0, SAMPLE_ROWS - bs), (0, 0)))
    outs_p, outs_s = [], []
    for l in range(depth):
        wb = in_proj(l)
        tb = _s5_tables(s5_a_re[l], s5_a_im[l], s5_log_dt[l], s5_b_re[l], s5_b_im[l], s5_c_re[l], s5_c_im[l],
                        s5_d[l])
        w1 = [w(l) for w in ffn1]
        w2 = [w(l) for w in ffn2]
        wglu, wro, wso, wdo, wo = [w(l) for w in mixer_out]

        xp, xpb = _ffn_ln(xp, *w1, ln1_g[l], ln1_b[l], alpha)
        p = _project(xpb, wb, sample=False)
        o_ret, ret_p = _retention(p['qk'], p['v'], p['g'], cos_p, sin_p, bp)
        y_s5, h_p = _s5(p['su'], tb, bp)
        z = _glu(y_s5, wglu)
        att = _dsa_prompt(p['dq'], p['iq'], p['ikw'], p['ikb'], p['kvb'], bp)
        merged = _merge(o_ret, z, att, p['gates'], wro, wso, wdo)
        xp, xpb = _out_ln(xp, merged, wo, ln2_g[l], ln2_b[l], alpha)
        xp, xpb = _ffn_ln(xp, *w2, ln3_g[l], ln3_b[l], alpha)
        s5r_p, s5i_p = _s5_state_out(h_p, groups)
        outs_p.append((p['kv'][:, :kvw].reshape(bp, lp, DSA_KV_HEADS, DSA_HEAD_DIM),
                       p['kv'][:, kvw:].reshape(bp, lp, DSA_KV_HEADS, DSA_HEAD_DIM),
                       p['ikw'][:, :IDX_DIM].reshape(bp, lp, IDX_DIM), ret_p, s5r_p, s5i_p))

        xs, xsb = _ffn_ln(xs, *w1, ln1_g[l], ln1_b[l], alpha)
        q = _project(xsb, wb, sample=True)
        o_ret_s, ret_s = _retention_step(q['qk'], q['v'], q['g'], cos_s, sin_s, state_ret[l])
        o_ret_s = jnp.pad(o_ret_s[:, 0], ((0, SAMPLE_ROWS - bs), (0, 0))).astype(BF16)
        h0 = jnp.concatenate([state_s5_re[l].reshape(bs, groups // S5_OCT, S5_OCT * S5_STATE),
                              state_s5_im[l].reshape(bs, groups // S5_OCT, S5_OCT * S5_STATE)], axis=-1)
        h0 = jnp.pad(jnp.swapaxes(h0, 0, 1), ((0, 0), (0, SAMPLE_ROWS - bs), (0, 0)))
        y_s, h_s = _s5_step(q['su'], h0, tb)
        z_s = _glu(y_s, wglu)
        scores = _page_scores(page_table, q['iq'], q['ikw'], cache_idx_t, l)
        bias, bias_self = _page_select(scores, q['iq'], q['ikw'], nsel_s)
        att_s = _page_attend(page_table, q['dq'], q['kv'], bias, bias_self, cache_k, cache_v, l)
        att_s = jnp.pad(att_s[:, 0], ((0, SAMPLE_ROWS - bs), (0, 0))).astype(BF16)
        merged_s = _merge(o_ret_s, z_s, att_s, q['gates'], wro, wso, wdo)
        xs, xsb = _out_ln(xs, merged_s, wo, ln2_g[l], ln2_b[l], alpha)
        xs, xsb = _ffn_ln(xs, *w2, ln3_g[l], ln3_b[l], alpha)
        s5r_s, s5i_s = _s5_state_out(jnp.swapaxes(h_s, 0, 1)[:bs, :, None, :], groups)
        outs_s.append((q['kv'][:bs, :kvw].reshape(bs, ls, DSA_KV_HEADS, DSA_HEAD_DIM),
                       q['kv'][:bs, kvw:].reshape(bs, ls, DSA_KV_HEADS, DSA_HEAD_DIM),
                       q['ikw'][:bs, :IDX_DIM].reshape(bs, ls, IDX_DIM), ret_s, s5r_s, s5i_s))

    k_p, v_p, ik_p, ret_p, s5r_p, s5i_p = [jnp.stack(a) for a in zip(*outs_p)]
    k_s, v_s, ik_s, ret_s, s5r_s, s5i_s = [jnp.stack(a) for a in zip(*outs_s)]
    return (xp.reshape(bp, lp, d), xs[:bs].reshape(bs, ls, d), k_p, v_p, ik_p, k_s, v_s, ik_s,
            ret_p, ret_s, s5r_p, s5i_p, s5r_s, s5i_s)
```
